```python
import jax, jax.numpy as jnp
from jax import lax
import numpy as np

D_MODEL = 1024
BATCH = 2
SEQ = 8192
DEPTH = 4

N_MIXERS = 4
N_HEADS = 8
HEAD_DIM = D_MODEL // N_HEADS
D_FF = 4 * D_MODEL
ROPE_THETA = 10000.0
NORM_EPS = 1e-6
NEG_INF = -1e30

NSA_KV_GROUPS = 2
NSA_HEADS_PER_GROUP = N_HEADS // NSA_KV_GROUPS
NSA_CMP_BLOCK = 32
NSA_CMP_STRIDE = 16
NSA_SEL_BLOCK = 64
NSA_SEL_TOPK = 16
NSA_WINDOW = 512
NSA_QBLOCK = 64
NSA_FORCE_BONUS = 1000.0
NSA_IN_DIM = N_HEADS * HEAD_DIM + 6 * NSA_KV_GROUPS * HEAD_DIM + 3 * N_HEADS

SB_QBLOCK = 128

CONV_WIDTH = 31

MOBA_BLOCK = 256
MOBA_TOPK = 3
MOBA_QBLOCK = 32

N_NSA = (DEPTH + 3) // 4
N_SB = (DEPTH + 2) // 4
N_CONV = (DEPTH + 1) // 4
N_MOBA = DEPTH // 4

kernel_name = 'hybrid_nsa_stickbreak_conformer_moba_trunk'


def _rms_norm(x, g):
    xf = x.astype(jnp.float32)
    y = xf * lax.rsqrt(jnp.mean(xf * xf, axis=-1, keepdims=True) + NORM_EPS)
    return (y * g.astype(jnp.float32)).astype(x.dtype)


def _rope_tables(seq):
    half = HEAD_DIM // 2
    inv_freq = ROPE_THETA ** (-jnp.arange(half, dtype=jnp.float32) / half)
    ang = jnp.arange(seq, dtype=jnp.float32)[:, None] * inv_freq[None, :]
    return jnp.cos(ang), jnp.sin(ang)


def _apply_rope(x, cos, sin):
    xf = x.astype(jnp.float32)
    x1, x2 = jnp.split(xf, 2, axis=-1)
    c = cos[None, :, None, :]
    s = sin[None, :, None, :]
    return jnp.concatenate([x1 * c - x2 * s, x1 * s + x2 * c], axis=-1).astype(x.dtype)


def _masked_softmax(logits, mask):
    z = jnp.where(mask, logits.astype(jnp.float32), NEG_INF)
    return jnp.where(mask, jax.nn.softmax(z, axis=-1), 0.0)


def _nsa_mixer(h, w_in, q_gain, k_gain, cmp_pos, w_cmp, w_out, cos, sin):
    B, S, _ = h.shape
    H, G, HG, DH = N_HEADS, NSA_KV_GROUPS, NSA_HEADS_PER_GROUP, HEAD_DIM
    L, STR, SB, W, QB = NSA_CMP_BLOCK, NSA_CMP_STRIDE, NSA_SEL_BLOCK, NSA_WINDOW, NSA_QBLOCK
    sizes = [H * DH] + [G * DH] * 6
    q, kc, vc, ks, vs, kw, vw, gates = jnp.split(h @ w_in, np.cumsum(sizes).tolist(), axis=-1)
    q = _apply_rope(_rms_norm(q.reshape(B, S, H, DH), q_gain), cos, sin)
    kc, ks, kw = [_apply_rope(_rms_norm(k.reshape(B, S, G, DH), k_gain[i]), cos, sin)
                  for i, k in enumerate((kc, ks, kw))]
    vc, vs, vw = [v.reshape(B, S, G, DH) for v in (vc, vs, vw)]
    gates = jax.nn.sigmoid(gates.astype(jnp.float32)).reshape(B, S, 3, G, HG)
    n_cmp = (S - L) // STR + 1
    cmp_start = jnp.arange(n_cmp) * STR
    cidx = cmp_start[:, None] + jnp.arange(L)[None, :]
    kc = jnp.einsum('bnlgd,lde->bnge', kc[:, cidx] + cmp_pos[0][None, None, :, None, :], w_cmp[0])
    vc = jnp.einsum('bnlgd,lde->bnge', vc[:, cidx] + cmp_pos[1][None, None, :, None, :], w_cmp[1])
    cmp_end = cmp_start + L - 1
    n_sel = S // SB
    sel_start = jnp.arange(n_sel) * SB
    overlap = ((cmp_start[:, None] < sel_start[None, :] + SB)
               & (cmp_start[:, None] + L > sel_start[None, :])).astype(jnp.float32)
    topk = min(NSA_SEL_TOPK, n_sel)
    ks_blk = ks.reshape(B, n_sel, SB, G, DH).transpose(0, 3, 1, 2, 4)
    vs_blk = vs.reshape(B, n_sel, SB, G, DH).transpose(0, 3, 1, 2, 4)
    kw_pad = jnp.pad(kw, ((0, 0), (W, 0), (0, 0), (0, 0)))
    vw_pad = jnp.pad(vw, ((0, 0), (W, 0), (0, 0), (0, 0)))
    scale = DH ** -0.5
    b_ix = jnp.arange(B)[:, None, None, None]
    g_ix = jnp.arange(G)[None, None, :, None]
    blk = jnp.arange(n_sel)
    in_blk = jnp.arange(SB)
    win_off = jnp.arange(W + QB) - W

    def chunk(c0):
        t = c0 + jnp.arange(QB)
        qc = lax.dynamic_slice_in_dim(q, c0, QB, axis=1).reshape(B, QB, G, HG, DH)
        p_c = _masked_softmax(jnp.einsum('bqghd,bngd->bqghn', qc, kc) * scale,
                              (cmp_end[None, :] <= t[:, None])[None, :, None, None, :])
        o_c = jnp.einsum('bqghn,bngd->bqghd', p_c.astype(vc.dtype), vc)
        imp = jnp.einsum('bqgn,nj->bqgj', p_c.sum(axis=3), overlap)
        cur = c0 // SB
        forced = (blk == 0) | (blk == cur) | (blk == cur - 1)
        imp = jnp.where(blk <= cur, imp + NSA_FORCE_BONUS * forced, -1.0)
        _, idx = lax.top_k(imp, topk)
        k_sel = ks_blk[b_ix, g_ix, idx].reshape(B, QB, G, topk * SB, DH)
        v_sel = vs_blk[b_ix, g_ix, idx].reshape(B, QB, G, topk * SB, DH)
        pos = (idx[..., None] * SB + in_blk).reshape(B, QB, G, 1, topk * SB)
        p_s = _masked_softmax(jnp.einsum('bqghd,bqgmd->bqghm', qc, k_sel) * scale,
                              pos <= t[None, :, None, None, None])
        o_s = jnp.einsum('bqghm,bqgmd->bqghd', p_s.astype(v_sel.dtype), v_sel)
        pos_w = c0 + win_off
        diff = t[:, None] - pos_w[None, :]
        m_w = ((pos_w[None, :] >= 0) & (diff >= 0) & (diff < W))[None, :, None, None, :]
        kwc = lax.dynamic_slice_in_dim(kw_pad, c0, W + QB, axis=1)
        vwc = lax.dynamic_slice_in_dim(vw_pad, c0, W + QB, axis=1)
        p_w = _masked_softmax(jnp.einsum('bqghd,bkgd->bqghk', qc, kwc) * scale, m_w)
        o_w = jnp.einsum('bqghk,bkgd->bqghd', p_w.astype(vwc.dtype), vwc)
        g = lax.dynamic_slice_in_dim(gates, c0, QB, axis=1)[..., None]
        o = g[:, :, 0] * o_c + g[:, :, 1] * o_s + g[:, :, 2] * o_w
        return o.reshape(B, QB, H * DH).astype(h.dtype)

    out = lax.map(chunk, jnp.arange(S // QB) * QB)
    out = jnp.transpose(out, (1, 0, 2, 3)).reshape(B, S, H * DH)
    return out @ w_out


def _stick_breaking_mixer(h, w_in, w_out):
    B, S, _ = h.shape
    H, DH, QB = N_HEADS, HEAD_DIM, SB_QBLOCK
    q, k, v = [a.reshape(B, S, H, DH) for a in jnp.split(h @ w_in, 3, axis=-1)]
    scale = DH ** -0.5
    s_idx = jnp.arange(S)

    def chunk(c0):
        t = c0 + jnp.arange(QB)
        qc = lax.dynamic_slice_in_dim(q, c0, QB, axis=1)
        z = jnp.einsum('bqhd,bshd->bhqs', qc, k).astype(jnp.float32) * scale
        mask = (s_idx[None, :] < t[:, None])[None, None]
        log_beta = jax.nn.log_sigmoid(z)
        log_one_minus = jnp.where(mask, jax.nn.log_sigmoid(-z), 0.0)
        between = lax.cumsum(log_one_minus, axis=3, reverse=True) - log_one_minus
        a = jnp.where(mask, jnp.exp(log_beta + between), 0.0)
        o = jnp.einsum('bhqs,bshd->bqhd', a.astype(v.dtype), v)
        return o.reshape(B, QB, H * DH)

    out = lax.map(chunk, jnp.arange(S // QB) * QB)
    out = jnp.transpose(out, (1, 0, 2, 3)).reshape(B, S, H * DH)
    return out @ w_out


def _conformer_conv_mixer(h, w_in, dw_w, dw_b, ln_g, ln_b, w_out):
    a, b = jnp.split(h @ w_in, 2, axis=-1)
    u = a * jax.nn.sigmoid(b)
    u = lax.conv_general_dilated(u, dw_w[:, None, :], window_strides=(1,),
                                 padding=[(CONV_WIDTH - 1, 0)],
                                 dimension_numbers=('NWC', 'WIO', 'NWC'),
                                 feature_group_count=D_MODEL) + dw_b
    uf = u.astype(jnp.float32)
    mu = jnp.mean(uf, axis=-1, keepdims=True)
    var = jnp.mean(jnp.square(uf - mu), axis=-1, keepdims=True)
    un = (uf - mu) * lax.rsqrt(var + NORM_EPS) * ln_g.astype(jnp.float32) + ln_b.astype(jnp.float32)
    return jax.nn.silu(un).astype(h.dtype) @ w_out


def _moba_mixer(h, w_in, q_gain, k_gain, w_out, cos, sin):
    B, S, _ = h.shape
    H, DH, BS, QB = N_HEADS, HEAD_DIM, MOBA_BLOCK, MOBA_QBLOCK
    q, k, v = [a.reshape(B, S, H, DH) for a in jnp.split(h @ w_in, 3, axis=-1)]
    q = _apply_rope(_rms_norm(q, q_gain), cos, sin)
    k = _apply_rope(_rms_norm(k, k_gain), cos, sin)
    n_blk = -(-S // BS)
    pad = n_blk * BS - S
    kp = jnp.pad(k, ((0, 0), (0, pad), (0, 0), (0, 0)))
    vp = jnp.pad(v, ((0, 0), (0, pad), (0, 0), (0, 0)))
    k_blocks = kp.reshape(B, n_blk, BS, H, DH)
    k_mean = jnp.mean(k_blocks.astype(jnp.float32), axis=2)
    kb_t = k_blocks.transpose(0, 3, 1, 2, 4)
    vb_t = vp.reshape(B, n_blk, BS, H, DH).transpose(0, 3, 1, 2, 4)
    topk = min(MOBA_TOPK, n_blk)
    scale = DH ** -0.5
    b_ix = jnp.arange(B)[:, None, None, None]
    h_ix = jnp.arange(H)[None, None, :, None]
    blk = jnp.arange(n_blk)

    def chunk(c0):
        t = c0 + jnp.arange(QB)
        qc = lax.dynamic_slice_in_dim(q, c0, QB, axis=1)
        cur = c0 // BS
        gate = jnp.einsum('bqhd,bnhd->bqhn', qc.astype(jnp.float32), k_mean)
        gate = jnp.where(blk < cur, gate, NEG_INF)
        _, idx = lax.top_k(gate, topk)
        valid = idx < cur
        k_sel = kb_t[b_ix, h_ix, idx].reshape(B, QB, H, topk * BS, DH)
        v_sel = vb_t[b_ix, h_ix, idx].reshape(B, QB, H, topk * BS, DH)
        s_past = jnp.einsum('bqhd,bqhmd->bqhm', qc, k_sel) * scale
        m_past = jnp.broadcast_to(valid[..., None], (B, QB, H, topk, BS)).reshape(B, QB, H, topk * BS)
        own0 = cur * BS
        k_own = lax.dynamic_slice_in_dim(kp, own0, BS, axis=1)
        v_own = lax.dynamic_slice_in_dim(vp, own0, BS, axis=1)
        s_own = jnp.einsum('bqhd,bkhd->bqhk', qc, k_own) * scale
        m_own = jnp.broadcast_to(((own0 + jnp.arange(BS))[None, :] <= t[:, None])[None, :, None, :],
                                 (B, QB, H, BS))
        p = _masked_softmax(jnp.concatenate([s_past, s_own], axis=-1),
                            jnp.concatenate([m_past, m_own], axis=-1))
        p = p.astype(v.dtype)
        o = (jnp.einsum('bqhm,bqhmd->bqhd', p[..., :topk * BS], v_sel)
             + jnp.einsum('bqhk,bkhd->bqhd', p[..., topk * BS:], v_own))
        return o.reshape(B, QB, H * DH)

    out = lax.map(chunk, jnp.arange(S // QB) * QB)
    out = jnp.transpose(out, (1, 0, 2, 3)).reshape(B, S, H * DH)
    return out @ w_out


def setup_inputs(seed: int = 0) -> dict:
    key = jax.random.key(seed)
    keys = iter(jax.random.split(key, 32))

    def w(shape, fan_in):
        return jax.random.normal(next(keys), shape, jnp.float32) * (fan_in ** -0.5)

    def gain(shape):
        return 1.0 + 0.05 * jax.random.normal(next(keys), shape, jnp.float32)

    def small(shape, s):
        return s * jax.random.normal(next(keys), shape, jnp.float32)

    D, DH, L = D_MODEL, HEAD_DIM, NSA_CMP_BLOCK
    return {
        'x': jax.random.normal(next(keys), (BATCH, SEQ, D), jnp.float32),
        'attn_norm': gain((DEPTH, D)),
        'mlp_norm': gain((DEPTH, D)),
        'mlp_w_up': w((DEPTH, D, D_FF), D),
        'mlp_w_down': w((DEPTH, D_FF, D), D_FF),
        'nsa_w_in': w((N_NSA, D, NSA_IN_DIM), D),
        'nsa_q_norm': gain((N_NSA, DH)),
        'nsa_k_norm': gain((N_NSA, 3, DH)),
        'nsa_cmp_pos': small((N_NSA, 2, L, DH), 0.1),
        'nsa_w_cmp': w((N_NSA, 2, L, DH, DH), L * DH),
        'nsa_w_out': w((N_NSA, N_HEADS * DH, D), N_HEADS * DH),
        'sb_w_in': w((N_SB, D, 3 * N_HEADS * DH), D),
        'sb_w_out': w((N_SB, N_HEADS * DH, D), N_HEADS * DH),
        'conv_w_in': w((N_CONV, D, 2 * D), D),
        'conv_dw_w': w((N_CONV, CONV_WIDTH, D), CONV_WIDTH),
        'conv_dw_b': small((N_CONV, D), 0.02),
        'conv_ln_g': gain((N_CONV, D)),
        'conv_ln_b': small((N_CONV, D), 0.02),
        'conv_w_out': w((N_CONV, D, D), D),
        'moba_w_in': w((N_MOBA, D, 3 * N_HEADS * DH), D),
        'moba_q_norm': gain((N_MOBA, DH)),
        'moba_k_norm': gain((N_MOBA, DH)),
        'moba_w_out': w((N_MOBA, N_HEADS * DH, D), N_HEADS * DH),
    }


def reference(x, attn_norm, mlp_norm, mlp_w_up, mlp_w_down,
              nsa_w_in, nsa_q_norm, nsa_k_norm, nsa_cmp_pos, nsa_w_cmp, nsa_w_out,
              sb_w_in, sb_w_out,
              conv_w_in, conv_dw_w, conv_dw_b, conv_ln_g, conv_ln_b, conv_w_out,
              moba_w_in, moba_q_norm, moba_k_norm, moba_w_out):
    cos, sin = _rope_tables(x.shape[1])
    for i in range(DEPTH):
        m, j = i % N_MIXERS, i // N_MIXERS
        h = _rms_norm(x, attn_norm[i])
        if m == 0:
            y = _nsa_mixer(h, nsa_w_in[j], nsa_q_norm[j], nsa_k_norm[j], nsa_cmp_pos[j],
                           nsa_w_cmp[j], nsa_w_out[j], cos, sin)
        elif m == 1:
            y = _stick_breaking_mixer(h, sb_w_in[j], sb_w_out[j])
        elif m == 2:
            y = _conformer_conv_mixer(h, conv_w_in[j], conv_dw_w[j], conv_dw_b[j],
                                      conv_ln_g[j], conv_ln_b[j], conv_w_out[j])
        else:
            y = _moba_mixer(h, moba_w_in[j], moba_q_norm[j], moba_k_norm[j], moba_w_out[j], cos, sin)
        x = x + y.astype(x.dtype)
        hm = _rms_norm(x, mlp_norm[i])
        x = x + jnp.square(jax.nn.relu(hm @ mlp_w_up[i])) @ mlp_w_down[i]
    return x
```

```python
import functools

import numpy as np
import jax
import jax.numpy as jnp
from jax import lax
from jax.experimental import pallas as pl
from jax.experimental.pallas import tpu as pltpu

F32 = jnp.float32
BF16 = jnp.bfloat16

D_MODEL = 1024
N_HEADS = 8
HEAD_DIM = 128
D_FF = 4 * D_MODEL
ROPE_THETA = 10000.0
NORM_EPS = 1e-6
NEG_INF = -1e30

NSA_KV_GROUPS = 2
NSA_HEADS_PER_GROUP = N_HEADS // NSA_KV_GROUPS
NSA_CMP_BLOCK = 32
NSA_CMP_STRIDE = 16
NSA_SEL_BLOCK = 64
NSA_SEL_TOPK = 16
NSA_WINDOW = 512
NSA_FORCE_BONUS = 1000.0

CONV_WIDTH = 31
MOBA_BLOCK = 256
MOBA_TOPK = 3

LANES = 128
VMEM_LIMIT_BYTES = 56 * 1024 * 1024
ROW_TILE = 512
PROJ_CHUNK = 512
FF_CHUNK = 1024
CONV_HALO = 32
CONV_ROWS = 32


def _cparams(*sem):
    return pltpu.CompilerParams(dimension_semantics=sem, vmem_limit_bytes=VMEM_LIMIT_BYTES)


def _const_spec(shape):
    zeros = (0,) * len(shape)
    return pl.BlockSpec(shape, lambda *_: zeros, pipeline_mode=pl.Buffered(1))


def _rms(x, gain):
    return x * lax.rsqrt(jnp.mean(x * x, axis=-1, keepdims=True) + NORM_EPS) * gain


def _dot(a, b):
    return jnp.dot(a, b, preferred_element_type=F32)


def _dot_nt(a, b):
    return lax.dot_general(a, b, (((1,), (1,)), ((), ())), preferred_element_type=F32)


def _split_bf16(x):
    hi = x.astype(BF16)
    lo = (x - hi.astype(F32)).astype(BF16)
    return hi, lo


def _proj_kernel(*refs, rope_blocks, has_gate):
    if has_gate:
        x_ref, g_ref, w_ref, cos_ref, sin_ref, hg_ref, wg_ref, o_ref, og_ref = refs
    else:
        x_ref, g_ref, w_ref, cos_ref, sin_ref, hg_ref, o_ref = refs
    h = _rms(x_ref[...], g_ref[...]).astype(BF16)
    blocks_per_chunk = PROJ_CHUNK // LANES
    for c in range(len(rope_blocks) // blocks_per_chunk):
        y = _dot(h, w_ref[:, c * PROJ_CHUNK:(c + 1) * PROJ_CHUNK])
        chunk_flags = rope_blocks[c * blocks_per_chunk:(c + 1) * blocks_per_chunk]
        if not any(chunk_flags):
            o_ref[:, c * PROJ_CHUNK:(c + 1) * PROJ_CHUNK] = y.astype(o_ref.dtype)
            continue
        for k, flag in enumerate(chunk_flags):
            b = c * blocks_per_chunk + k
            yb = y[:, k * LANES:(k + 1) * LANES]
            if flag:
                yb = _rms(yb, hg_ref[b:b + 1, :])
                yb = yb * cos_ref[...] + pltpu.roll(yb, HEAD_DIM // 2, 1) * sin_ref[...]
            o_ref[:, b * LANES:(b + 1) * LANES] = yb.astype(o_ref.dtype)
    if has_gate:
        og_ref[...] = _dot(h, wg_ref[...])


def _project(x2d, gain, w, cos_t, sin_t, head_gains, rope_blocks, seq, w_gate=None):
    t_rows, d = x2d.shape
    n = w.shape[1]
    assert n % PROJ_CHUNK == 0 and len(rope_blocks) == n // LANES and seq % ROW_TILE == 0
    seq_tiles = seq // ROW_TILE
    has_gate = w_gate is not None
    in_specs = [
        pl.BlockSpec((ROW_TILE, d), lambda i: (i, 0)),
        _const_spec((1, d)),
        _const_spec((d, n)),
        pl.BlockSpec((ROW_TILE, LANES), lambda i: (i % seq_tiles, 0)),
        pl.BlockSpec((ROW_TILE, LANES), lambda i: (i % seq_tiles, 0)),
        _const_spec(head_gains.shape),
    ]
    args = [x2d, gain.reshape(1, d), w, cos_t, sin_t, head_gains]
    out_shape = [jax.ShapeDtypeStruct((t_rows, n), BF16)]
    out_specs = [pl.BlockSpec((ROW_TILE, n), lambda i: (i, 0))]
    if has_gate:
        in_specs.append(_const_spec(w_gate.shape))
        args.append(w_gate)
        out_shape.append(jax.ShapeDtypeStruct((t_rows, LANES), F32))
        out_specs.append(pl.BlockSpec((ROW_TILE, LANES), lambda i: (i, 0)))
    outs = pl.pallas_call(
        functools.partial(_proj_kernel, rope_blocks=tuple(rope_blocks), has_gate=has_gate),
        grid=(t_rows // ROW_TILE,),
        in_specs=in_specs,
        out_specs=out_specs,
        out_shape=out_shape,
        compiler_params=_cparams("parallel"),
        name="norm_proj",
    )(*args)
    return outs if has_gate else outs[0]


def _mlp_kernel(x_ref, a_ref, wo_ref, g_ref, wup_ref, wdn_ref, o_ref):
    x1 = x_ref[...] + _dot(a_ref[...], wo_ref[...])
    h = _rms(x1, g_ref[...]).astype(BF16)
    acc = x1
    for c in range(D_FF // FF_CHUNK):
        u = _dot(h, wup_ref[:, c * FF_CHUNK:(c + 1) * FF_CHUNK])
        act = jnp.square(jnp.maximum(u, 0.0)).astype(BF16)
        acc = acc + _dot(act, wdn_ref[c * FF_CHUNK:(c + 1) * FF_CHUNK, :])
    o_ref[...] = acc


def _mixer_out_and_mlp(x2d, a2d, w_out, gain, w_up, w_down):
    t_rows, d = x2d.shape
    return pl.pallas_call(
        _mlp_kernel,
        grid=(t_rows // ROW_TILE,),
        in_specs=[
            pl.BlockSpec((ROW_TILE, d), lambda i: (i, 0)),
            pl.BlockSpec((ROW_TILE, d), lambda i: (i, 0)),
            _const_spec((d, d)),
            _const_spec((1, d)),
            _const_spec((d, D_FF)),
            _const_spec((D_FF, d)),
        ],
        out_specs=pl.BlockSpec((ROW_TILE, d), lambda i: (i, 0)),
        out_shape=jax.ShapeDtypeStruct((t_rows, d), F32),
        compiler_params=_cparams("parallel"),
        name="outproj_mlp",
    )(x2d, a2d, w_out, gain.reshape(1, d), w_up, w_down)


SB_TILE = 256


def _sb_kernel(q_ref, k_ref, v_ref, o_ref):
    i = pl.program_id(2)
    t = SB_TILE
    q = q_ref[0]
    row = lax.broadcasted_iota(jnp.int32, (t, t), 0)
    col = lax.broadcasted_iota(jnp.int32, (t, t), 1)
    below = row > col
    suffix_ones = below.astype(BF16)

    def block(j, carry, diagonal):
        o, later = carry
        start = pl.multiple_of(j * t, t)
        kj = k_ref[0, pl.ds(start, t), :]
        vj = v_ref[0, pl.ds(start, t), :]
        z = _dot_nt(q, kj)
        softplus = jnp.maximum(z, 0.0) + jnp.log(1.0 + jnp.exp(-jnp.abs(z)))
        log_om = -softplus
        if diagonal:
            log_om = jnp.where(below, log_om, 0.0)
        hi, lo = _split_bf16(log_om)
        between = _dot(hi, suffix_ones) + _dot(lo, suffix_ones) + later
        a = jnp.exp(z - softplus + between)
        if diagonal:
            a = jnp.where(below, a, 0.0)
        o = o + _dot(a.astype(BF16), vj)
        later = later + jnp.sum(log_om, axis=1, keepdims=True)
        return o, later

    carry = (jnp.zeros((t, HEAD_DIM), F32), jnp.zeros((t, 1), F32))
    carry = block(i, carry, True)
    o, _ = lax.fori_loop(0, i, lambda jj, c: block(i - 1 - jj, c, False), carry)
    o_ref[0] = o.astype(o_ref.dtype)


def _sb_attention(qkv):
    b, s, _ = qkv.shape
    h = N_HEADS
    return pl.pallas_call(
        _sb_kernel,
        grid=(b, h, s // SB_TILE),
        in_specs=[
            pl.BlockSpec((1, SB_TILE, LANES), lambda bi, hi, i: (bi, i, hi)),
            pl.BlockSpec((1, s, LANES), lambda bi, hi, i: (bi, 0, h + hi)),
            pl.BlockSpec((1, s, LANES), lambda bi, hi, i: (bi, 0, 2 * h + hi)),
        ],
        out_specs=pl.BlockSpec((1, SB_TILE, LANES), lambda bi, hi, i: (bi, i, hi)),
        out_shape=jax.ShapeDtypeStruct((b, s, h * HEAD_DIM), BF16),
        compiler_params=_cparams("parallel", "parallel", "arbitrary"),
        name="stick_breaking_attn",
    )(qkv, qkv, qkv)


def _online_softmax_step(carry, s, v):
    m, l, o = carry
    m_new = jnp.maximum(m, jnp.max(s, axis=1, keepdims=True))
    alpha = jnp.exp(m - m_new)
    p = jnp.exp(s - m_new)
    l = alpha * l + jnp.sum(p, axis=1, keepdims=True)
    o = alpha * o + _dot(p.astype(BF16), v)
    return m_new, l, o


def _moba_kernel(q_ref, k_ref, v_ref, o_ref, kaug_ref, kmean_ref, *, n_blk):
    i = pl.program_id(2)
    t = MOBA_BLOCK
    s_len = k_ref.shape[1]

    @pl.when(i == 0)
    def _():
        k = k_ref[0]
        kaug_ref[:, :LANES] = k
        blk = lax.broadcasted_iota(jnp.int32, (s_len, LANES), 0) // t
        lane = lax.broadcasted_iota(jnp.int32, (s_len, LANES), 1)
        kaug_ref[:, LANES:] = (blk == lane).astype(BF16)
        kmean_ref[...] = jnp.mean(k.astype(F32).reshape(n_blk, t, LANES), axis=1)

    q = q_ref[0]
    km_hi, km_lo = _split_bf16(kmean_ref[...])
    gate = _dot_nt(km_hi, q) + _dot_nt(km_lo, q)
    blk_id = lax.broadcasted_iota(jnp.int32, (n_blk, t), 0)
    past = blk_id < i
    gate = jnp.where(past, gate, -jnp.inf)
    rank = jnp.zeros((n_blk, t), jnp.int32)
    for m in range(n_blk):
        gm = gate[m:m + 1, :]
        beats = (gm > gate) | ((gm == gate) & (m < blk_id))
        rank = rank + beats.astype(jnp.int32)
    sel_bias = jnp.where((rank < MOBA_TOPK) & past, 0.0, NEG_INF)
    sel_bias = jnp.concatenate([sel_bias, jnp.zeros((LANES - n_blk, t), F32)], axis=0)
    q_aug = jnp.concatenate([q, sel_bias.T.astype(BF16)], axis=1)

    start = pl.multiple_of(i * t, t)
    row = lax.broadcasted_iota(jnp.int32, (t, t), 0)
    col = lax.broadcasted_iota(jnp.int32, (t, t), 1)
    s = jnp.where(col <= row, _dot_nt(q, k_ref[0, pl.ds(start, t), :]), NEG_INF)
    m0 = jnp.max(s, axis=1, keepdims=True)
    p = jnp.exp(s - m0)
    carry = (m0, jnp.sum(p, axis=1, keepdims=True), _dot(p.astype(BF16), v_ref[0, pl.ds(start, t), :]))

    def past_block(n, carry):
        st = pl.multiple_of(n * t, t)
        s = _dot_nt(q_aug, kaug_ref[pl.ds(st, t), :])
        return _online_softmax_step(carry, s, v_ref[0, pl.ds(st, t), :])

    _, l, o = lax.fori_loop(0, i, past_block, carry)
    o_ref[0] = (o / l).astype(o_ref.dtype)


def _moba_attention(qkv):
    b, s, _ = qkv.shape
    h = N_HEADS
    assert s % MOBA_BLOCK == 0
    n_blk = s // MOBA_BLOCK
    assert n_blk % 8 == 0 and n_blk <= LANES
    return pl.pallas_call(
        functools.partial(_moba_kernel, n_blk=n_blk),
        grid=(b, h, n_blk),
        in_specs=[
            pl.BlockSpec((1, MOBA_BLOCK, LANES), lambda bi, hi, i: (bi, i, hi)),
            pl.BlockSpec((1, s, LANES), lambda bi, hi, i: (bi, 0, h + hi)),
            pl.BlockSpec((1, s, LANES), lambda bi, hi, i: (bi, 0, 2 * h + hi)),
        ],
        out_specs=pl.BlockSpec((1, MOBA_BLOCK, LANES), lambda bi, hi, i: (bi, i, hi)),
        out_shape=jax.ShapeDtypeStruct((b, s, h * HEAD_DIM), BF16),
        scratch_shapes=[pltpu.VMEM((s, 2 * LANES), BF16), pltpu.VMEM((n_blk, LANES), F32)],
        compiler_params=_cparams("parallel", "parallel", "arbitrary"),
        name="moba_attn",
    )(qkv, qkv, qkv)


NSA_TQ = 128
NSA_TK = 512
CMP_HALF = NSA_CMP_STRIDE * HEAD_DIM


def _cmp_kernel(xk_ref, xv_ref, pos_ref, w_ref, ok_ref, ov_ref):
    n16 = xk_ref.shape[2]
    for t, (x_ref, o_ref) in enumerate(((xk_ref, ok_ref), (xv_ref, ov_ref))):
        x = x_ref[0, 0].astype(F32)
        first = _dot((x + pos_ref[t, 0:1, :]).astype(BF16), w_ref[t, 0])
        second = _dot((x + pos_ref[t, 1:2, :]).astype(BF16), w_ref[t, 1])
        o_ref[0, 0] = (first + pltpu.roll(second, n16 - 1, 0)).astype(o_ref.dtype)


def _nsa_compress(kc, vc, cmp_pos, w_cmp):
    b, g, n16, _ = kc.shape
    pos = cmp_pos.reshape(2, 2, CMP_HALF).astype(F32)
    w = w_cmp.reshape(2, 2, CMP_HALF, HEAD_DIM).astype(BF16)
    x_spec = pl.BlockSpec((1, 1, n16, CMP_HALF), lambda bi, gi: (bi, gi, 0, 0))
    o_spec = pl.BlockSpec((1, 1, n16, HEAD_DIM), lambda bi, gi: (bi, gi, 0, 0))
    return pl.pallas_call(
        _cmp_kernel,
        grid=(b, g),
        in_specs=[x_spec, x_spec, _const_spec(pos.shape), _const_spec(w.shape)],
        out_specs=[o_spec, o_spec],
        out_shape=[jax.ShapeDtypeStruct((b, g, n16, HEAD_DIM), BF16)] * 2,
        compiler_params=_cparams("parallel", "parallel"),
        name="nsa_compress",
    )(kc, vc, pos, w)


def _masked_softmax_rows(s, mask):
    z = jnp.where(mask, s, NEG_INF)
    e = jnp.where(mask, jnp.exp(z - jnp.max(z, axis=1, keepdims=True)), 0.0)
    l = jnp.sum(e, axis=1, keepdims=True)
    return e / jnp.where(l > 0.0, l, 1.0)


def _nsa_kernel(q_ref, kc_ref, vc_ref, ks_ref, vs_ref, kw_ref, vw_ref, gate_ref, ov_ref, o_ref,
                ksaug_ref, *, n_sel):
    i = pl.program_id(2)
    tq, tk, hg = NSA_TQ, NSA_TK, NSA_HEADS_PER_GROUP
    rows = hg * tq
    s_len = ks_ref.shape[1]
    n_cmp_pad = kc_ref.shape[2]
    c0 = i * tq

    @pl.when(i == 0)
    def _():
        ksaug_ref[:, :LANES] = ks_ref[0]
        blk = lax.broadcasted_iota(jnp.int32, (s_len, LANES), 0) // NSA_SEL_BLOCK
        lane = lax.broadcasted_iota(jnp.int32, (s_len, LANES), 1)
        ksaug_ref[:, LANES:] = (blk == lane).astype(BF16)

    q_all = q_ref[0]
    q4 = jnp.concatenate([q_all[:, h * LANES:(h + 1) * LANES] for h in range(hg)], axis=0)

    def row_pos(width):
        r = lax.broadcasted_iota(jnp.int32, (rows, width), 0)
        return c0 + (r & (tq - 1))

    s_c = _dot_nt(q4, kc_ref[0, 0])
    n_idx = lax.broadcasted_iota(jnp.int32, (rows, n_cmp_pad), 1)
    p_c = _masked_softmax_rows(
        s_c, n_idx * NSA_CMP_STRIDE + (NSA_CMP_BLOCK - 1) <= row_pos(n_cmp_pad))
    o_c = _dot(p_c.astype(BF16), vc_ref[0, 0])

    p_sum = p_c[0:tq]
    for h in range(1, hg):
        p_sum = p_sum + p_c[h * tq:(h + 1) * tq]
    ps_hi, ps_lo = _split_bf16(p_sum)
    imp = _dot(ps_hi, ov_ref[...]) + _dot(ps_lo, ov_ref[...])
    blk = lax.broadcasted_iota(jnp.int32, (tq, LANES), 1)
    cur = (c0 + lax.broadcasted_iota(jnp.int32, (tq, LANES), 0)) // NSA_SEL_BLOCK
    forced = (blk == 0) | (blk == cur) | (blk == cur - 1)
    score = jnp.where(blk <= cur, imp + NSA_FORCE_BONUS * forced.astype(F32), -1.0)
    score = jnp.where(blk < n_sel, score, -jnp.inf)
    lane_f = blk.astype(F32)
    chosen = jnp.zeros((tq, LANES), jnp.bool_)
    for _ in range(min(NSA_SEL_TOPK, n_sel)):
        best = jnp.max(score, axis=1, keepdims=True)
        first = jnp.min(jnp.where(score == best, lane_f, float(LANES)), axis=1, keepdims=True)
        hit = lane_f == first
        chosen = chosen | hit
        score = jnp.where(hit, -jnp.inf, score)
    sel_bias = jnp.where(chosen, 0.0, NEG_INF).astype(BF16)
    q_aug = jnp.concatenate([q4, jnp.concatenate([sel_bias] * hg, axis=0)], axis=1)

    jd = c0 // tk
    dstart = pl.multiple_of(jd * tk, tk)
    key_pos = dstart + lax.broadcasted_iota(jnp.int32, (rows, tk), 1)
    s = _dot_nt(q_aug, ksaug_ref[pl.ds(dstart, tk), :])
    s = jnp.where(key_pos <= row_pos(tk), s, NEG_INF)
    m0 = jnp.max(s, axis=1, keepdims=True)
    p = jnp.exp(s - m0)
    carry = (m0, jnp.sum(p, axis=1, keepdims=True), _dot(p.astype(BF16), vs_ref[0, pl.ds(dstart, tk), :]))

    def sel_block(j, carry):
        st = pl.multiple_of(j * tk, tk)
        s = _dot_nt(q_aug, ksaug_ref[pl.ds(st, tk), :])
        return _online_softmax_step(carry, s, vs_ref[0, pl.ds(st, tk), :])

    _, l_s, o_s = lax.fori_loop(0, jd, sel_block, carry)
    o_s = o_s / l_s

    span = tq + NSA_WINDOW
    wstart = pl.multiple_of(jnp.maximum(c0 - NSA_WINDOW, 0), tq)
    w_pos = wstart + lax.broadcasted_iota(jnp.int32, (rows, span), 1)
    t_pos = row_pos(span)
    s_w = _dot_nt(q4, kw_ref[0, pl.ds(wstart, span), :])
    p_w = _masked_softmax_rows(s_w, (w_pos <= t_pos) & (t_pos - w_pos < NSA_WINDOW))
    o_w = _dot(p_w.astype(BF16), vw_ref[0, pl.ds(wstart, span), :])

    gates = 1.0 / (1.0 + jnp.exp(-gate_ref[0]))
    g_idx = pl.program_id(1)
    lane = lax.broadcasted_iota(jnp.int32, (tq, LANES), 1)
    for h in range(hg):
        head = g_idx * hg + h
        sl = slice(h * tq, (h + 1) * tq)
        out = jnp.zeros((tq, HEAD_DIM), F32)
        for branch, o_b in enumerate((o_c, o_s, o_w)):
            g_col = jnp.sum(jnp.where(lane == branch * N_HEADS + head, gates, 0.0), axis=1, keepdims=True)
            out = out + g_col * o_b[sl]
        o_ref[0, :, h * LANES:(h + 1) * LANES] = out.astype(o_ref.dtype)


def _nsa_attention(main, gate_logits, kc_cmp, vc_cmp, overlap):
    b, s, _ = main.shape
    g, hg = NSA_KV_GROUPS, NSA_HEADS_PER_GROUP
    n_sel = s // NSA_SEL_BLOCK
    assert s % NSA_TK == 0 and n_sel <= LANES
    n16 = kc_cmp.shape[2]
    q_blocks = N_HEADS

    def kv_spec(which):
        return pl.BlockSpec((1, s, LANES), lambda bi, gi, i: (bi, 0, q_blocks + which * g + gi))

    cmp_spec = pl.BlockSpec((1, 1, n16, HEAD_DIM), lambda bi, gi, i: (bi, gi, 0, 0))
    return pl.pallas_call(
        functools.partial(_nsa_kernel, n_sel=n_sel),
        grid=(b, g, s // NSA_TQ),
        in_specs=[
            pl.BlockSpec((1, NSA_TQ, hg * LANES), lambda bi, gi, i: (bi, i, gi)),
            cmp_spec, cmp_spec,
            kv_spec(2), kv_spec(3), kv_spec(4), kv_spec(5),
            pl.BlockSpec((1, NSA_TQ, LANES), lambda bi, gi, i: (bi, i, 0)),
            _const_spec(overlap.shape),
        ],
        out_specs=pl.BlockSpec((1, NSA_TQ, hg * LANES), lambda bi, gi, i: (bi, i, gi)),
        out_shape=jax.ShapeDtypeStruct((b, s, N_HEADS * HEAD_DIM), BF16),
        scratch_shapes=[pltpu.VMEM((s, 2 * LANES), BF16)],
        compiler_params=_cparams("parallel", "parallel", "arbitrary"),
        name="nsa_attn",
    )(main, kc_cmp, vc_cmp, main, main, main, main, gate_logits, overlap)


def _nsa_overlap(seq):
    n_cmp = (seq - NSA_CMP_BLOCK) // NSA_CMP_STRIDE + 1
    n_sel = seq // NSA_SEL_BLOCK
    cmp_start = np.arange(seq // NSA_CMP_STRIDE) * NSA_CMP_STRIDE
    sel_start = np.arange(LANES) * NSA_SEL_BLOCK
    ov = ((cmp_start[:, None] < sel_start[None, :] + NSA_SEL_BLOCK)
          & (cmp_start[:, None] + NSA_CMP_BLOCK > sel_start[None, :]))
    ov &= (np.arange(seq // NSA_CMP_STRIDE)[:, None] < n_cmp) & (np.arange(LANES)[None, :] < n_sel)
    return jnp.asarray(ov, dtype=BF16)


def _glu_proj_kernel(x_ref, g_ref, w_ref, o_ref):
    h = _rms(x_ref[...], g_ref[...]).astype(BF16)
    d = o_ref.shape[1]
    for c in range(d // PROJ_CHUNK):
        a = _dot(h, w_ref[:, c * PROJ_CHUNK:(c + 1) * PROJ_CHUNK])
        gate = _dot(h, w_ref[:, d + c * PROJ_CHUNK:d + (c + 1) * PROJ_CHUNK])
        o_ref[:, c * PROJ_CHUNK:(c + 1) * PROJ_CHUNK] = (a / (1.0 + jnp.exp(-gate))).astype(o_ref.dtype)


def _glu_project(x2d, gain, w):
    t_rows, d = x2d.shape
    return pl.pallas_call(
        _glu_proj_kernel,
        grid=(t_rows // ROW_TILE,),
        in_specs=[pl.BlockSpec((ROW_TILE, d), lambda i: (i, 0)), _const_spec((1, d)), _const_spec(w.shape)],
        out_specs=pl.BlockSpec((ROW_TILE, d), lambda i: (i, 0)),
        out_shape=jax.ShapeDtypeStruct((t_rows, d), BF16),
        compiler_params=_cparams("parallel"),
        name="norm_glu_proj",
    )(x2d, gain.reshape(1, d), w)


CONV_TILE = 256


def _conv_kernel(u_ref, halo_ref, dw_ref, db_ref, lg_ref, lb_ref, o_ref, ext_ref):
    i = pl.program_id(1)
    halo = halo_ref[0].astype(F32)
    ext_ref[0:CONV_HALO, :] = jnp.where(i == 0, 0.0, halo)
    ext_ref[CONV_HALO:, :] = u_ref[0].astype(F32)
    lead = CONV_HALO - (CONV_WIDTH - 1)
    for r in range(CONV_TILE // CONV_ROWS):
        acc = jnp.zeros((CONV_ROWS, D_MODEL), F32) + db_ref[...]
        for w in range(CONV_WIDTH):
            acc = acc + dw_ref[w:w + 1, :] * ext_ref[pl.ds(r * CONV_ROWS + lead + w, CONV_ROWS), :]
        mu = jnp.mean(acc, axis=-1, keepdims=True)
        cen = acc - mu
        var = jnp.mean(cen * cen, axis=-1, keepdims=True)
        un = cen * lax.rsqrt(var + NORM_EPS) * lg_ref[...] + lb_ref[...]
        o_ref[0, r * CONV_ROWS:(r + 1) * CONV_ROWS, :] = (un / (1.0 + jnp.exp(-un))).astype(o_ref.dtype)


def _conv_ln_swish(u, dw_w, dw_b, ln_g, ln_b):
    b, s, d = u.shape
    halo_per_tile = CONV_TILE // CONV_HALO
    dw = jnp.concatenate([dw_w, jnp.zeros((1, d), F32)], axis=0)
    return pl.pallas_call(
        _conv_kernel,
        grid=(b, s // CONV_TILE),
        in_specs=[
            pl.BlockSpec((1, CONV_TILE, d), lambda bi, i: (bi, i, 0)),
            pl.BlockSpec((1, CONV_HALO, d), lambda bi, i: (bi, jnp.maximum(i * halo_per_tile - 1, 0), 0)),
            _const_spec(dw.shape), _const_spec((1, d)), _const_spec((1, d)), _const_spec((1, d)),
        ],
        out_specs=pl.BlockSpec((1, CONV_TILE, d), lambda bi, i: (bi, i, 0)),
        out_shape=jax.ShapeDtypeStruct((b, s, d), BF16),
        scratch_shapes=[pltpu.VMEM((CONV_TILE + CONV_HALO, d), F32)],
        compiler_params=_cparams("parallel", "parallel"),
        name="conv_ln_swish",
    )(u, u, dw, dw_b.reshape(1, d), ln_g.reshape(1, d), ln_b.reshape(1, d))


def _rope_tables(seq):
    half = HEAD_DIM // 2
    inv_freq = ROPE_THETA ** (-jnp.arange(half, dtype=F32) / half)
    ang = jnp.arange(seq, dtype=F32)[:, None] * inv_freq[None, :]
    cos, sin = jnp.cos(ang), jnp.sin(ang)
    return jnp.concatenate([cos, cos], axis=1), jnp.concatenate([-sin, sin], axis=1)


def _nsa_mixer(x2d, norm_gain, w_in, q_gain, k_gain, cmp_pos, w_cmp, tables, batch, seq):
    g, dh = NSA_KV_GROUPS, HEAD_DIM
    n_main = (N_HEADS + 6 * g) * dh
    scale = dh ** -0.5
    w_main = w_in[:, :n_main].astype(BF16)
    w_gate = jnp.pad(w_in[:, n_main:], ((0, 0), (0, LANES - 3 * N_HEADS))).astype(BF16)
    ones = jnp.ones((dh,), F32)
    head_gains = jnp.stack([q_gain * scale] * N_HEADS + [k_gain[0]] * g + [ones] * g
                           + [k_gain[1]] * g + [ones] * g + [k_gain[2]] * g + [ones] * g)
    rope_blocks = [True] * N_HEADS + [True] * g + [False] * g + [True] * g + [False] * g + [True] * g + [False] * g
    main, gate_logits = _project(x2d, norm_gain, w_main, *tables, head_gains, rope_blocks, seq, w_gate=w_gate)
    main = main.reshape(batch, seq, n_main)
    gate_logits = gate_logits.reshape(batch, seq, LANES)

    def cmp_layout(col0):
        t = main[:, :, col0:col0 + g * dh].reshape(batch, seq // NSA_CMP_STRIDE, NSA_CMP_STRIDE, g, dh)
        return t.transpose(0, 3, 1, 2, 4).reshape(batch, g, seq // NSA_CMP_STRIDE, CMP_HALF)

    kc_cmp, vc_cmp = _nsa_compress(cmp_layout(N_HEADS * dh), cmp_layout((N_HEADS + g) * dh), cmp_pos, w_cmp)
    out = _nsa_attention(main, gate_logits, kc_cmp, vc_cmp, _nsa_overlap(seq))
    return out.reshape(batch * seq, N_HEADS * dh)


def _qkv_weight_with_scaled_q(w_in):
    n_q = N_HEADS * HEAD_DIM
    return jnp.concatenate([w_in[:, :n_q] * HEAD_DIM ** -0.5, w_in[:, n_q:]], axis=1).astype(BF16)


def _sb_mixer(x2d, norm_gain, w_in, tables, batch, seq):
    n = 3 * N_HEADS * HEAD_DIM
    head_gains = jnp.ones((n // LANES, HEAD_DIM), F32)
    qkv = _project(x2d, norm_gain, _qkv_weight_with_scaled_q(w_in), *tables, head_gains,
                   [False] * (n // LANES), seq)
    return _sb_attention(qkv.reshape(batch, seq, n)).reshape(batch * seq, N_HEADS * HEAD_DIM)


def _conv_mixer(x2d, norm_gain, w_in, dw_w, dw_b, ln_g, ln_b, batch, seq):
    u = _glu_project(x2d, norm_gain, w_in.astype(BF16))
    a = _conv_ln_swish(u.reshape(batch, seq, D_MODEL), dw_w, dw_b, ln_g, ln_b)
    return a.reshape(batch * seq, D_MODEL)


def _moba_mixer(x2d, norm_gain, w_in, q_gain, k_gain, tables, batch, seq):
    n = 3 * N_HEADS * HEAD_DIM
    ones = jnp.ones((HEAD_DIM,), F32)
    head_gains = jnp.stack([q_gain * HEAD_DIM ** -0.5] * N_HEADS + [k_gain] * N_HEADS + [ones] * N_HEADS)
    rope_blocks = [True] * (2 * N_HEADS) + [False] * N_HEADS
    qkv = _project(x2d, norm_gain, w_in.astype(BF16), *tables, head_gains, rope_blocks, seq)
    return _moba_attention(qkv.reshape(batch, seq, n)).reshape(batch * seq, N_HEADS * HEAD_DIM)


def kernel(x, attn_norm, mlp_norm, mlp_w_up, mlp_w_down, nsa_w_in, nsa_q_norm, nsa_k_norm, nsa_cmp_pos, nsa_w_cmp, nsa_w_out, sb_w_in, sb_w_out, conv_w_in, conv_dw_w, conv_dw_b, conv_ln_g, conv_ln_b, conv_w_out, moba_w_in, moba_q_norm, moba_k_norm, moba_w_out):
    batch, seq, d = x.shape
    depth = attn_norm.shape[0]
    tables = _rope_tables(seq)
    x2d = x.reshape(batch * seq, d)
    for i in range(depth):
        m, j = i % 4, i // 4
        if m == 0:
            a = _nsa_mixer(x2d, attn_norm[i], nsa_w_in[j], nsa_q_norm[j], nsa_k_norm[j], nsa_cmp_pos[j],
                           nsa_w_cmp[j], tables, batch, seq)
            w_out = nsa_w_out[j]
        elif m == 1:
            a = _sb_mixer(x2d, attn_norm[i], sb_w_in[j], tables, batch, seq)
            w_out = sb_w_out[j]
        elif m == 2:
            a = _conv_mixer(x2d, attn_norm[i], conv_w_in[j], conv_dw_w[j], conv_dw_b[j], conv_ln_g[j],
                            conv_ln_b[j], batch, seq)
            w_out = conv_w_out[j]
        else:
            a = _moba_mixer(x2d, attn_norm[i], moba_w_in[j], moba_q_norm[j], moba_k_norm[j], tables, batch, seq)
            w_out = moba_w_out[j]
        x2d = _mixer_out_and_mlp(x2d, a, w_out.astype(BF16), mlp_norm[i], mlp_w_up[i].astype(BF16),
                                 mlp_w_down[i].astype(BF16))
    return x2d.reshape(batch, seq, d)
```

```python
import functools

import numpy as np
import jax
import jax.numpy as jnp
from jax import lax
from jax.experimental import pallas as pl
from jax.experimental.pallas import tpu as pltpu

F32 = jnp.float32
BF16 = jnp.bfloat16

D_MODEL = 1024
N_HEADS = 8
HEAD_DIM = 128
D_FF = 4 * D_MODEL
ROPE_THETA = 10000.0
NORM_EPS = 1e-6
NEG_INF = -1e30

NSA_KV_GROUPS = 2
NSA_HEADS_PER_GROUP = N_HEADS // NSA_KV_GROUPS
NSA_CMP_BLOCK = 32
NSA_CMP_STRIDE = 16
NSA_SEL_BLOCK = 64
NSA_SEL_TOPK = 16
NSA_WINDOW = 512
NSA_FORCE_BONUS = 1000.0

CONV_WIDTH = 31
MOBA_BLOCK = 256
MOBA_TOPK = 3

LANES = 128
VMEM_LIMIT_BYTES = 56 * 1024 * 1024
ROW_TILE = 512
PROJ_CHUNK = 512
FF_CHUNK = 1024
CONV_HALO = 32
CONV_ROWS = 32


def _cparams(*sem):
    return pltpu.CompilerParams(dimension_semantics=sem, vmem_limit_bytes=VMEM_LIMIT_BYTES)


def _const_spec(shape):
    zeros = (0,) * len(shape)
    return pl.BlockSpec(shape, lambda *_: zeros, pipeline_mode=pl.Buffered(1))


def _rms(x, gain):
    return x * lax.rsqrt(jnp.mean(x * x, axis=-1, keepdims=True) + NORM_EPS) * gain


def _dot(a, b):
    return jnp.dot(a, b, preferred_element_type=F32)


def _dot_nt(a, b):
    return lax.dot_general(a, b, (((1,), (1,)), ((), ())), preferred_element_type=F32)


def _split_bf16(x):
    hi = x.astype(BF16)
    lo = (x - hi.astype(F32)).astype(BF16)
    return hi, lo


def _proj_kernel(*refs, rope_blocks, has_gate):
    if has_gate:
        x_ref, g_ref, w_ref, cos_ref, sin_ref, hg_ref, wg_ref, o_ref, og_ref = refs
    else:
        x_ref, g_ref, w_ref, cos_ref, sin_ref, hg_ref, o_ref = refs
    h = _rms(x_ref[...], g_ref[...]).astype(BF16)
    blocks_per_chunk = PROJ_CHUNK // LANES
    for c in range(len(rope_blocks) // blocks_per_chunk):
        y = _dot(h, w_ref[:, c * PROJ_CHUNK:(c + 1) * PROJ_CHUNK])
        chunk_flags = rope_blocks[c * blocks_per_chunk:(c + 1) * blocks_per_chunk]
        if not any(chunk_flags):
            o_ref[:, c * PROJ_CHUNK:(c + 1) * PROJ_CHUNK] = y.astype(o_ref.dtype)
            continue
        for k, flag in enumerate(chunk_flags):
            b = c * blocks_per_chunk + k
            yb = y[:, k * LANES:(k + 1) * LANES]
            if flag:
                yb = _rms(yb, hg_ref[b:b + 1, :])
                yb = yb * cos_ref[...] + pltpu.roll(yb, HEAD_DIM // 2, 1) * sin_ref[...]
            o_ref[:, b * LANES:(b + 1) * LANES] = yb.astype(o_ref.dtype)
    if has_gate:
        og_ref[...] = _dot(h, wg_ref[...])


def _project(x2d, gain, w, cos_t, sin_t, head_gains, rope_blocks, seq, w_gate=None):
    t_rows, d = x2d.shape
    n = w.shape[1]
    assert n % PROJ_CHUNK == 0 and len(rope_blocks) == n // LANES and seq % ROW_TILE == 0
    seq_tiles = seq // ROW_TILE
    has_gate = w_gate is not None
    in_specs = [
        pl.BlockSpec((ROW_TILE, d), lambda i: (i, 0)),
        _const_spec((1, d)),
        _const_spec((d, n)),
        pl.BlockSpec((ROW_TILE, LANES), lambda i: (i % seq_tiles, 0)),
        pl.BlockSpec((ROW_TILE, LANES), lambda i: (i % seq_tiles, 0)),
        _const_spec(head_gains.shape),
    ]
    args = [x2d, gain.reshape(1, d), w, cos_t, sin_t, head_gains]
    out_shape = [jax.ShapeDtypeStruct((t_rows, n), BF16)]
    out_specs = [pl.BlockSpec((ROW_TILE, n), lambda i: (i, 0))]
    if has_gate:
        in_specs.append(_const_spec(w_gate.shape))
        args.append(w_gate)
        out_shape.append(jax.ShapeDtypeStruct((t_rows, LANES), F32))
        out_specs.append(pl.BlockSpec((ROW_TILE, LANES), lambda i: (i, 0)))
    outs = pl.pallas_call(
        functools.partial(_proj_kernel, rope_blocks=tuple(rope_blocks), has_gate=has_gate),
        grid=(t_rows // ROW_TILE,),
        in_specs=in_specs,
        out_specs=out_specs,
        out_shape=out_shape,
        compiler_params=_cparams("parallel"),
        name="norm_proj",
    )(*args)
    return outs if has_gate else outs[0]


def _mlp_kernel(x_ref, a_ref, wo_ref, g_ref, wup_ref, wdn_ref, o_ref):
    x1 = x_ref[...] + _dot(a_ref[...], wo_ref[...])
    h = _rms(x1, g_ref[...]).astype(BF16)
    acc = x1
    for c in range(D_FF // FF_CHUNK):
        u = _dot(h, wup_ref[:, c * FF_CHUNK:(c + 1) * FF_CHUNK])
        act = jnp.square(jnp.maximum(u, 0.0)).astype(BF16)
        acc = acc + _dot(act, wdn_ref[c * FF_CHUNK:(c + 1) * FF_CHUNK, :])
    o_ref[...] = acc


def _mixer_out_and_mlp(x2d, a2d, w_out, gain, w_up, w_down):
    t_rows, d = x2d.shape
    return pl.pallas_call(
        _mlp_kernel,
        grid=(t_rows // ROW_TILE,),
        in_specs=[
            pl.BlockSpec((ROW_TILE, d), lambda i: (i, 0)),
            pl.BlockSpec((ROW_TILE, d), lambda i: (i, 0)),
            _const_spec((d, d)),
            _const_spec((1, d)),
            _const_spec((d, D_FF)),
            _const_spec((D_FF, d)),
        ],
        out_specs=pl.BlockSpec((ROW_TILE, d), lambda i: (i, 0)),
        out_shape=jax.ShapeDtypeStruct((t_rows, d), F32),
        compiler_params=_cparams("parallel"),
        name="outproj_mlp",
    )(x2d, a2d, w_out, gain.reshape(1, d), w_up, w_down)


SB_TILE = 256
SB_UNDERFLOW_LOG = -104.0


def _sb_kernel(q_ref, k_ref, v_ref, o_ref):
    i = pl.program_id(2)
    t = SB_TILE
    q = q_ref[0]
    row = lax.broadcasted_iota(jnp.int32, (t, t), 0)
    col = lax.broadcasted_iota(jnp.int32, (t, t), 1)
    below = row > col
    suffix_ones = below.astype(BF16)

    def block(j, carry, diagonal):
        o, later = carry
        start = pl.multiple_of(j * t, t)
        kj = k_ref[0, pl.ds(start, t), :]
        vj = v_ref[0, pl.ds(start, t), :]
        z = _dot_nt(q, kj)
        softplus = jnp.maximum(z, 0.0) + jnp.log(1.0 + jnp.exp(-jnp.abs(z)))
        log_om = -softplus
        if diagonal:
            log_om = jnp.where(below, log_om, 0.0)
        hi, lo = _split_bf16(log_om)
        between = _dot(hi, suffix_ones) + _dot(lo, suffix_ones) + later
        a = jnp.exp(z - softplus + between)
        if diagonal:
            a = jnp.where(below, a, 0.0)
        o = o + _dot(a.astype(BF16), vj)
        later = later + jnp.sum(log_om, axis=1, keepdims=True)
        return o, later

    carry = (jnp.zeros((t, HEAD_DIM), F32), jnp.zeros((t, 1), F32))
    o, later = block(i, carry, True)

    def more(state):
        j, _, later = state
        return (j >= 0) & (jnp.max(later) > SB_UNDERFLOW_LOG)

    def step(state):
        j, o, later = state
        o, later = block(j, (o, later), False)
        return j - 1, o, later

    _, o, _ = lax.while_loop(more, step, (i - 1, o, later))
    o_ref[0] = o.astype(o_ref.dtype)


def _sb_attention(qkv):
    b, s, _ = qkv.shape
    h = N_HEADS
    return pl.pallas_call(
        _sb_kernel,
        grid=(b, h, s // SB_TILE),
        in_specs=[
            pl.BlockSpec((1, SB_TILE, LANES), lambda bi, hi, i: (bi, i, hi)),
            pl.BlockSpec((1, s, LANES), lambda bi, hi, i: (bi, 0, h + hi)),
            pl.BlockSpec((1, s, LANES), lambda bi, hi, i: (bi, 0, 2 * h + hi)),
        ],
        out_specs=pl.BlockSpec((1, SB_TILE, LANES), lambda bi, hi, i: (bi, i, hi)),
        out_shape=jax.ShapeDtypeStruct((b, s, h * HEAD_DIM), BF16),
        compiler_params=_cparams("parallel", "parallel", "arbitrary"),
        name="stick_breaking_attn",
    )(qkv, qkv, qkv)


def _online_softmax_step(carry, s, v):
    m, l, o = carry
    m_new = jnp.maximum(m, jnp.max(s, axis=1, keepdims=True))
    alpha = jnp.exp(m - m_new)
    p = jnp.exp(s - m_new)
    l = alpha * l + jnp.sum(p, axis=1, keepdims=True)
    o = alpha * o + _dot(p.astype(BF16), v)
    return m_new, l, o


MOBA_TQ = 512


def _moba_kernel(q_ref, k_ref, v_ref, o_ref, kaug_ref, kmean_ref, *, n_blk):
    i = pl.program_id(2)
    t, bs = MOBA_TQ, MOBA_BLOCK
    s_len = k_ref.shape[1]

    @pl.when(i == 0)
    def _():
        k = k_ref[0]
        kaug_ref[:, :LANES] = k
        blk = lax.broadcasted_iota(jnp.int32, (s_len, LANES), 0) // bs
        lane = lax.broadcasted_iota(jnp.int32, (s_len, LANES), 1)
        kaug_ref[:, LANES:] = (blk == lane).astype(BF16)
        kmean_ref[...] = jnp.mean(k.astype(F32).reshape(n_blk, bs, LANES), axis=1)

    q = q_ref[0]
    km_hi, km_lo = _split_bf16(kmean_ref[...])
    gate = _dot_nt(km_hi, q) + _dot_nt(km_lo, q)
    blk_id = lax.broadcasted_iota(jnp.int32, (n_blk, t), 0)
    cur = i * (t // bs) + lax.broadcasted_iota(jnp.int32, (n_blk, t), 1) // bs
    past = blk_id < cur
    gate = jnp.where(past, gate, -jnp.inf)
    rank = jnp.zeros((n_blk, t), jnp.int32)
    for m in range(n_blk):
        gm = gate[m:m + 1, :]
        beats = (gm > gate) | ((gm == gate) & (m < blk_id))
        rank = rank + beats.astype(jnp.int32)
    visible = ((rank < MOBA_TOPK) & past) | (blk_id == cur)
    sel_bias = jnp.where(visible, 0.0, NEG_INF)
    sel_bias = jnp.concatenate([sel_bias, jnp.zeros((LANES - n_blk, t), F32)], axis=0)
    q_aug = jnp.concatenate([q, sel_bias.T.astype(BF16)], axis=1)

    def scores(start):
        return _dot_nt(q_aug, kaug_ref[pl.ds(start, t), :])

    start = pl.multiple_of(i * t, t)
    row = lax.broadcasted_iota(jnp.int32, (t, t), 0)
    col = lax.broadcasted_iota(jnp.int32, (t, t), 1)
    s = jnp.where(col <= row, scores(start), NEG_INF)
    m0 = jnp.max(s, axis=1, keepdims=True)
    p = jnp.exp(s - m0)
    carry = (m0, jnp.sum(p, axis=1, keepdims=True), _dot(p.astype(BF16), v_ref[0, pl.ds(start, t), :]))

    def past_step(n, carry):
        st = pl.multiple_of(n * t, t)
        return _online_softmax_step(carry, scores(st), v_ref[0, pl.ds(st, t), :])

    _, l, o = lax.fori_loop(0, i, past_step, carry)
    o_ref[0] = (o / l).astype(o_ref.dtype)


def _moba_attention(qkv):
    b, s, _ = qkv.shape
    h = N_HEADS
    assert s % MOBA_TQ == 0 and MOBA_TQ % MOBA_BLOCK == 0
    n_blk = s // MOBA_BLOCK
    assert n_blk % 8 == 0 and n_blk <= LANES
    return pl.pallas_call(
        functools.partial(_moba_kernel, n_blk=n_blk),
        grid=(b, h, s // MOBA_TQ),
        in_specs=[
            pl.BlockSpec((1, MOBA_TQ, LANES), lambda bi, hi, i: (bi, i, hi)),
            pl.BlockSpec((1, s, LANES), lambda bi, hi, i: (bi, 0, h + hi)),
            pl.BlockSpec((1, s, LANES), lambda bi, hi, i: (bi, 0, 2 * h + hi)),
        ],
        out_specs=pl.BlockSpec((1, MOBA_TQ, LANES), lambda bi, hi, i: (bi, i, hi)),
        out_shape=jax.ShapeDtypeStruct((b, s, h * HEAD_DIM), BF16),
        scratch_shapes=[pltpu.VMEM((s, 2 * LANES), BF16), pltpu.VMEM((n_blk, LANES), F32)],
        compiler_params=_cparams("parallel", "parallel", "arbitrary"),
        name="moba_attn",
    )(qkv, qkv, qkv)


NSA_TQ = 128
NSA_TK = 512
CMP_HALF = NSA_CMP_STRIDE * HEAD_DIM


def _cmp_kernel(xk_ref, xv_ref, pos_ref, w_ref, ok_ref, ov_ref):
    n16 = xk_ref.shape[2]
    for t, (x_ref, o_ref) in enumerate(((xk_ref, ok_ref), (xv_ref, ov_ref))):
        x = x_ref[0, 0].astype(F32)
        first = _dot((x + pos_ref[t, 0:1, :]).astype(BF16), w_ref[t, 0])
        second = _dot((x + pos_ref[t, 1:2, :]).astype(BF16), w_ref[t, 1])
        o_ref[0, 0] = (first + pltpu.roll(second, n16 - 1, 0)).astype(o_ref.dtype)


def _nsa_compress(kc, vc, cmp_pos, w_cmp):
    b, g, n16, _ = kc.shape
    pos = cmp_pos.reshape(2, 2, CMP_HALF).astype(F32)
    w = w_cmp.reshape(2, 2, CMP_HALF, HEAD_DIM).astype(BF16)
    x_spec = pl.BlockSpec((1, 1, n16, CMP_HALF), lambda bi, gi: (bi, gi, 0, 0))
    o_spec = pl.BlockSpec((1, 1, n16, HEAD_DIM), lambda bi, gi: (bi, gi, 0, 0))
    return pl.pallas_call(
        _cmp_kernel,
        grid=(b, g),
        in_specs=[x_spec, x_spec, _const_spec(pos.shape), _const_spec(w.shape)],
        out_specs=[o_spec, o_spec],
        out_shape=[jax.ShapeDtypeStruct((b, g, n16, HEAD_DIM), BF16)] * 2,
        compiler_params=_cparams("parallel", "parallel"),
        name="nsa_compress",
    )(kc, vc, pos, w)


def _masked_softmax_rows(s, mask):
    z = jnp.where(mask, s, NEG_INF)
    e = jnp.where(mask, jnp.exp(z - jnp.max(z, axis=1, keepdims=True)), 0.0)
    l = jnp.sum(e, axis=1, keepdims=True)
    return e / jnp.where(l > 0.0, l, 1.0)


def _nsa_kernel(q_ref, kc_ref, vc_ref, ks_ref, vs_ref, kw_ref, vw_ref, gate_ref, ov_ref, o_ref,
                ksaug_ref, *, n_sel):
    i = pl.program_id(2)
    tq, tk, hg = NSA_TQ, NSA_TK, NSA_HEADS_PER_GROUP
    rows = hg * tq
    s_len = ks_ref.shape[1]
    n_cmp_pad = kc_ref.shape[2]
    c0 = i * tq

    @pl.when(i == 0)
    def _():
        ksaug_ref[:, :LANES] = ks_ref[0]
        blk = lax.broadcasted_iota(jnp.int32, (s_len, LANES), 0) // NSA_SEL_BLOCK
        lane = lax.broadcasted_iota(jnp.int32, (s_len, LANES), 1)
        ksaug_ref[:, LANES:] = (blk == lane).astype(BF16)

    q_all = q_ref[0]
    q4 = jnp.concatenate([q_all[:, h * LANES:(h + 1) * LANES] for h in range(hg)], axis=0)

    def row_pos(width):
        r = lax.broadcasted_iota(jnp.int32, (rows, width), 0)
        return c0 + (r & (tq - 1))

    s_c = _dot_nt(q4, kc_ref[0, 0])
    n_idx = lax.broadcasted_iota(jnp.int32, (rows, n_cmp_pad), 1)
    p_c = _masked_softmax_rows(
        s_c, n_idx * NSA_CMP_STRIDE + (NSA_CMP_BLOCK - 1) <= row_pos(n_cmp_pad))
    o_c = _dot(p_c.astype(BF16), vc_ref[0, 0])

    p_sum = p_c[0:tq]
    for h in range(1, hg):
        p_sum = p_sum + p_c[h * tq:(h + 1) * tq]
    ps_hi, ps_lo = _split_bf16(p_sum)
    imp = _dot(ps_hi, ov_ref[...]) + _dot(ps_lo, ov_ref[...])
    blk = lax.broadcasted_iota(jnp.int32, (tq, LANES), 1)
    cur = (c0 + lax.broadcasted_iota(jnp.int32, (tq, LANES), 0)) // NSA_SEL_BLOCK
    forced = (blk == 0) | (blk == cur) | (blk == cur - 1)
    score = jnp.where(blk <= cur, imp + NSA_FORCE_BONUS * forced.astype(F32), -1.0)
    score = jnp.where(blk < n_sel, score, -jnp.inf)
    lane_f = blk.astype(F32)
    chosen = jnp.zeros((tq, LANES), jnp.bool_)
    for _ in range(min(NSA_SEL_TOPK, n_sel)):
        best = jnp.max(score, axis=1, keepdims=True)
        first = jnp.min(jnp.where(score == best, lane_f, float(LANES)), axis=1, keepdims=True)
        hit = lane_f == first
        chosen = chosen | hit
        score = jnp.where(hit, -jnp.inf, score)
    sel_bias = jnp.where(chosen, 0.0, NEG_INF).astype(BF16)
    q_aug = jnp.concatenate([q4, jnp.concatenate([sel_bias] * hg, axis=0)], axis=1)

    jd = c0 // tk
    dstart = pl.multiple_of(jd * tk, tk)
    key_pos = dstart + lax.broadcasted_iota(jnp.int32, (rows, tk), 1)
    s = _dot_nt(q_aug, ksaug_ref[pl.ds(dstart, tk), :])
    s = jnp.where(key_pos <= row_pos(tk), s, NEG_INF)
    m0 = jnp.max(s, axis=1, keepdims=True)
    p = jnp.exp(s - m0)
    carry = (m0, jnp.sum(p, axis=1, keepdims=True), _dot(p.astype(BF16), vs_ref[0, pl.ds(dstart, tk), :]))

    def sel_block(j, carry):
        st = pl.multiple_of(j * tk, tk)
        s = _dot_nt(q_aug, ksaug_ref[pl.ds(st, tk), :])
        return _online_softmax_step(carry, s, vs_ref[0, pl.ds(st, tk), :])

    _, l_s, o_s = lax.fori_loop(0, jd, sel_block, carry)
    o_s = o_s / l_s

    span = tq + NSA_WINDOW
    wstart = pl.multiple_of(jnp.maximum(c0 - NSA_WINDOW, 0), tq)
    w_pos = wstart + lax.broadcasted_iota(jnp.int32, (rows, span), 1)
    t_pos = row_pos(span)
    s_w = _dot_nt(q4, kw_ref[0, pl.ds(wstart, span), :])
    p_w = _masked_softmax_rows(s_w, (w_pos <= t_pos) & (t_pos - w_pos < NSA_WINDOW))
    o_w = _dot(p_w.astype(BF16), vw_ref[0, pl.ds(wstart, span), :])

    gates = 1.0 / (1.0 + jnp.exp(-gate_ref[0]))
    g_idx = pl.program_id(1)
    lane = lax.broadcasted_iota(jnp.int32, (tq, LANES), 1)
    for h in range(hg):
        head = g_idx * hg + h
        sl = slice(h * tq, (h + 1) * tq)
        out = jnp.zeros((tq, HEAD_DIM), F32)
        for branch, o_b in enumerate((o_c, o_s, o_w)):
            g_col = jnp.sum(jnp.where(lane == branch * N_HEADS + head, gates, 0.0), axis=1, keepdims=True)
            out = out + g_col * o_b[sl]
        o_ref[0, :, h * LANES:(h + 1) * LANES] = out.astype(o_ref.dtype)


def _nsa_attention(main, gate_logits, kc_cmp, vc_cmp, overlap):
    b, s, _ = main.shape
    g, hg = NSA_KV_GROUPS, NSA_HEADS_PER_GROUP
    n_sel = s // NSA_SEL_BLOCK
    assert s % NSA_TK == 0 and n_sel <= LANES
    n16 = kc_cmp.shape[2]
    q_blocks = N_HEADS

    def kv_spec(which):
        return pl.BlockSpec((1, s, LANES), lambda bi, gi, i: (bi, 0, q_blocks + which * g + gi))

    cmp_spec = pl.BlockSpec((1, 1, n16, HEAD_DIM), lambda bi, gi, i: (bi, gi, 0, 0))
    return pl.pallas_call(
        functools.partial(_nsa_kernel, n_sel=n_sel),
        grid=(b, g, s // NSA_TQ),
        in_specs=[
            pl.BlockSpec((1, NSA_TQ, hg * LANES), lambda bi, gi, i: (bi, i, gi)),
            cmp_spec, cmp_spec,
            kv_spec(2), kv_spec(3), kv_spec(4), kv_spec(5),
            pl.BlockSpec((1, NSA_TQ, LANES), lambda bi, gi, i: (bi, i, 0)),
            _const_spec(overlap.shape),
        ],
        out_specs=pl.BlockSpec((1, NSA_TQ, hg * LANES), lambda bi, gi, i: (bi, i, gi)),
        out_shape=jax.ShapeDtypeStruct((b, s, N_HEADS * HEAD_DIM), BF16),
        scratch_shapes=[pltpu.VMEM((s, 2 * LANES), BF16)],
        compiler_params=_cparams("parallel", "parallel", "arbitrary"),
        name="nsa_attn",
    )(main, kc_cmp, vc_cmp, main, main, main, main, gate_logits, overlap)


def _nsa_overlap(seq):
    n_cmp = (seq - NSA_CMP_BLOCK) // NSA_CMP_STRIDE + 1
    n_sel = seq // NSA_SEL_BLOCK
    cmp_start = np.arange(seq // NSA_CMP_STRIDE) * NSA_CMP_STRIDE
    sel_start = np.arange(LANES) * NSA_SEL_BLOCK
    ov = ((cmp_start[:, None] < sel_start[None, :] + NSA_SEL_BLOCK)
          & (cmp_start[:, None] + NSA_CMP_BLOCK > sel_start[None, :]))
    ov &= (np.arange(seq // NSA_CMP_STRIDE)[:, None] < n_cmp) & (np.arange(LANES)[None, :] < n_sel)
    return jnp.asarray(ov, dtype=BF16)


def _glu_proj_kernel(x_ref, g_ref, w_ref, o_ref):
    h = _rms(x_ref[...], g_ref[...]).astype(BF16)
    d = o_ref.shape[1]
    for c in range(d // PROJ_CHUNK):
        a = _dot(h, w_ref[:, c * PROJ_CHUNK:(c + 1) * PROJ_CHUNK])
        gate = _dot(h, w_ref[:, d + c * PROJ_CHUNK:d + (c + 1) * PROJ_CHUNK])
        o_ref[:, c * PROJ_CHUNK:(c + 1) * PROJ_CHUNK] = (a / (1.0 + jnp.exp(-gate))).astype(o_ref.dtype)


def _glu_project(x2d, gain, w):
    t_rows, d = x2d.shape
    return pl.pallas_call(
        _glu_proj_kernel,
        grid=(t_rows // ROW_TILE,),
        in_specs=[pl.BlockSpec((ROW_TILE, d), lambda i: (i, 0)), _const_spec((1, d)), _const_spec(w.shape)],
        out_specs=pl.BlockSpec((ROW_TILE, d), lambda i: (i, 0)),
        out_shape=jax.ShapeDtypeStruct((t_rows, d), BF16),
        compiler_params=_cparams("parallel"),
        name="norm_glu_proj",
    )(x2d, gain.reshape(1, d), w)


CONV_TILE = 256


def _conv_kernel(u_ref, halo_ref, dw_ref, db_ref, lg_ref, lb_ref, o_ref, ext_ref):
    i = pl.program_id(1)
    halo = halo_ref[0].astype(F32)
    ext_ref[0:CONV_HALO, :] = jnp.where(i == 0, 0.0, halo)
    ext_ref[CONV_HALO:, :] = u_ref[0].astype(F32)
    lead = CONV_HALO - (CONV_WIDTH - 1)
    for r in range(CONV_TILE // CONV_ROWS):
        acc = jnp.zeros((CONV_ROWS, D_MODEL), F32) + db_ref[...]
        for w in range(CONV_WIDTH):
            acc = acc + dw_ref[w:w + 1, :] * ext_ref[pl.ds(r * CONV_ROWS + lead + w, CONV_ROWS), :]
        mu = jnp.mean(acc, axis=-1, keepdims=True)
        cen = acc - mu
        var = jnp.mean(cen * cen, axis=-1, keepdims=True)
        un = cen * lax.rsqrt(var + NORM_EPS) * lg_ref[...] + lb_ref[...]
        o_ref[0, r * CONV_ROWS:(r + 1) * CONV_ROWS, :] = (un / (1.0 + jnp.exp(-un))).astype(o_ref.dtype)


def _conv_ln_swish(u, dw_w, dw_b, ln_g, ln_b):
    b, s, d = u.shape
    halo_per_tile = CONV_TILE // CONV_HALO
    dw = jnp.concatenate([dw_w, jnp.zeros((1, d), F32)], axis=0)
    return pl.pallas_call(
        _conv_kernel,
        grid=(b, s // CONV_TILE),
        in_specs=[
            pl.BlockSpec((1, CONV_TILE, d), lambda bi, i: (bi, i, 0)),
            pl.BlockSpec((1, CONV_HALO, d), lambda bi, i: (bi, jnp.maximum(i * halo_per_tile - 1, 0), 0)),
            _const_spec(dw.shape), _const_spec((1, d)), _const_spec((1, d)), _const_spec((1, d)),
        ],
        out_specs=pl.BlockSpec((1, CONV_TILE, d), lambda bi, i: (bi, i, 0)),
        out_shape=jax.ShapeDtypeStruct((b, s, d), BF16),
        scratch_shapes=[pltpu.VMEM((CONV_TILE + CONV_HALO, d), F32)],
        compiler_params=_cparams("parallel", "parallel"),
        name="conv_ln_swish",
    )(u, u, dw, dw_b.reshape(1, d), ln_g.reshape(1, d), ln_b.reshape(1, d))


def _rope_tables(seq):
    half = HEAD_DIM // 2
    inv_freq = ROPE_THETA ** (-jnp.arange(half, dtype=F32) / half)
    ang = jnp.arange(seq, dtype=F32)[:, None] * inv_freq[None, :]
    cos, sin = jnp.cos(ang), jnp.sin(ang)
    return jnp.concatenate([cos, cos], axis=1), jnp.concatenate([-sin, sin], axis=1)


def _nsa_mixer(x2d, norm_gain, w_in, q_gain, k_gain, cmp_pos, w_cmp, tables, batch, seq):
    g, dh = NSA_KV_GROUPS, HEAD_DIM
    n_main = (N_HEADS + 6 * g) * dh
    scale = dh ** -0.5
    w_main = w_in[:, :n_main].astype(BF16)
    w_gate = jnp.pad(w_in[:, n_main:], ((0, 0), (0, LANES - 3 * N_HEADS))).astype(BF16)
    ones = jnp.ones((dh,), F32)
    head_gains = jnp.stack([q_gain * scale] * N_HEADS + [k_gain[0]] * g + [ones] * g
                           + [k_gain[1]] * g + [ones] * g + [k_gain[2]] * g + [ones] * g)
    rope_blocks = [True] * N_HEADS + [True] * g + [False] * g + [True] * g + [False] * g + [True] * g + [False] * g
    main, gate_logits = _project(x2d, norm_gain, w_main, *tables, head_gains, rope_blocks, seq, w_gate=w_gate)
    main = main.reshape(batch, seq, n_main)
    gate_logits = gate_logits.reshape(batch, seq, LANES)

    def cmp_layout(col0):
        t = main[:, :, col0:col0 + g * dh].reshape(batch, seq // NSA_CMP_STRIDE, NSA_CMP_STRIDE, g, dh)
        return t.transpose(0, 3, 1, 2, 4).reshape(batch, g, seq // NSA_CMP_STRIDE, CMP_HALF)

    kc_cmp, vc_cmp = _nsa_compress(cmp_layout(N_HEADS * dh), cmp_layout((N_HEADS + g) * dh), cmp_pos, w_cmp)
    out = _nsa_attention(main, gate_logits, kc_cmp, vc_cmp, _nsa_overlap(seq))
    return out.reshape(batch * seq, N_HEADS * dh)


def _qkv_weight_with_scaled_q(w_in):
    n_q = N_HEADS * HEAD_DIM
    return jnp.concatenate([w_in[:, :n_q] * HEAD_DIM ** -0.5, w_in[:, n_q:]], axis=1).astype(BF16)


def _sb_mixer(x2d, norm_gain, w_in, tables, batch, seq):
    n = 3 * N_HEADS * HEAD_DIM
    head_gains = jnp.ones((n // LANES, HEAD_DIM), F32)
    qkv = _project(x2d, norm_gain, _qkv_weight_with_scaled_q(w_in), *tables, head_gains,
                   [False] * (n // LANES), seq)
    return _sb_attention(qkv.reshape(batch, seq, n)).reshape(batch * seq, N_HEADS * HEAD_DIM)


def _conv_mixer(x2d, norm_gain, w_in, dw_w, dw_b, ln_g, ln_b, batch, seq):
    u = _glu_project(x2d, norm_gain, w_in.astype(BF16))
    a = _conv_ln_swish(u.reshape(batch, seq, D_MODEL), dw_w, dw_b, ln_g, ln_b)
    return a.reshape(batch * seq, D_MODEL)


def _moba_mixer(x2d, norm_gain, w_in, q_gain, k_gain, tables, batch, seq):
    n = 3 * N_HEADS * HEAD_DIM
    ones = jnp.ones((HEAD_DIM,), F32)
    head_gains = jnp.stack([q_gain * HEAD_DIM ** -0.5] * N_HEADS + [k_gain] * N_HEADS + [ones] * N_HEADS)
    rope_blocks = [True] * (2 * N_HEADS) + [False] * N_HEADS
    qkv = _project(x2d, norm_gain, w_in.astype(BF16), *tables, head_gains, rope_blocks, seq)
    return _moba_attention(qkv.reshape(batch, seq, n)).reshape(batch * seq, N_HEADS * HEAD_DIM)


def kernel(x, attn_norm, mlp_norm, mlp_w_up, mlp_w_down, nsa_w_in, nsa_q_norm, nsa_k_norm, nsa_cmp_pos, nsa_w_cmp, nsa_w_out, sb_w_in, sb_w_out, conv_w_in, conv_dw_w, conv_dw_b, conv_ln_g, conv_ln_b, conv_w_out, moba_w_in, moba_q_norm, moba_k_norm, moba_w_out):
    batch, seq, d = x.shape
    depth = attn_norm.shape[0]
    tables = _rope_tables(seq)
    x2d = x.reshape(batch * seq, d)
    for i in range(depth):
        m, j = i % 4, i // 4
        if m == 0:
            a = _nsa_mixer(x2d, attn_norm[i], nsa_w_in[j], nsa_q_norm[j], nsa_k_norm[j], nsa_cmp_pos[j],
                           nsa_w_cmp[j], tables, batch, seq)
            w_out = nsa_w_out[j]
        elif m == 1:
            a = _sb_mixer(x2d, attn_norm[i], sb_w_in[j], tables, batch, seq)
            w_out = sb_w_out[j]
        elif m == 2:
            a = _conv_mixer(x2d, attn_norm[i], conv_w_in[j], conv_dw_w[j], conv_dw_b[j], conv_ln_g[j],
                            conv_ln_b[j], batch, seq)
            w_out = conv_w_out[j]
        else:
            a = _moba_mixer(x2d, attn_norm[i], moba_w_in[j], moba_q_norm[j], moba_k_norm[j], tables, batch, seq)
            w_out = moba_w_out[j]
        x2d = _mixer_out_and_mlp(x2d, a, w_out.astype(BF16), mlp_norm[i], mlp_w_up[i].astype(BF16),
                                 mlp_w_down[i].astype(BF16))
    return x2d.reshape(batch, seq, d)
```

```python
import functools

import numpy as np
import jax
import jax.numpy as jnp
from jax import lax
from jax.experimental import pallas as pl
from jax.experimental.pallas import tpu as pltpu

F32 = jnp.float32
BF16 = jnp.bfloat16

D_MODEL = 1024
N_HEADS = 8
HEAD_DIM = 128
D_FF = 4 * D_MODEL
ROPE_THETA = 10000.0
NORM_EPS = 1e-6
NEG_INF = -1e30

NSA_KV_GROUPS = 2
NSA_HEADS_PER_GROUP = N_HEADS // NSA_KV_GROUPS
NSA_CMP_BLOCK = 32
NSA_CMP_STRIDE = 16
NSA_SEL_BLOCK = 64
NSA_SEL_TOPK = 16
NSA_WINDOW = 512
NSA_FORCE_BONUS = 1000.0

CONV_WIDTH = 31
MOBA_BLOCK = 256
MOBA_TOPK = 3

LANES = 128
VMEM_LIMIT_BYTES = 56 * 1024 * 1024
ROW_TILE = 512
PROJ_CHUNK = 512
FF_CHUNK = 1024
CONV_HALO = 32
CONV_ROWS = 32


def _cparams(*sem):
    return pltpu.CompilerParams(dimension_semantics=sem, vmem_limit_bytes=VMEM_LIMIT_BYTES)


def _const_spec(shape):
    zeros = (0,) * len(shape)
    return pl.BlockSpec(shape, lambda *_: zeros, pipeline_mode=pl.Buffered(1))


def _rms(x, gain):
    return x * lax.rsqrt(jnp.mean(x * x, axis=-1, keepdims=True) + NORM_EPS) * gain


def _dot(a, b):
    return jnp.dot(a, b, preferred_element_type=F32)


def _dot_nt(a, b):
    return lax.dot_general(a, b, (((1,), (1,)), ((), ())), preferred_element_type=F32)


def _split_bf16(x):
    hi = x.astype(BF16)
    lo = (x - hi.astype(F32)).astype(BF16)
    return hi, lo


def _proj_kernel(*refs, rope_blocks, has_gate):
    if has_gate:
        x_ref, g_ref, w_ref, cos_ref, sin_ref, hg_ref, wg_ref, o_ref, og_ref = refs
    else:
        x_ref, g_ref, w_ref, cos_ref, sin_ref, hg_ref, o_ref = refs
    h = _rms(x_ref[...], g_ref[...]).astype(BF16)
    blocks_per_chunk = PROJ_CHUNK // LANES
    for c in range(len(rope_blocks) // blocks_per_chunk):
        y = _dot(h, w_ref[:, c * PROJ_CHUNK:(c + 1) * PROJ_CHUNK])
        chunk_flags = rope_blocks[c * blocks_per_chunk:(c + 1) * blocks_per_chunk]
        if not any(chunk_flags):
            o_ref[:, c * PROJ_CHUNK:(c + 1) * PROJ_CHUNK] = y.astype(o_ref.dtype)
            continue
        for k, flag in enumerate(chunk_flags):
            b = c * blocks_per_chunk + k
            yb = y[:, k * LANES:(k + 1) * LANES]
            if flag:
                yb = _rms(yb, hg_ref[b:b + 1, :])
                yb = yb * cos_ref[...] + pltpu.roll(yb, HEAD_DIM // 2, 1) * sin_ref[...]
            o_ref[:, b * LANES:(b + 1) * LANES] = yb.astype(o_ref.dtype)
    if has_gate:
        og_ref[...] = _dot(h, wg_ref[...])


def _project(x2d, gain, w, cos_t, sin_t, head_gains, rope_blocks, seq, w_gate=None):
    t_rows, d = x2d.shape
    n = w.shape[1]
    assert n % PROJ_CHUNK == 0 and len(rope_blocks) == n // LANES and seq % ROW_TILE == 0
    seq_tiles = seq // ROW_TILE
    has_gate = w_gate is not None
    in_specs = [
        pl.BlockSpec((ROW_TILE, d), lambda i: (i, 0)),
        _const_spec((1, d)),
        _const_spec((d, n)),
        pl.BlockSpec((ROW_TILE, LANES), lambda i: (i % seq_tiles, 0)),
        pl.BlockSpec((ROW_TILE, LANES), lambda i: (i % seq_tiles, 0)),
        _const_spec(head_gains.shape),
    ]
    args = [x2d, gain.reshape(1, d), w, cos_t, sin_t, head_gains]
    out_shape = [jax.ShapeDtypeStruct((t_rows, n), BF16)]
    out_specs = [pl.BlockSpec((ROW_TILE, n), lambda i: (i, 0))]
    if has_gate:
        in_specs.append(_const_spec(w_gate.shape))
        args.append(w_gate)
        out_shape.append(jax.ShapeDtypeStruct((t_rows, LANES), F32))
        out_specs.append(pl.BlockSpec((ROW_TILE, LANES), lambda i: (i, 0)))
    outs = pl.pallas_call(
        functools.partial(_proj_kernel, rope_blocks=tuple(rope_blocks), has_gate=has_gate),
        grid=(t_rows // ROW_TILE,),
        in_specs=in_specs,
        out_specs=out_specs,
        out_shape=out_shape,
        compiler_params=_cparams("parallel"),
        name="norm_proj",
    )(*args)
    return outs if has_gate else outs[0]


def _mlp_kernel(x_ref, a_ref, wo_ref, g_ref, wup_ref, wdn_ref, o_ref):
    x1 = x_ref[...] + _dot(a_ref[...], wo_ref[...])
    h = _rms(x1, g_ref[...]).astype(BF16)
    acc = x1
    for c in range(D_FF // FF_CHUNK):
        u = _dot(h, wup_ref[:, c * FF_CHUNK:(c + 1) * FF_CHUNK])
        act = jnp.square(jnp.maximum(u, 0.0)).astype(BF16)
        acc = acc + _dot(act, wdn_ref[c * FF_CHUNK:(c + 1) * FF_CHUNK, :])
    o_ref[...] = acc


def _mixer_out_and_mlp(x2d, a2d, w_out, gain, w_up, w_down):
    t_rows, d = x2d.shape
    return pl.pallas_call(
        _mlp_kernel,
        grid=(t_rows // ROW_TILE,),
        in_specs=[
            pl.BlockSpec((ROW_TILE, d), lambda i: (i, 0)),
            pl.BlockSpec((ROW_TILE, d), lambda i: (i, 0)),
            _const_spec((d, d)),
            _const_spec((1, d)),
            _const_spec((d, D_FF)),
            _const_spec((D_FF, d)),
        ],
        out_specs=pl.BlockSpec((ROW_TILE, d), lambda i: (i, 0)),
        out_shape=jax.ShapeDtypeStruct((t_rows, d), F32),
        compiler_params=_cparams("parallel"),
        name="outproj_mlp",
    )(x2d, a2d, w_out, gain.reshape(1, d), w_up, w_down)


SB_TILE = 256
SB_UNDERFLOW_LOG = -104.0


def _sb_kernel(q_ref, k_ref, v_ref, o_ref):
    i = pl.program_id(2)
    t = SB_TILE
    q = q_ref[0]
    row = lax.broadcasted_iota(jnp.int32, (t, t), 0)
    col = lax.broadcasted_iota(jnp.int32, (t, t), 1)
    below = row > col
    suffix_ones = below.astype(BF16)

    def block(j, carry, diagonal):
        o, later = carry
        start = pl.multiple_of(j * t, t)
        kj = k_ref[0, pl.ds(start, t), :]
        vj = v_ref[0, pl.ds(start, t), :]
        z = _dot_nt(q, kj)
        softplus = jnp.maximum(z, 0.0) + jnp.log(1.0 + jnp.exp(-jnp.abs(z)))
        log_om = -softplus
        if diagonal:
            log_om = jnp.where(below, log_om, 0.0)
        hi, lo = _split_bf16(log_om)
        between = _dot(hi, suffix_ones) + _dot(lo, suffix_ones) + later
        a = jnp.exp(z - softplus + between)
        if diagonal:
            a = jnp.where(below, a, 0.0)
        o = o + _dot(a.astype(BF16), vj)
        later = later + jnp.sum(log_om, axis=1, keepdims=True)
        return o, later

    carry = (jnp.zeros((t, HEAD_DIM), F32), jnp.zeros((t, 1), F32))
    o, later = block(i, carry, True)

    def more(state):
        j, _, later = state
        return (j >= 0) & (jnp.max(later) > SB_UNDERFLOW_LOG)

    def step(state):
        j, o, later = state
        o, later = block(j, (o, later), False)
        return j - 1, o, later

    _, o, _ = lax.while_loop(more, step, (i - 1, o, later))
    o_ref[0] = o.astype(o_ref.dtype)


def _sb_attention(qkv):
    b, s, _ = qkv.shape
    h = N_HEADS
    return pl.pallas_call(
        _sb_kernel,
        grid=(b, h, s // SB_TILE),
        in_specs=[
            pl.BlockSpec((1, SB_TILE, LANES), lambda bi, hi, i: (bi, i, hi)),
            pl.BlockSpec((1, s, LANES), lambda bi, hi, i: (bi, 0, h + hi)),
            pl.BlockSpec((1, s, LANES), lambda bi, hi, i: (bi, 0, 2 * h + hi)),
        ],
        out_specs=pl.BlockSpec((1, SB_TILE, LANES), lambda bi, hi, i: (bi, i, hi)),
        out_shape=jax.ShapeDtypeStruct((b, s, h * HEAD_DIM), BF16),
        compiler_params=_cparams("parallel", "parallel", "arbitrary"),
        name="stick_breaking_attn",
    )(qkv, qkv, qkv)


def _online_softmax_step(carry, s, v):
    m, l, o = carry
    m_new = jnp.maximum(m, jnp.max(s, axis=1, keepdims=True))
    alpha = jnp.exp(m - m_new)
    p = jnp.exp(s - m_new)
    l = alpha * l + jnp.sum(p, axis=1, keepdims=True)
    o = alpha * o + _dot(p.astype(BF16), v)
    return m_new, l, o


MOBA_TQ = 1024
MOBA_TK = 512
MOBA_CHAINS = 2


def _flash_loop(n_steps, q_aug, keys_fn, values_fn, carry, n_chains):
    r = q_aug.shape[0] // n_chains
    q_parts = [q_aug[c * r:(c + 1) * r] for c in range(n_chains)]
    state = tuple(tuple(x[c * r:(c + 1) * r] for x in carry) for c in range(n_chains))

    def body(n, state):
        k, v = keys_fn(n), values_fn(n)
        scores = [_dot_nt(qp, k) for qp in q_parts]
        return tuple(_online_softmax_step(st, s, v) for st, s in zip(state, scores))

    state = lax.fori_loop(0, n_steps, body, state)
    return tuple(jnp.concatenate([st[x] for st in state], axis=0) for x in range(3))


def _moba_kernel(q_ref, k_ref, v_ref, o_ref, kaug_ref, kmean_ref, *, n_blk):
    i = pl.program_id(2)
    t, bs = MOBA_TQ, MOBA_BLOCK
    s_len = k_ref.shape[1]

    @pl.when(i == 0)
    def _():
        k = k_ref[0]
        kaug_ref[:, :LANES] = k
        blk = lax.broadcasted_iota(jnp.int32, (s_len, LANES), 0) // bs
        lane = lax.broadcasted_iota(jnp.int32, (s_len, LANES), 1)
        kaug_ref[:, LANES:] = (blk == lane).astype(BF16)
        kmean_ref[...] = jnp.mean(k.astype(F32).reshape(n_blk, bs, LANES), axis=1)

    q = q_ref[0]
    km_hi, km_lo = _split_bf16(kmean_ref[...])
    gate = _dot_nt(km_hi, q) + _dot_nt(km_lo, q)
    blk_id = lax.broadcasted_iota(jnp.int32, (n_blk, t), 0)
    cur = i * (t // bs) + lax.broadcasted_iota(jnp.int32, (n_blk, t), 1) // bs
    past = blk_id < cur
    gate = jnp.where(past, gate, -jnp.inf)
    rank = jnp.zeros((n_blk, t), jnp.int32)
    for m in range(n_blk):
        gm = gate[m:m + 1, :]
        beats = (gm > gate) | ((gm == gate) & (m < blk_id))
        rank = rank + beats.astype(jnp.int32)
    visible = ((rank < MOBA_TOPK) & past) | (blk_id == cur)
    sel_bias = jnp.where(visible, 0.0, NEG_INF)
    sel_bias = jnp.concatenate([sel_bias, jnp.zeros((LANES - n_blk, t), F32)], axis=0)
    q_aug = jnp.concatenate([q, sel_bias.T.astype(BF16)], axis=1)

    tk = MOBA_TK

    def keys(n):
        return kaug_ref[pl.ds(pl.multiple_of(n * tk, tk), tk), :]

    def values(n):
        return v_ref[0, pl.ds(pl.multiple_of(n * tk, tk), tk), :]

    own = i * (t // tk)
    row = i * t + lax.broadcasted_iota(jnp.int32, (t, tk), 0)
    col = lax.broadcasted_iota(jnp.int32, (t, tk), 1)

    def causal_scores(n):
        return jnp.where(n * tk + col <= row, _dot_nt(q_aug, keys(n)), NEG_INF)

    s = causal_scores(own)
    m0 = jnp.max(s, axis=1, keepdims=True)
    p = jnp.exp(s - m0)
    carry = (m0, jnp.sum(p, axis=1, keepdims=True), _dot(p.astype(BF16), values(own)))
    for extra in range(1, t // tk):
        carry = _online_softmax_step(carry, causal_scores(own + extra), values(own + extra))

    _, l, o = _flash_loop(own, q_aug, keys, values, carry, MOBA_CHAINS)
    o_ref[0] = (o * (1.0 / l)).astype(o_ref.dtype)


def _moba_attention(qkv):
    b, s, _ = qkv.shape
    h = N_HEADS
    assert s % MOBA_TQ == 0 and MOBA_TQ % MOBA_BLOCK == 0
    n_blk = s // MOBA_BLOCK
    assert n_blk % 8 == 0 and n_blk <= LANES
    return pl.pallas_call(
        functools.partial(_moba_kernel, n_blk=n_blk),
        grid=(b, h, s // MOBA_TQ),
        in_specs=[
            pl.BlockSpec((1, MOBA_TQ, LANES), lambda bi, hi, i: (bi, i, hi)),
            pl.BlockSpec((1, s, LANES), lambda bi, hi, i: (bi, 0, h + hi)),
            pl.BlockSpec((1, s, LANES), lambda bi, hi, i: (bi, 0, 2 * h + hi)),
        ],
        out_specs=pl.BlockSpec((1, MOBA_TQ, LANES), lambda bi, hi, i: (bi, i, hi)),
        out_shape=jax.ShapeDtypeStruct((b, s, h * HEAD_DIM), BF16),
        scratch_shapes=[pltpu.VMEM((s, 2 * LANES), BF16), pltpu.VMEM((n_blk, LANES), F32)],
        compiler_params=_cparams("parallel", "parallel", "arbitrary"),
        name="moba_attn",
    )(qkv, qkv, qkv)


NSA_TQ = 256
NSA_TK = 512
NSA_CHAINS = 2
CMP_HALF = NSA_CMP_STRIDE * HEAD_DIM


def _cmp_kernel(xk_ref, xv_ref, pos_ref, w_ref, ok_ref, ov_ref):
    n16 = xk_ref.shape[2]
    for t, (x_ref, o_ref) in enumerate(((xk_ref, ok_ref), (xv_ref, ov_ref))):
        x = x_ref[0, 0].astype(F32)
        first = _dot((x + pos_ref[t, 0:1, :]).astype(BF16), w_ref[t, 0])
        second = _dot((x + pos_ref[t, 1:2, :]).astype(BF16), w_ref[t, 1])
        o_ref[0, 0] = (first + pltpu.roll(second, n16 - 1, 0)).astype(o_ref.dtype)


def _nsa_compress(kc, vc, cmp_pos, w_cmp):
    b, g, n16, _ = kc.shape
    pos = cmp_pos.reshape(2, 2, CMP_HALF).astype(F32)
    w = w_cmp.reshape(2, 2, CMP_HALF, HEAD_DIM).astype(BF16)
    x_spec = pl.BlockSpec((1, 1, n16, CMP_HALF), lambda bi, gi: (bi, gi, 0, 0))
    o_spec = pl.BlockSpec((1, 1, n16, HEAD_DIM), lambda bi, gi: (bi, gi, 0, 0))
    return pl.pallas_call(
        _cmp_kernel,
        grid=(b, g),
        in_specs=[x_spec, x_spec, _const_spec(pos.shape), _const_spec(w.shape)],
        out_specs=[o_spec, o_spec],
        out_shape=[jax.ShapeDtypeStruct((b, g, n16, HEAD_DIM), BF16)] * 2,
        compiler_params=_cparams("parallel", "parallel"),
        name="nsa_compress",
    )(kc, vc, pos, w)


def _masked_softmax_rows(s, mask):
    tq, w = mask.shape
    s3 = s.reshape(s.shape[0] // tq, tq, w)
    z = jnp.where(mask[None], s3, NEG_INF)
    e = jnp.where(mask[None], jnp.exp(z - jnp.max(z, axis=2, keepdims=True)), 0.0)
    l = jnp.sum(e, axis=2, keepdims=True)
    return (e * (1.0 / jnp.where(l > 0.0, l, 1.0))).reshape(s.shape)


def _top_k_mask(score, k):
    rows = score.shape[0]
    work = score
    taken = jnp.zeros((rows, 1), F32)
    level = jnp.full((rows, 1), jnp.inf, F32)
    above = jnp.zeros((rows, 1), F32)
    for _ in range(k):
        best = jnp.max(work, axis=1, keepdims=True)
        hit = work == best
        active = taken < k
        level = jnp.where(active, best, level)
        above = jnp.where(active, taken, above)
        taken = taken + jnp.sum(hit.astype(F32), axis=1, keepdims=True)
        work = jnp.where(hit, -jnp.inf, work)
    tie = score == level
    lower_lane = (lax.broadcasted_iota(jnp.int32, (LANES, LANES), 0)
                  < lax.broadcasted_iota(jnp.int32, (LANES, LANES), 1)).astype(BF16)
    ties_before = _dot(tie.astype(BF16), lower_lane)
    return (score > level) | (tie & (ties_before < k - above))


def _nsa_kernel(q_ref, kc_ref, vc_ref, ks_ref, vs_ref, kw_ref, vw_ref, gate_ref, ov_ref, gx_ref, o_ref,
                ksaug_ref, *, n_sel):
    i = pl.program_id(2)
    tq, tk, hg = NSA_TQ, NSA_TK, NSA_HEADS_PER_GROUP
    rows = hg * tq
    s_len = ks_ref.shape[1]
    n_cmp_pad = kc_ref.shape[2]
    c0 = i * tq

    @pl.when(i == 0)
    def _():
        ksaug_ref[:, :LANES] = ks_ref[0]
        blk = lax.broadcasted_iota(jnp.int32, (s_len, LANES), 0) // NSA_SEL_BLOCK
        lane = lax.broadcasted_iota(jnp.int32, (s_len, LANES), 1)
        ksaug_ref[:, LANES:] = (blk == lane).astype(BF16)

    q_all = q_ref[0]
    q4 = jnp.concatenate([q_all[:, h * LANES:(h + 1) * LANES] for h in range(hg)], axis=0)

    def q_pos(width):
        return c0 + lax.broadcasted_iota(jnp.int32, (tq, width), 0)

    def col(width):
        return lax.broadcasted_iota(jnp.int32, (tq, width), 1)

    s_c = _dot_nt(q4, kc_ref[0, 0])
    p_c = _masked_softmax_rows(
        s_c, col(n_cmp_pad) * NSA_CMP_STRIDE + (NSA_CMP_BLOCK - 1) <= q_pos(n_cmp_pad))
    o_c = _dot(p_c.astype(BF16), vc_ref[0, 0])

    p_sum = p_c[0:tq]
    for h in range(1, hg):
        p_sum = p_sum + p_c[h * tq:(h + 1) * tq]
    ps_hi, ps_lo = _split_bf16(p_sum)
    imp = _dot(ps_hi, ov_ref[...]) + _dot(ps_lo, ov_ref[...])
    blk = lax.broadcasted_iota(jnp.int32, (tq, LANES), 1)
    cur = (c0 + lax.broadcasted_iota(jnp.int32, (tq, LANES), 0)) // NSA_SEL_BLOCK
    forced = (blk == 0) | (blk == cur) | (blk == cur - 1)
    score = jnp.where(blk <= cur, imp + NSA_FORCE_BONUS * forced.astype(F32), -1.0)
    score = jnp.where(blk < n_sel, score, -jnp.inf)
    chosen = _top_k_mask(score, min(NSA_SEL_TOPK, n_sel))
    sel_bias = jnp.where(chosen, 0.0, NEG_INF).astype(BF16)
    q_aug = jnp.concatenate([q4, jnp.concatenate([sel_bias] * hg, axis=0)], axis=1)

    def keys(j):
        return ksaug_ref[pl.ds(pl.multiple_of(j * tk, tk), tk), :]

    def values(j):
        return vs_ref[0, pl.ds(pl.multiple_of(j * tk, tk), tk), :]

    jd = c0 // tk
    causal = (jd * tk + col(tk) <= q_pos(tk))[None]
    s = jnp.where(causal, _dot_nt(q_aug, keys(jd)).reshape(hg, tq, tk), NEG_INF).reshape(rows, tk)
    m0 = jnp.max(s, axis=1, keepdims=True)
    p = jnp.exp(s - m0)
    carry = (m0, jnp.sum(p, axis=1, keepdims=True), _dot(p.astype(BF16), values(jd)))
    _, l_s, o_s = _flash_loop(jd, q_aug, keys, values, carry, NSA_CHAINS)
    o_s = o_s * (1.0 / l_s)

    span = tq + NSA_WINDOW
    wstart = pl.multiple_of(jnp.maximum(c0 - NSA_WINDOW, 0), tq)
    w_pos = wstart + col(span)
    t_pos = q_pos(span)
    s_w = _dot_nt(q4, kw_ref[0, pl.ds(wstart, span), :])
    p_w = _masked_softmax_rows(s_w, (w_pos <= t_pos) & (t_pos - w_pos < NSA_WINDOW))
    o_w = _dot(p_w.astype(BF16), vw_ref[0, pl.ds(wstart, span), :])

    g_hi, g_lo = _split_bf16(1.0 / (1.0 + jnp.exp(-gate_ref[0])))
    g_wide = _dot(g_hi, gx_ref[0]) + _dot(g_lo, gx_ref[0])
    for h in range(hg):
        sl = slice(h * tq, (h + 1) * tq)
        out = jnp.zeros((tq, HEAD_DIM), F32)
        for branch, o_b in enumerate((o_c, o_s, o_w)):
            c = (branch * hg + h) * LANES
            out = out + g_wide[:, c:c + LANES] * o_b[sl]
        o_ref[0, :, h * LANES:(h + 1) * LANES] = out.astype(o_ref.dtype)


def _nsa_attention(main, gate_logits, kc_cmp, vc_cmp, overlap):
    b, s, _ = main.shape
    g, hg = NSA_KV_GROUPS, NSA_HEADS_PER_GROUP
    n_sel = s // NSA_SEL_BLOCK
    assert s % NSA_TK == 0 and n_sel <= LANES
    n16 = kc_cmp.shape[2]
    q_blocks = N_HEADS

    def kv_spec(which):
        return pl.BlockSpec((1, s, LANES), lambda bi, gi, i: (bi, 0, q_blocks + which * g + gi),
                            pipeline_mode=pl.Buffered(1))

    col = (np.arange(3)[:, None] * N_HEADS + np.arange(hg)[None, :]).reshape(-1)
    gx = np.zeros((g, LANES, 3 * hg * LANES), np.float32)
    for gi in range(g):
        for j, c in enumerate(col + gi * hg):
            gx[gi, c, j * LANES:(j + 1) * LANES] = 1.0
    gate_expand = jnp.asarray(gx, dtype=BF16)

    cmp_spec = pl.BlockSpec((1, 1, n16, HEAD_DIM), lambda bi, gi, i: (bi, gi, 0, 0))
    return pl.pallas_call(
        functools.partial(_nsa_kernel, n_sel=n_sel),
        grid=(b, g, s // NSA_TQ),
        in_specs=[
            pl.BlockSpec((1, NSA_TQ, hg * LANES), lambda bi, gi, i: (bi, i, gi)),
            cmp_spec, cmp_spec,
            kv_spec(2), kv_spec(3), kv_spec(4), kv_spec(5),
            pl.BlockSpec((1, NSA_TQ, LANES), lambda bi, gi, i: (bi, i, 0)),
            _const_spec(overlap.shape),
            pl.BlockSpec((1, LANES, 3 * hg * LANES), lambda bi, gi, i: (gi, 0, 0)),
        ],
        out_specs=pl.BlockSpec((1, NSA_TQ, hg * LANES), lambda bi, gi, i: (bi, i, gi)),
        out_shape=jax.ShapeDtypeStruct((b, s, N_HEADS * HEAD_DIM), BF16),
        scratch_shapes=[pltpu.VMEM((s, 2 * LANES), BF16)],
        compiler_params=_cparams("parallel", "parallel", "arbitrary"),
        name="nsa_attn",
    )(main, kc_cmp, vc_cmp, main, main, main, main, gate_logits, overlap, gate_expand)


def _nsa_overlap(seq):
    n_cmp = (seq - NSA_CMP_BLOCK) // NSA_CMP_STRIDE + 1
    n_sel = seq // NSA_SEL_BLOCK
    cmp_start = np.arange(seq // NSA_CMP_STRIDE) * NSA_CMP_STRIDE
    sel_start = np.arange(LANES) * NSA_SEL_BLOCK
    ov = ((cmp_start[:, None] < sel_start[None, :] + NSA_SEL_BLOCK)
          & (cmp_start[:, None] + NSA_CMP_BLOCK > sel_start[None, :]))
    ov &= (np.arange(seq // NSA_CMP_STRIDE)[:, None] < n_cmp) & (np.arange(LANES)[None, :] < n_sel)
    return jnp.asarray(ov, dtype=BF16)


def _glu_proj_kernel(x_ref, g_ref, w_ref, o_ref):
    h = _rms(x_ref[...], g_ref[...]).astype(BF16)
    d = o_ref.shape[1]
    for c in range(d // PROJ_CHUNK):
        a = _dot(h, w_ref[:, c * PROJ_CHUNK:(c + 1) * PROJ_CHUNK])
        gate = _dot(h, w_ref[:, d + c * PROJ_CHUNK:d + (c + 1) * PROJ_CHUNK])
        o_ref[:, c * PROJ_CHUNK:(c + 1) * PROJ_CHUNK] = (a / (1.0 + jnp.exp(-gate))).astype(o_ref.dtype)


def _glu_project(x2d, gain, w):
    t_rows, d = x2d.shape
    return pl.pallas_call(
        _glu_proj_kernel,
        grid=(t_rows // ROW_TILE,),
        in_specs=[pl.BlockSpec((ROW_TILE, d), lambda i: (i, 0)), _const_spec((1, d)), _const_spec(w.shape)],
        out_specs=pl.BlockSpec((ROW_TILE, d), lambda i: (i, 0)),
        out_shape=jax.ShapeDtypeStruct((t_rows, d), BF16),
        compiler_params=_cparams("parallel"),
        name="norm_glu_proj",
    )(x2d, gain.reshape(1, d), w)


CONV_TILE = 256
SUBLANES = 8
CONV_PHASE_ROWS = CONV_TILE + CONV_HALO - SUBLANES


def _conv_kernel(u_ref, halo_ref, dw_ref, db_ref, lg_ref, lb_ref, o_ref, ext_ref, phase_ref):
    i = pl.program_id(1)
    halo = halo_ref[0].astype(F32)
    ext_ref[0:CONV_HALO, :] = jnp.where(i == 0, 0.0, halo)
    ext_ref[CONV_HALO:, :] = u_ref[0].astype(F32)
    for b in range(1, SUBLANES):
        phase_ref[b - 1] = ext_ref[pl.ds(b, CONV_PHASE_ROWS), :]
    lead = CONV_HALO - (CONV_WIDTH - 1)
    for r in range(CONV_TILE // CONV_ROWS):
        acc = jnp.zeros((CONV_ROWS, D_MODEL), F32) + db_ref[...]
        for w in range(CONV_WIDTH):
            shift = (lead + w) % SUBLANES
            start = r * CONV_ROWS + lead + w - shift
            src = ext_ref if shift == 0 else phase_ref.at[shift - 1]
            acc = acc + dw_ref[w:w + 1, :] * src[pl.ds(start, CONV_ROWS), :]
        mu = jnp.mean(acc, axis=-1, keepdims=True)
        cen = acc - mu
        var = jnp.mean(cen * cen, axis=-1, keepdims=True)
        un = cen * lax.rsqrt(var + NORM_EPS) * lg_ref[...] + lb_ref[...]
        o_ref[0, r * CONV_ROWS:(r + 1) * CONV_ROWS, :] = (un / (1.0 + jnp.exp(-un))).astype(o_ref.dtype)


def _conv_ln_swish(u, dw_w, dw_b, ln_g, ln_b):
    b, s, d = u.shape
    halo_per_tile = CONV_TILE // CONV_HALO
    dw = jnp.concatenate([dw_w, jnp.zeros((1, d), F32)], axis=0)
    return pl.pallas_call(
        _conv_kernel,
        grid=(b, s // CONV_TILE),
        in_specs=[
            pl.BlockSpec((1, CONV_TILE, d), lambda bi, i: (bi, i, 0)),
            pl.BlockSpec((1, CONV_HALO, d), lambda bi, i: (bi, jnp.maximum(i * halo_per_tile - 1, 0), 0)),
            _const_spec(dw.shape), _const_spec((1, d)), _const_spec((1, d)), _const_spec((1, d)),
        ],
        out_specs=pl.BlockSpec((1, CONV_TILE, d), lambda bi, i: (bi, i, 0)),
        out_shape=jax.ShapeDtypeStruct((b, s, d), BF16),
        scratch_shapes=[pltpu.VMEM((CONV_TILE + CONV_HALO, d), F32),
                        pltpu.VMEM((SUBLANES - 1, CONV_PHASE_ROWS, d), F32)],
        compiler_params=_cparams("parallel", "parallel"),
        name="conv_ln_swish",
    )(u, u, dw, dw_b.reshape(1, d), ln_g.reshape(1, d), ln_b.reshape(1, d))


def _rope_tables(seq):
    half = HEAD_DIM // 2
    inv_freq = ROPE_THETA ** (-jnp.arange(half, dtype=F32) / half)
    ang = jnp.arange(seq, dtype=F32)[:, None] * inv_freq[None, :]
    cos, sin = jnp.cos(ang), jnp.sin(ang)
    return jnp.concatenate([cos, cos], axis=1), jnp.concatenate([-sin, sin], axis=1)


def _nsa_mixer(x2d, norm_gain, w_in, q_gain, k_gain, cmp_pos, w_cmp, tables, batch, seq):
    g, dh = NSA_KV_GROUPS, HEAD_DIM
    n_main = (N_HEADS + 6 * g) * dh
    scale = dh ** -0.5
    w_main = w_in[:, :n_main].astype(BF16)
    w_gate = jnp.pad(w_in[:, n_main:], ((0, 0), (0, LANES - 3 * N_HEADS))).astype(BF16)
    ones = jnp.ones((dh,), F32)
    head_gains = jnp.stack([q_gain * scale] * N_HEADS + [k_gain[0]] * g + [ones] * g
                           + [k_gain[1]] * g + [ones] * g + [k_gain[2]] * g + [ones] * g)
    rope_blocks = [True] * N_HEADS + [True] * g + [False] * g + [True] * g + [False] * g + [True] * g + [False] * g
    main, gate_logits = _project(x2d, norm_gain, w_main, *tables, head_gains, rope_blocks, seq, w_gate=w_gate)
    main = main.reshape(batch, seq, n_main)
    gate_logits = gate_logits.reshape(batch, seq, LANES)

    def cmp_layout(col0):
        t = main[:, :, col0:col0 + g * dh].reshape(batch, seq // NSA_CMP_STRIDE, NSA_CMP_STRIDE, g, dh)
        return t.transpose(0, 3, 1, 2, 4).reshape(batch, g, seq // NSA_CMP_STRIDE, CMP_HALF)

    kc_cmp, vc_cmp = _nsa_compress(cmp_layout(N_HEADS * dh), cmp_layout((N_HEADS + g) * dh), cmp_pos, w_cmp)
    out = _nsa_attention(main, gate_logits, kc_cmp, vc_cmp, _nsa_overlap(seq))
    return out.reshape(batch * seq, N_HEADS * dh)


def _qkv_weight_with_scaled_q(w_in):
    n_q = N_HEADS * HEAD_DIM
    return jnp.concatenate([w_in[:, :n_q] * HEAD_DIM ** -0.5, w_in[:, n_q:]], axis=1).astype(BF16)


def _sb_mixer(x2d, norm_gain, w_in, tables, batch, seq):
    n = 3 * N_HEADS * HEAD_DIM
    head_gains = jnp.ones((n // LANES, HEAD_DIM), F32)
    qkv = _project(x2d, norm_gain, _qkv_weight_with_scaled_q(w_in), *tables, head_gains,
                   [False] * (n // LANES), seq)
    return _sb_attention(qkv.reshape(batch, seq, n)).reshape(batch * seq, N_HEADS * HEAD_DIM)


def _conv_mixer(x2d, norm_gain, w_in, dw_w, dw_b, ln_g, ln_b, batch, seq):
    u = _glu_project(x2d, norm_gain, w_in.astype(BF16))
    a = _conv_ln_swish(u.reshape(batch, seq, D_MODEL), dw_w, dw_b, ln_g, ln_b)
    return a.reshape(batch * seq, D_MODEL)


def _moba_mixer(x2d, norm_gain, w_in, q_gain, k_gain, tables, batch, seq):
    n = 3 * N_HEADS * HEAD_DIM
    ones = jnp.ones((HEAD_DIM,), F32)
    head_gains = jnp.stack([q_gain * HEAD_DIM ** -0.5] * N_HEADS + [k_gain] * N_HEADS + [ones] * N_HEADS)
    rope_blocks = [True] * (2 * N_HEADS) + [False] * N_HEADS
    qkv = _project(x2d, norm_gain, w_in.astype(BF16), *tables, head_gains, rope_blocks, seq)
    return _moba_attention(qkv.reshape(batch, seq, n)).reshape(batch * seq, N_HEADS * HEAD_DIM)


def kernel(x, attn_norm, mlp_norm, mlp_w_up, mlp_w_down, nsa_w_in, nsa_q_norm, nsa_k_norm, nsa_cmp_pos, nsa_w_cmp, nsa_w_out, sb_w_in, sb_w_out, conv_w_in, conv_dw_w, conv_dw_b, conv_ln_g, conv_ln_b, conv_w_out, moba_w_in, moba_q_norm, moba_k_norm, moba_w_out):
    batch, seq, d = x.shape
    depth = attn_norm.shape[0]
    tables = _rope_tables(seq)
    x2d = x.reshape(batch * seq, d)
    for i in range(depth):
        m, j = i % 4, i // 4
        if m == 0:
            a = _nsa_mixer(x2d, attn_norm[i], nsa_w_in[j], nsa_q_norm[j], nsa_k_norm[j], nsa_cmp_pos[j],
                           nsa_w_cmp[j], tables, batch, seq)
            w_out = nsa_w_out[j]
        elif m == 1:
            a = _sb_mixer(x2d, attn_norm[i], sb_w_in[j], tables, batch, seq)
            w_out = sb_w_out[j]
        elif m == 2:
            a = _conv_mixer(x2d, attn_norm[i], conv_w_in[j], conv_dw_w[j], conv_dw_b[j], conv_ln_g[j],
                            conv_ln_b[j], batch, seq)
            w_out = conv_w_out[j]
        else:
            a = _moba_mixer(x2d, attn_norm[i], moba_w_in[j], moba_q_norm[j], moba_k_norm[j], tables, batch, seq)
            w_out = moba_w_out[j]
        x2d = _mixer_out_and_mlp(x2d, a, w_out.astype(BF16), mlp_norm[i], mlp_w_up[i].astype(BF16),
                                 mlp_w_down[i].astype(BF16))
    return x2d.reshape(batch, seq, d)
```

```python
import functools

import numpy as np
import jax
import jax.numpy as jnp
from jax import lax
from jax.experimental import pallas as pl
from jax.experimental.pallas import tpu as pltpu

F32 = jnp.float32
BF16 = jnp.bfloat16

D_MODEL = 1024
N_HEADS = 8
HEAD_DIM = 128
D_FF = 4 * D_MODEL
ROPE_THETA = 10000.0
NORM_EPS = 1e-6
NEG_INF = -1e30
LOG2_E = 1.4426950408889634

NSA_KV_GROUPS = 2
NSA_HEADS_PER_GROUP = N_HEADS // NSA_KV_GROUPS
NSA_CMP_BLOCK = 32
NSA_CMP_STRIDE = 16
NSA_SEL_BLOCK = 64
NSA_SEL_TOPK = 16
NSA_WINDOW = 512
NSA_FORCE_BONUS = 1000.0

CONV_WIDTH = 31
MOBA_BLOCK = 256
MOBA_TOPK = 3

LANES = 128
VMEM_LIMIT_BYTES = 56 * 1024 * 1024
ROW_TILE = 512
PROJ_CHUNK = 512
FF_CHUNK = 1024
CONV_HALO = 32
CONV_ROWS = 32


def _cparams(*sem):
    return pltpu.CompilerParams(dimension_semantics=sem, vmem_limit_bytes=VMEM_LIMIT_BYTES)


def _const_spec(shape):
    zeros = (0,) * len(shape)
    return pl.BlockSpec(shape, lambda *_: zeros, pipeline_mode=pl.Buffered(1))


def _rms(x, gain):
    return x * lax.rsqrt(jnp.mean(x * x, axis=-1, keepdims=True) + NORM_EPS) * gain


def _dot(a, b):
    return jnp.dot(a, b, preferred_element_type=F32)


def _dot_nt(a, b):
    return lax.dot_general(a, b, (((1,), (1,)), ((), ())), preferred_element_type=F32)


def _split_bf16(x):
    hi = x.astype(BF16)
    lo = (x - hi.astype(F32)).astype(BF16)
    return hi, lo


def _proj_kernel(*refs, rope_blocks, has_gate):
    if has_gate:
        x_ref, g_ref, w_ref, cos_ref, sin_ref, hg_ref, wg_ref, o_ref, og_ref = refs
    else:
        x_ref, g_ref, w_ref, cos_ref, sin_ref, hg_ref, o_ref = refs
    h = _rms(x_ref[...], g_ref[...]).astype(BF16)
    blocks_per_chunk = PROJ_CHUNK // LANES
    for c in range(len(rope_blocks) // blocks_per_chunk):
        y = _dot(h, w_ref[:, c * PROJ_CHUNK:(c + 1) * PROJ_CHUNK])
        chunk_flags = rope_blocks[c * blocks_per_chunk:(c + 1) * blocks_per_chunk]
        if not any(chunk_flags):
            o_ref[:, c * PROJ_CHUNK:(c + 1) * PROJ_CHUNK] = y.astype(o_ref.dtype)
            continue
        for k, flag in enumerate(chunk_flags):
            b = c * blocks_per_chunk + k
            yb = y[:, k * LANES:(k + 1) * LANES]
            if flag:
                yb = _rms(yb, hg_ref[b:b + 1, :])
                yb = yb * cos_ref[...] + pltpu.roll(yb, HEAD_DIM // 2, 1) * sin_ref[...]
            o_ref[:, b * LANES:(b + 1) * LANES] = yb.astype(o_ref.dtype)
    if has_gate:
        og_ref[...] = _dot(h, wg_ref[...])


def _project(x2d, gain, w, cos_t, sin_t, head_gains, rope_blocks, seq, w_gate=None):
    t_rows, d = x2d.shape
    n = w.shape[1]
    assert n % PROJ_CHUNK == 0 and len(rope_blocks) == n // LANES and seq % ROW_TILE == 0
    seq_tiles = seq // ROW_TILE
    has_gate = w_gate is not None
    in_specs = [
        pl.BlockSpec((ROW_TILE, d), lambda i: (i, 0)),
        _const_spec((1, d)),
        _const_spec((d, n)),
        pl.BlockSpec((ROW_TILE, LANES), lambda i: (i % seq_tiles, 0)),
        pl.BlockSpec((ROW_TILE, LANES), lambda i: (i % seq_tiles, 0)),
        _const_spec(head_gains.shape),
    ]
    args = [x2d, gain.reshape(1, d), w, cos_t, sin_t, head_gains]
    out_shape = [jax.ShapeDtypeStruct((t_rows, n), BF16)]
    out_specs = [pl.BlockSpec((ROW_TILE, n), lambda i: (i, 0))]
    if has_gate:
        in_specs.append(_const_spec(w_gate.shape))
        args.append(w_gate)
        out_shape.append(jax.ShapeDtypeStruct((t_rows, LANES), F32))
        out_specs.append(pl.BlockSpec((ROW_TILE, LANES), lambda i: (i, 0)))
    outs = pl.pallas_call(
        functools.partial(_proj_kernel, rope_blocks=tuple(rope_blocks), has_gate=has_gate),
        grid=(t_rows // ROW_TILE,),
        in_specs=in_specs,
        out_specs=out_specs,
        out_shape=out_shape,
        compiler_params=_cparams("parallel"),
        name="norm_proj",
    )(*args)
    return outs if has_gate else outs[0]


def _mlp_kernel(x_ref, a_ref, wo_ref, g_ref, wup_ref, wdn_ref, o_ref):
    x1 = x_ref[...] + _dot(a_ref[...], wo_ref[...])
    h = _rms(x1, g_ref[...]).astype(BF16)
    acc = x1
    for c in range(D_FF // FF_CHUNK):
        u = _dot(h, wup_ref[:, c * FF_CHUNK:(c + 1) * FF_CHUNK])
        act = jnp.square(jnp.maximum(u, 0.0)).astype(BF16)
        acc = acc + _dot(act, wdn_ref[c * FF_CHUNK:(c + 1) * FF_CHUNK, :])
    o_ref[...] = acc


def _mixer_out_and_mlp(x2d, a2d, w_out, gain, w_up, w_down):
    t_rows, d = x2d.shape
    return pl.pallas_call(
        _mlp_kernel,
        grid=(t_rows // ROW_TILE,),
        in_specs=[
            pl.BlockSpec((ROW_TILE, d), lambda i: (i, 0)),
            pl.BlockSpec((ROW_TILE, d), lambda i: (i, 0)),
            _const_spec((d, d)),
            _const_spec((1, d)),
            _const_spec((d, D_FF)),
            _const_spec((D_FF, d)),
        ],
        out_specs=pl.BlockSpec((ROW_TILE, d), lambda i: (i, 0)),
        out_shape=jax.ShapeDtypeStruct((t_rows, d), F32),
        compiler_params=_cparams("parallel"),
        name="outproj_mlp",
    )(x2d, a2d, w_out, gain.reshape(1, d), w_up, w_down)


SB_TILE = 256
SB_UNDERFLOW_LOG = -104.0


def _sb_kernel(q_ref, k_ref, v_ref, o_ref):
    i = pl.program_id(2)
    t = SB_TILE
    q = q_ref[0]
    row = lax.broadcasted_iota(jnp.int32, (t, t), 0)
    col = lax.broadcasted_iota(jnp.int32, (t, t), 1)
    below = row > col
    suffix_ones = below.astype(BF16)

    def block(j, carry, diagonal):
        o, later = carry
        start = pl.multiple_of(j * t, t)
        kj = k_ref[0, pl.ds(start, t), :]
        vj = v_ref[0, pl.ds(start, t), :]
        z = _dot_nt(q, kj)
        softplus = jnp.maximum(z, 0.0) + jnp.log(1.0 + jnp.exp(-jnp.abs(z)))
        log_om = -softplus
        if diagonal:
            log_om = jnp.where(below, log_om, 0.0)
        hi, lo = _split_bf16(log_om)
        between = _dot(hi, suffix_ones) + _dot(lo, suffix_ones) + later
        a = jnp.exp(z - softplus + between)
        if diagonal:
            a = jnp.where(below, a, 0.0)
        o = o + _dot(a.astype(BF16), vj)
        later = later + jnp.sum(log_om, axis=1, keepdims=True)
        return o, later

    carry = (jnp.zeros((t, HEAD_DIM), F32), jnp.zeros((t, 1), F32))
    o, later = block(i, carry, True)

    def more(state):
        j, _, later = state
        return (j >= 0) & (jnp.max(later) > SB_UNDERFLOW_LOG)

    def step(state):
        j, o, later = state
        o, later = block(j, (o, later), False)
        return j - 1, o, later

    _, o, _ = lax.while_loop(more, step, (i - 1, o, later))
    o_ref[0] = o.astype(o_ref.dtype)


def _sb_attention(qkv):
    b, s, _ = qkv.shape
    h = N_HEADS
    return pl.pallas_call(
        _sb_kernel,
        grid=(b, h, s // SB_TILE),
        in_specs=[
            pl.BlockSpec((1, SB_TILE, LANES), lambda bi, hi, i: (bi, i, hi)),
            pl.BlockSpec((1, s, LANES), lambda bi, hi, i: (bi, 0, h + hi)),
            pl.BlockSpec((1, s, LANES), lambda bi, hi, i: (bi, 0, 2 * h + hi)),
        ],
        out_specs=pl.BlockSpec((1, SB_TILE, LANES), lambda bi, hi, i: (bi, i, hi)),
        out_shape=jax.ShapeDtypeStruct((b, s, h * HEAD_DIM), BF16),
        compiler_params=_cparams("parallel", "parallel", "arbitrary"),
        name="stick_breaking_attn",
    )(qkv, qkv, qkv)


MOBA_TQ = 1024
MOBA_TK = 512
MOBA_CHAINS = 2
ONES_ROWS = 16


def _online_softmax_step_t(carry, s_t, v_t):
    m, acc = carry
    m_new = jnp.maximum(m, jnp.max(s_t, axis=0, keepdims=True))
    p_t = jnp.exp2(s_t - m_new)
    acc = jnp.exp2(m - m_new) * acc + _dot(v_t, p_t.astype(BF16))
    return m_new, acc


def _flash_loop_t(n_steps, q_aug, keys_fn, values_t_fn, carry, n_chains):
    r = q_aug.shape[0] // n_chains
    q_parts = [q_aug[c * r:(c + 1) * r] for c in range(n_chains)]
    state = tuple(tuple(x[:, c * r:(c + 1) * r] for x in carry) for c in range(n_chains))

    def body(n, state):
        k, v_t = keys_fn(n), values_t_fn(n)
        scores = [_dot_nt(k, qp) for qp in q_parts]
        return tuple(_online_softmax_step_t(st, s, v_t) for st, s in zip(state, scores))

    state = lax.fori_loop(0, n_steps, body, state)
    return tuple(jnp.concatenate([st[x] for st in state], axis=1) for x in range(2))


def _transposed_values_with_ones(v):
    v_t = v.astype(F32).T.astype(BF16)
    return jnp.concatenate([v_t, jnp.ones((ONES_ROWS, v.shape[0]), BF16)], axis=0)


def _moba_kernel(q_ref, k_ref, v_ref, o_ref, kaug_ref, kmean_ref, vt_ref, *, n_blk):
    i = pl.program_id(2)
    t, bs = MOBA_TQ, MOBA_BLOCK
    s_len = k_ref.shape[1]

    @pl.when(i == 0)
    def _():
        k = k_ref[0]
        kaug_ref[:, :LANES] = k
        blk = lax.broadcasted_iota(jnp.int32, (s_len, LANES), 0) // bs
        lane = lax.broadcasted_iota(jnp.int32, (s_len, LANES), 1)
        kaug_ref[:, LANES:] = (blk == lane).astype(BF16)
        kmean_ref[...] = jnp.mean(k.astype(F32).reshape(n_blk, bs, LANES), axis=1)
        for c in range(s_len // MOBA_TK):
            vt_ref[c] = _transposed_values_with_ones(v_ref[0, c * MOBA_TK:(c + 1) * MOBA_TK, :])

    q = q_ref[0]
    km_hi, km_lo = _split_bf16(kmean_ref[...])
    gate = _dot_nt(km_hi, q) + _dot_nt(km_lo, q)
    blk_id = lax.broadcasted_iota(jnp.int32, (n_blk, t), 0)
    cur = i * (t // bs) + lax.broadcasted_iota(jnp.int32, (n_blk, t), 1) // bs
    past = blk_id < cur
    gate = jnp.where(past, gate, -jnp.inf)
    rank = jnp.zeros((n_blk, t), jnp.int32)
    for m in range(n_blk):
        gm = gate[m:m + 1, :]
        beats = (gm > gate) | ((gm == gate) & (m < blk_id))
        rank = rank + beats.astype(jnp.int32)
    visible = ((rank < MOBA_TOPK) & past) | (blk_id == cur)
    sel_bias = jnp.where(visible, 0.0, NEG_INF)
    sel_bias = jnp.concatenate([sel_bias, jnp.zeros((LANES - n_blk, t), F32)], axis=0)
    q_aug = jnp.concatenate([q, sel_bias.T.astype(BF16)], axis=1)

    tk = MOBA_TK

    def keys(n):
        return kaug_ref[pl.ds(pl.multiple_of(n * tk, tk), tk), :]

    def values_t(n):
        return vt_ref[n]

    own = i * (t // tk)
    key = lax.broadcasted_iota(jnp.int32, (tk, t), 0)
    query = i * t + lax.broadcasted_iota(jnp.int32, (tk, t), 1)

    def causal_scores_t(n):
        return jnp.where(n * tk + key <= query, _dot_nt(keys(n), q_aug), NEG_INF)

    s_t = causal_scores_t(own)
    m0 = jnp.max(s_t, axis=0, keepdims=True)
    carry = (m0, _dot(values_t(own), jnp.exp2(s_t - m0).astype(BF16)))
    for extra in range(1, t // tk):
        carry = _online_softmax_step_t(carry, causal_scores_t(own + extra), values_t(own + extra))

    _, acc = _flash_loop_t(own, q_aug, keys, values_t, carry, MOBA_CHAINS)
    o_t = acc[:HEAD_DIM] * (1.0 / acc[HEAD_DIM:HEAD_DIM + 1])
    o_ref[0] = o_t.T.astype(o_ref.dtype)


def _moba_attention(qkv):
    b, s, _ = qkv.shape
    h = N_HEADS
    assert s % MOBA_TQ == 0 and MOBA_TQ % MOBA_BLOCK == 0
    n_blk = s // MOBA_BLOCK
    assert n_blk % 8 == 0 and n_blk <= LANES
    return pl.pallas_call(
        functools.partial(_moba_kernel, n_blk=n_blk),
        grid=(b, h, s // MOBA_TQ),
        in_specs=[
            pl.BlockSpec((1, MOBA_TQ, LANES), lambda bi, hi, i: (bi, i, hi)),
            pl.BlockSpec((1, s, LANES), lambda bi, hi, i: (bi, 0, h + hi)),
            pl.BlockSpec((1, s, LANES), lambda bi, hi, i: (bi, 0, 2 * h + hi)),
        ],
        out_specs=pl.BlockSpec((1, MOBA_TQ, LANES), lambda bi, hi, i: (bi, i, hi)),
        out_shape=jax.ShapeDtypeStruct((b, s, h * HEAD_DIM), BF16),
        scratch_shapes=[pltpu.VMEM((s, 2 * LANES), BF16), pltpu.VMEM((n_blk, LANES), F32),
                        pltpu.VMEM((s // MOBA_TK, HEAD_DIM + ONES_ROWS, MOBA_TK), BF16)],
        compiler_params=_cparams("parallel", "parallel", "arbitrary"),
        name="moba_attn",
    )(qkv, qkv, qkv)


NSA_TQ = 256
NSA_TK = 512
NSA_CHAINS = 2
CMP_HALF = NSA_CMP_STRIDE * HEAD_DIM


def _cmp_kernel(xk_ref, xv_ref, pos_ref, w_ref, ok_ref, ov_ref):
    n16 = xk_ref.shape[2]
    for t, (x_ref, o_ref) in enumerate(((xk_ref, ok_ref), (xv_ref, ov_ref))):
        x = x_ref[0, 0].astype(F32)
        first = _dot((x + pos_ref[t, 0:1, :]).astype(BF16), w_ref[t, 0])
        second = _dot((x + pos_ref[t, 1:2, :]).astype(BF16), w_ref[t, 1])
        o_ref[0, 0] = (first + pltpu.roll(second, n16 - 1, 0)).astype(o_ref.dtype)


def _nsa_compress(kc, vc, cmp_pos, w_cmp):
    b, g, n16, _ = kc.shape
    pos = cmp_pos.reshape(2, 2, CMP_HALF).astype(F32)
    w = w_cmp.reshape(2, 2, CMP_HALF, HEAD_DIM).astype(BF16)
    x_spec = pl.BlockSpec((1, 1, n16, CMP_HALF), lambda bi, gi: (bi, gi, 0, 0))
    o_spec = pl.BlockSpec((1, 1, n16, HEAD_DIM), lambda bi, gi: (bi, gi, 0, 0))
    return pl.pallas_call(
        _cmp_kernel,
        grid=(b, g),
        in_specs=[x_spec, x_spec, _const_spec(pos.shape), _const_spec(w.shape)],
        out_specs=[o_spec, o_spec],
        out_shape=[jax.ShapeDtypeStruct((b, g, n16, HEAD_DIM), BF16)] * 2,
        compiler_params=_cparams("parallel", "parallel"),
        name="nsa_compress",
    )(kc, vc, pos, w)


def _masked_softmax_rows(s, mask):
    tq, w = mask.shape
    s3 = s.reshape(s.shape[0] // tq, tq, w)
    z = jnp.where(mask[None], s3, NEG_INF)
    e = jnp.where(mask[None], jnp.exp2(z - jnp.max(z, axis=2, keepdims=True)), 0.0)
    l = jnp.sum(e, axis=2, keepdims=True)
    return (e * (1.0 / jnp.where(l > 0.0, l, 1.0))).reshape(s.shape)


def _top_k_mask(score, k):
    rows = score.shape[0]
    work = score
    taken = jnp.zeros((rows, 1), F32)
    level = jnp.full((rows, 1), jnp.inf, F32)
    above = jnp.zeros((rows, 1), F32)
    for _ in range(k):
        best = jnp.max(work, axis=1, keepdims=True)
        hit = work == best
        active = taken < k
        level = jnp.where(active, best, level)
        above = jnp.where(active, taken, above)
        taken = taken + jnp.sum(hit.astype(F32), axis=1, keepdims=True)
        work = jnp.where(hit, -jnp.inf, work)
    tie = score == level
    lower_lane = (lax.broadcasted_iota(jnp.int32, (LANES, LANES), 0)
                  < lax.broadcasted_iota(jnp.int32, (LANES, LANES), 1)).astype(BF16)
    ties_before = _dot(tie.astype(BF16), lower_lane)
    return (score > level) | (tie & (ties_before < k - above))


def _nsa_kernel(q_ref, kc_ref, vc_ref, ks_ref, vs_ref, kw_ref, vw_ref, gate_ref, ov_ref, gx_ref, o_ref,
                ksaug_ref, vst_ref, *, n_sel):
    i = pl.program_id(2)
    tq, tk, hg = NSA_TQ, NSA_TK, NSA_HEADS_PER_GROUP
    rows = hg * tq
    s_len = ks_ref.shape[1]
    n_cmp_pad = kc_ref.shape[2]
    c0 = i * tq

    @pl.when(i == 0)
    def _():
        ksaug_ref[:, :LANES] = ks_ref[0]
        blk = lax.broadcasted_iota(jnp.int32, (s_len, LANES), 0) // NSA_SEL_BLOCK
        lane = lax.broadcasted_iota(jnp.int32, (s_len, LANES), 1)
        ksaug_ref[:, LANES:] = (blk == lane).astype(BF16)
        for c in range(s_len // tk):
            vst_ref[c] = _transposed_values_with_ones(vs_ref[0, c * tk:(c + 1) * tk, :])

    q_all = q_ref[0]
    q4 = jnp.concatenate([q_all[:, h * LANES:(h + 1) * LANES] for h in range(hg)], axis=0)

    def q_pos(width):
        return c0 + lax.broadcasted_iota(jnp.int32, (tq, width), 0)

    def col(width):
        return lax.broadcasted_iota(jnp.int32, (tq, width), 1)

    s_c = _dot_nt(q4, kc_ref[0, 0])
    p_c = _masked_softmax_rows(
        s_c, col(n_cmp_pad) * NSA_CMP_STRIDE + (NSA_CMP_BLOCK - 1) <= q_pos(n_cmp_pad))
    o_c = _dot(p_c.astype(BF16), vc_ref[0, 0])

    p_sum = p_c[0:tq]
    for h in range(1, hg):
        p_sum = p_sum + p_c[h * tq:(h + 1) * tq]
    ps_hi, ps_lo = _split_bf16(p_sum)
    imp = _dot(ps_hi, ov_ref[...]) + _dot(ps_lo, ov_ref[...])
    blk = lax.broadcasted_iota(jnp.int32, (tq, LANES), 1)
    cur = (c0 + lax.broadcasted_iota(jnp.int32, (tq, LANES), 0)) // NSA_SEL_BLOCK
    forced = (blk == 0) | (blk == cur) | (blk == cur - 1)
    score = jnp.where(blk <= cur, imp + NSA_FORCE_BONUS * forced.astype(F32), -1.0)
    score = jnp.where(blk < n_sel, score, -jnp.inf)
    chosen = _top_k_mask(score, min(NSA_SEL_TOPK, n_sel))
    sel_bias = jnp.where(chosen, 0.0, NEG_INF).astype(BF16)
    q_aug = jnp.concatenate([q4, jnp.concatenate([sel_bias] * hg, axis=0)], axis=1)

    def keys(j):
        return ksaug_ref[pl.ds(pl.multiple_of(j * tk, tk), tk), :]

    def values_t(j):
        return vst_ref[j]

    jd = c0 // tk
    key_pos = jd * tk + lax.broadcasted_iota(jnp.int32, (tk, tq), 0)
    causal_t = key_pos <= c0 + lax.broadcasted_iota(jnp.int32, (tk, tq), 1)
    s_t = jnp.where(jnp.concatenate([causal_t] * hg, axis=1), _dot_nt(keys(jd), q_aug), NEG_INF)
    m0 = jnp.max(s_t, axis=0, keepdims=True)
    carry = (m0, _dot(values_t(jd), jnp.exp2(s_t - m0).astype(BF16)))
    _, acc = _flash_loop_t(jd, q_aug, keys, values_t, carry, NSA_CHAINS)
    o_s = (acc[:HEAD_DIM] * (1.0 / acc[HEAD_DIM:HEAD_DIM + 1])).T

    span = tq + NSA_WINDOW
    wstart = pl.multiple_of(jnp.maximum(c0 - NSA_WINDOW, 0), tq)
    w_pos = wstart + col(span)
    t_pos = q_pos(span)
    s_w = _dot_nt(q4, kw_ref[0, pl.ds(wstart, span), :])
    p_w = _masked_softmax_rows(s_w, (w_pos <= t_pos) & (t_pos - w_pos < NSA_WINDOW))
    o_w = _dot(p_w.astype(BF16), vw_ref[0, pl.ds(wstart, span), :])

    g_hi, g_lo = _split_bf16(1.0 / (1.0 + jnp.exp(-gate_ref[0])))
    g_wide = _dot(g_hi, gx_ref[0]) + _dot(g_lo, gx_ref[0])
    for h in range(hg):
        sl = slice(h * tq, (h + 1) * tq)
        out = jnp.zeros((tq, HEAD_DIM), F32)
        for branch, o_b in enumerate((o_c, o_s, o_w)):
            c = (branch * hg + h) * LANES
            out = out + g_wide[:, c:c + LANES] * o_b[sl]
        o_ref[0, :, h * LANES:(h + 1) * LANES] = out.astype(o_ref.dtype)


def _nsa_attention(main, gate_logits, kc_cmp, vc_cmp, overlap):
    b, s, _ = main.shape
    g, hg = NSA_KV_GROUPS, NSA_HEADS_PER_GROUP
    n_sel = s // NSA_SEL_BLOCK
    assert s % NSA_TK == 0 and n_sel <= LANES
    n16 = kc_cmp.shape[2]
    q_blocks = N_HEADS

    def kv_spec(which):
        return pl.BlockSpec((1, s, LANES), lambda bi, gi, i: (bi, 0, q_blocks + which * g + gi),
                            pipeline_mode=pl.Buffered(1))

    col = (np.arange(3)[:, None] * N_HEADS + np.arange(hg)[None, :]).reshape(-1)
    gx = np.zeros((g, LANES, 3 * hg * LANES), np.float32)
    for gi in range(g):
        for j, c in enumerate(col + gi * hg):
            gx[gi, c, j * LANES:(j + 1) * LANES] = 1.0
    gate_expand = jnp.asarray(gx, dtype=BF16)

    cmp_spec = pl.BlockSpec((1, 1, n16, HEAD_DIM), lambda bi, gi, i: (bi, gi, 0, 0))
    return pl.pallas_call(
        functools.partial(_nsa_kernel, n_sel=n_sel),
        grid=(b, g, s // NSA_TQ),
        in_specs=[
            pl.BlockSpec((1, NSA_TQ, hg * LANES), lambda bi, gi, i: (bi, i, gi)),
            cmp_spec, cmp_spec,
            kv_spec(2), kv_spec(3), kv_spec(4), kv_spec(5),
            pl.BlockSpec((1, NSA_TQ, LANES), lambda bi, gi, i: (bi, i, 0)),
            _const_spec(overlap.shape),
            pl.BlockSpec((1, LANES, 3 * hg * LANES), lambda bi, gi, i: (gi, 0, 0)),
        ],
        out_specs=pl.BlockSpec((1, NSA_TQ, hg * LANES), lambda bi, gi, i: (bi, i, gi)),
        out_shape=jax.ShapeDtypeStruct((b, s, N_HEADS * HEAD_DIM), BF16),
        scratch_shapes=[pltpu.VMEM((s, 2 * LANES), BF16),
                        pltpu.VMEM((s // NSA_TK, HEAD_DIM + ONES_ROWS, NSA_TK), BF16)],
        compiler_params=_cparams("parallel", "parallel", "arbitrary"),
        name="nsa_attn",
    )(main, kc_cmp, vc_cmp, main, main, main, main, gate_logits, overlap, gate_expand)


def _nsa_overlap(seq):
    n_cmp = (seq - NSA_CMP_BLOCK) // NSA_CMP_STRIDE + 1
    n_sel = seq // NSA_SEL_BLOCK
    cmp_start = np.arange(seq // NSA_CMP_STRIDE) * NSA_CMP_STRIDE
    sel_start = np.arange(LANES) * NSA_SEL_BLOCK
    ov = ((cmp_start[:, None] < sel_start[None, :] + NSA_SEL_BLOCK)
          & (cmp_start[:, None] + NSA_CMP_BLOCK > sel_start[None, :]))
    ov &= (np.arange(seq // NSA_CMP_STRIDE)[:, None] < n_cmp) & (np.arange(LANES)[None, :] < n_sel)
    return jnp.asarray(ov, dtype=BF16)


def _glu_proj_kernel(x_ref, g_ref, w_ref, o_ref):
    h = _rms(x_ref[...], g_ref[...]).astype(BF16)
    d = o_ref.shape[1]
    for c in range(d // PROJ_CHUNK):
        a = _dot(h, w_ref[:, c * PROJ_CHUNK:(c + 1) * PROJ_CHUNK])
        gate = _dot(h, w_ref[:, d + c * PROJ_CHUNK:d + (c + 1) * PROJ_CHUNK])
        o_ref[:, c * PROJ_CHUNK:(c + 1) * PROJ_CHUNK] = (a / (1.0 + jnp.exp(-gate))).astype(o_ref.dtype)


def _glu_project(x2d, gain, w):
    t_rows, d = x2d.shape
    return pl.pallas_call(
        _glu_proj_kernel,
        grid=(t_rows // ROW_TILE,),
        in_specs=[pl.BlockSpec((ROW_TILE, d), lambda i: (i, 0)), _const_spec((1, d)), _const_spec(w.shape)],
        out_specs=pl.BlockSpec((ROW_TILE, d), lambda i: (i, 0)),
        out_shape=jax.ShapeDtypeStruct((t_rows, d), BF16),
        compiler_params=_cparams("parallel"),
        name="norm_glu_proj",
    )(x2d, gain.reshape(1, d), w)


CONV_TILE = 256
SUBLANES = 8
CONV_PHASE_ROWS = CONV_TILE + CONV_HALO - SUBLANES


def _conv_kernel(u_ref, halo_ref, dw_ref, db_ref, lg_ref, lb_ref, o_ref, ext_ref, phase_ref):
    i = pl.program_id(1)
    halo = halo_ref[0].astype(F32)
    ext_ref[0:CONV_HALO, :] = jnp.where(i == 0, 0.0, halo)
    ext_ref[CONV_HALO:, :] = u_ref[0].astype(F32)
    for b in range(1, SUBLANES):
        phase_ref[b - 1] = ext_ref[pl.ds(b, CONV_PHASE_ROWS), :]
    lead = CONV_HALO - (CONV_WIDTH - 1)
    for r in range(CONV_TILE // CONV_ROWS):
        acc = jnp.zeros((CONV_ROWS, D_MODEL), F32) + db_ref[...]
        for w in range(CONV_WIDTH):
            shift = (lead + w) % SUBLANES
            start = r * CONV_ROWS + lead + w - shift
            src = ext_ref if shift == 0 else phase_ref.at[shift - 1]
            acc = acc + dw_ref[w:w + 1, :] * src[pl.ds(start, CONV_ROWS), :]
        mu = jnp.mean(acc, axis=-1, keepdims=True)
        cen = acc - mu
        var = jnp.mean(cen * cen, axis=-1, keepdims=True)
        un = cen * lax.rsqrt(var + NORM_EPS) * lg_ref[...] + lb_ref[...]
        o_ref[0, r * CONV_ROWS:(r + 1) * CONV_ROWS, :] = (un / (1.0 + jnp.exp(-un))).astype(o_ref.dtype)


def _conv_ln_swish(u, dw_w, dw_b, ln_g, ln_b):
    b, s, d = u.shape
    halo_per_tile = CONV_TILE // CONV_HALO
    dw = jnp.concatenate([dw_w, jnp.zeros((1, d), F32)], axis=0)
    return pl.pallas_call(
        _conv_kernel,
        grid=(b, s // CONV_TILE),
        in_specs=[
            pl.BlockSpec((1, CONV_TILE, d), lambda bi, i: (bi, i, 0)),
            pl.BlockSpec((1, CONV_HALO, d), lambda bi, i: (bi, jnp.maximum(i * halo_per_tile - 1, 0), 0)),
            _const_spec(dw.shape), _const_spec((1, d)), _const_spec((1, d)), _const_spec((1, d)),
        ],
        out_specs=pl.BlockSpec((1, CONV_TILE, d), lambda bi, i: (bi, i, 0)),
        out_shape=jax.ShapeDtypeStruct((b, s, d), BF16),
        scratch_shapes=[pltpu.VMEM((CONV_TILE + CONV_HALO, d), F32),
                        pltpu.VMEM((SUBLANES - 1, CONV_PHASE_ROWS, d), F32)],
        compiler_params=_cparams("parallel", "parallel"),
        name="conv_ln_swish",
    )(u, u, dw, dw_b.reshape(1, d), ln_g.reshape(1, d), ln_b.reshape(1, d))


def _rope_tables(seq):
    half = HEAD_DIM // 2
    inv_freq = ROPE_THETA ** (-jnp.arange(half, dtype=F32) / half)
    ang = jnp.arange(seq, dtype=F32)[:, None] * inv_freq[None, :]
    cos, sin = jnp.cos(ang), jnp.sin(ang)
    return jnp.concatenate([cos, cos], axis=1), jnp.concatenate([-sin, sin], axis=1)


def _nsa_mixer(x2d, norm_gain, w_in, q_gain, k_gain, cmp_pos, w_cmp, tables, batch, seq):
    g, dh = NSA_KV_GROUPS, HEAD_DIM
    n_main = (N_HEADS + 6 * g) * dh
    scale = dh ** -0.5 * LOG2_E
    w_main = w_in[:, :n_main].astype(BF16)
    w_gate = jnp.pad(w_in[:, n_main:], ((0, 0), (0, LANES - 3 * N_HEADS))).astype(BF16)
    ones = jnp.ones((dh,), F32)
    head_gains = jnp.stack([q_gain * scale] * N_HEADS + [k_gain[0]] * g + [ones] * g
                           + [k_gain[1]] * g + [ones] * g + [k_gain[2]] * g + [ones] * g)
    rope_blocks = [True] * N_HEADS + [True] * g + [False] * g + [True] * g + [False] * g + [True] * g + [False] * g
    main, gate_logits = _project(x2d, norm_gain, w_main, *tables, head_gains, rope_blocks, seq, w_gate=w_gate)
    main = main.reshape(batch, seq, n_main)
    gate_logits = gate_logits.reshape(batch, seq, LANES)

    def cmp_layout(col0):
        t = main[:, :, col0:col0 + g * dh].reshape(batch, seq // NSA_CMP_STRIDE, NSA_CMP_STRIDE, g, dh)
        return t.transpose(0, 3, 1, 2, 4).reshape(batch, g, seq // NSA_CMP_STRIDE, CMP_HALF)

    kc_cmp, vc_cmp = _nsa_compress(cmp_layout(N_HEADS * dh), cmp_layout((N_HEADS + g) * dh), cmp_pos, w_cmp)
    out = _nsa_attention(main, gate_logits, kc_cmp, vc_cmp, _nsa_overlap(seq))
    return out.reshape(batch * seq, N_HEADS * dh)


def _qkv_weight_with_scaled_q(w_in):
    n_q = N_HEADS * HEAD_DIM
    return jnp.concatenate([w_in[:, :n_q] * HEAD_DIM ** -0.5, w_in[:, n_q:]], axis=1).astype(BF16)


def _sb_mixer(x2d, norm_gain, w_in, tables, batch, seq):
    n = 3 * N_HEADS * HEAD_DIM
    head_gains = jnp.ones((n // LANES, HEAD_DIM), F32)
    qkv = _project(x2d, norm_gain, _qkv_weight_with_scaled_q(w_in), *tables, head_gains,
                   [False] * (n // LANES), seq)
    return _sb_attention(qkv.reshape(batch, seq, n)).reshape(batch * seq, N_HEADS * HEAD_DIM)


def _conv_mixer(x2d, norm_gain, w_in, dw_w, dw_b, ln_g, ln_b, batch, seq):
    u = _glu_project(x2d, norm_gain, w_in.astype(BF16))
    a = _conv_ln_swish(u.reshape(batch, seq, D_MODEL), dw_w, dw_b, ln_g, ln_b)
    return a.reshape(batch * seq, D_MODEL)


def _moba_mixer(x2d, norm_gain, w_in, q_gain, k_gain, tables, batch, seq):
    n = 3 * N_HEADS * HEAD_DIM
    ones = jnp.ones((HEAD_DIM,), F32)
    head_gains = jnp.stack([q_gain * (HEAD_DIM ** -0.5 * LOG2_E)] * N_HEADS + [k_gain] * N_HEADS + [ones] * N_HEADS)
    rope_blocks = [True] * (2 * N_HEADS) + [False] * N_HEADS
    qkv = _project(x2d, norm_gain, w_in.astype(BF16), *tables, head_gains, rope_blocks, seq)
    return _moba_attention(qkv.reshape(batch, seq, n)).reshape(batch * seq, N_HEADS * HEAD_DIM)


def kernel(x, attn_norm, mlp_norm, mlp_w_up, mlp_w_down, nsa_w_in, nsa_q_norm, nsa_k_norm, nsa_cmp_pos, nsa_w_cmp, nsa_w_out, sb_w_in, sb_w_out, conv_w_in, conv_dw_w, conv_dw_b, conv_ln_g, conv_ln_b, conv_w_out, moba_w_in, moba_q_norm, moba_k_norm, moba_w_out):
    batch, seq, d = x.shape
    depth = attn_norm.shape[0]
    tables = _rope_tables(seq)
    x2d = x.reshape(batch * seq, d)
    for i in range(depth):
        m, j = i % 4, i // 4
        if m == 0:
            a = _nsa_mixer(x2d, attn_norm[i], nsa_w_in[j], nsa_q_norm[j], nsa_k_norm[j], nsa_cmp_pos[j],
                           nsa_w_cmp[j], tables, batch, seq)
            w_out = nsa_w_out[j]
        elif m == 1:
            a = _sb_mixer(x2d, attn_norm[i], sb_w_in[j], tables, batch, seq)
            w_out = sb_w_out[j]
        elif m == 2:
            a = _conv_mixer(x2d, attn_norm[i], conv_w_in[j], conv_dw_w[j], conv_dw_b[j], conv_ln_g[j],
                            conv_ln_b[j], batch, seq)
            w_out = conv_w_out[j]
        else:
            a = _moba_mixer(x2d, attn_norm[i], moba_w_in[j], moba_q_norm[j], moba_k_norm[j], tables, batch, seq)
            w_out = moba_w_out[j]
        x2d = _mixer_out_and_mlp(x2d, a, w_out.astype(BF16), mlp_norm[i], mlp_w_up[i].astype(BF16),
                                 mlp_w_down[i].astype(BF16))
    return x2d.reshape(batch, seq, d)
```

```python
import functools

import numpy as np
import jax
import jax.numpy as jnp
from jax import lax
from jax.experimental import pallas as pl
from jax.experimental.pallas import tpu as pltpu

F32 = jnp.float32
BF16 = jnp.bfloat16

D_MODEL = 1024
N_HEADS = 8
HEAD_DIM = 128
D_FF = 4 * D_MODEL
ROPE_THETA = 10000.0
NORM_EPS = 1e-6
NEG_INF = -1e30
LOG2_E = 1.4426950408889634

NSA_KV_GROUPS = 2
NSA_HEADS_PER_GROUP = N_HEADS // NSA_KV_GROUPS
NSA_CMP_BLOCK = 32
NSA_CMP_STRIDE = 16
NSA_SEL_BLOCK = 64
NSA_SEL_TOPK = 16
NSA_WINDOW = 512
NSA_FORCE_BONUS = 1000.0

CONV_WIDTH = 31
MOBA_BLOCK = 256
MOBA_TOPK = 3

LANES = 128
VMEM_LIMIT_BYTES = 56 * 1024 * 1024
ROW_TILE = 512
PROJ_CHUNK = 512
FF_CHUNK = 1024
CONV_HALO = 32
CONV_ROWS = 32


def _cparams(*sem):
    return pltpu.CompilerParams(dimension_semantics=sem, vmem_limit_bytes=VMEM_LIMIT_BYTES)


def _const_spec(shape):
    zeros = (0,) * len(shape)
    return pl.BlockSpec(shape, lambda *_: zeros, pipeline_mode=pl.Buffered(1))


def _rms(x, gain):
    return x * lax.rsqrt(jnp.mean(x * x, axis=-1, keepdims=True) + NORM_EPS) * gain


def _dot(a, b):
    return jnp.dot(a, b, preferred_element_type=F32)


def _dot_nt(a, b):
    return lax.dot_general(a, b, (((1,), (1,)), ((), ())), preferred_element_type=F32)


def _split_bf16(x):
    hi = x.astype(BF16)
    lo = (x - hi.astype(F32)).astype(BF16)
    return hi, lo


def _proj_kernel(*refs, rope_blocks, has_gate):
    if has_gate:
        x_ref, g_ref, w_ref, cos_ref, sin_ref, hg_ref, wg_ref, o_ref, og_ref = refs
    else:
        x_ref, g_ref, w_ref, cos_ref, sin_ref, hg_ref, o_ref = refs
    h = _rms(x_ref[...], g_ref[...]).astype(BF16)
    blocks_per_chunk = PROJ_CHUNK // LANES
    for c in range(len(rope_blocks) // blocks_per_chunk):
        y = _dot(h, w_ref[:, c * PROJ_CHUNK:(c + 1) * PROJ_CHUNK])
        chunk_flags = rope_blocks[c * blocks_per_chunk:(c + 1) * blocks_per_chunk]
        if not any(chunk_flags):
            o_ref[:, c * PROJ_CHUNK:(c + 1) * PROJ_CHUNK] = y.astype(o_ref.dtype)
            continue
        for k, flag in enumerate(chunk_flags):
            b = c * blocks_per_chunk + k
            yb = y[:, k * LANES:(k + 1) * LANES]
            if flag:
                yb = _rms(yb, hg_ref[b:b + 1, :])
                yb = yb * cos_ref[...] + pltpu.roll(yb, HEAD_DIM // 2, 1) * sin_ref[...]
            o_ref[:, b * LANES:(b + 1) * LANES] = yb.astype(o_ref.dtype)
    if has_gate:
        og_ref[...] = _dot(h, wg_ref[...])


def _project(x2d, gain, w, cos_t, sin_t, head_gains, rope_blocks, seq, w_gate=None):
    t_rows, d = x2d.shape
    n = w.shape[1]
    assert n % PROJ_CHUNK == 0 and len(rope_blocks) == n // LANES and seq % ROW_TILE == 0
    seq_tiles = seq // ROW_TILE
    has_gate = w_gate is not None
    in_specs = [
        pl.BlockSpec((ROW_TILE, d), lambda i: (i, 0)),
        _const_spec((1, d)),
        _const_spec((d, n)),
        pl.BlockSpec((ROW_TILE, LANES), lambda i: (i % seq_tiles, 0)),
        pl.BlockSpec((ROW_TILE, LANES), lambda i: (i % seq_tiles, 0)),
        _const_spec(head_gains.shape),
    ]
    args = [x2d, gain.reshape(1, d), w, cos_t, sin_t, head_gains]
    out_shape = [jax.ShapeDtypeStruct((t_rows, n), BF16)]
    out_specs = [pl.BlockSpec((ROW_TILE, n), lambda i: (i, 0))]
    if has_gate:
        in_specs.append(_const_spec(w_gate.shape))
        args.append(w_gate)
        out_shape.append(jax.ShapeDtypeStruct((t_rows, w_gate.shape[1]), F32))
        out_specs.append(pl.BlockSpec((ROW_TILE, w_gate.shape[1]), lambda i: (i, 0)))
    outs = pl.pallas_call(
        functools.partial(_proj_kernel, rope_blocks=tuple(rope_blocks), has_gate=has_gate),
        grid=(t_rows // ROW_TILE,),
        in_specs=in_specs,
        out_specs=out_specs,
        out_shape=out_shape,
        compiler_params=_cparams("parallel"),
        name="norm_proj",
    )(*args)
    return outs if has_gate else outs[0]


def _mlp_kernel(x_ref, a_ref, wo_ref, g_ref, wup_ref, wdn_ref, o_ref):
    x1 = x_ref[...] + _dot(a_ref[...], wo_ref[...])
    h = _rms(x1, g_ref[...]).astype(BF16)
    acc = x1
    for c in range(D_FF // FF_CHUNK):
        u = _dot(h, wup_ref[:, c * FF_CHUNK:(c + 1) * FF_CHUNK])
        act = jnp.square(jnp.maximum(u, 0.0)).astype(BF16)
        acc = acc + _dot(act, wdn_ref[c * FF_CHUNK:(c + 1) * FF_CHUNK, :])
    o_ref[...] = acc


def _mixer_out_and_mlp(x2d, a2d, w_out, gain, w_up, w_down):
    t_rows, d = x2d.shape
    return pl.pallas_call(
        _mlp_kernel,
        grid=(t_rows // ROW_TILE,),
        in_specs=[
            pl.BlockSpec((ROW_TILE, d), lambda i: (i, 0)),
            pl.BlockSpec((ROW_TILE, d), lambda i: (i, 0)),
            _const_spec((d, d)),
            _const_spec((1, d)),
            _const_spec((d, D_FF)),
            _const_spec((D_FF, d)),
        ],
        out_specs=pl.BlockSpec((ROW_TILE, d), lambda i: (i, 0)),
        out_shape=jax.ShapeDtypeStruct((t_rows, d), F32),
        compiler_params=_cparams("parallel"),
        name="outproj_mlp",
    )(x2d, a2d, w_out, gain.reshape(1, d), w_up, w_down)


SB_TILE = 256
SB_UNDERFLOW_LOG = -104.0


def _sb_kernel(q_ref, k_ref, v_ref, o_ref):
    i = pl.program_id(2)
    t = SB_TILE
    q = q_ref[0]
    row = lax.broadcasted_iota(jnp.int32, (t, t), 0)
    col = lax.broadcasted_iota(jnp.int32, (t, t), 1)
    below = row > col
    suffix_ones = below.astype(BF16)

    def block(j, carry, diagonal):
        o, later = carry
        start = pl.multiple_of(j * t, t)
        kj = k_ref[0, pl.ds(start, t), :]
        vj = v_ref[0, pl.ds(start, t), :]
        z = _dot_nt(q, kj)
        softplus = jnp.maximum(z, 0.0) + jnp.log(1.0 + jnp.exp(-jnp.abs(z)))
        log_om = -softplus
        if diagonal:
            log_om = jnp.where(below, log_om, 0.0)
        hi, lo = _split_bf16(log_om)
        between = _dot(hi, suffix_ones) + _dot(lo, suffix_ones) + later
        a = jnp.exp(z - softplus + between)
        if diagonal:
            a = jnp.where(below, a, 0.0)
        o = o + _dot(a.astype(BF16), vj)
        later = later + jnp.sum(log_om, axis=1, keepdims=True)
        return o, later

    carry = (jnp.zeros((t, HEAD_DIM), F32), jnp.zeros((t, 1), F32))
    o, later = block(i, carry, True)

    def more(state):
        j, _, later = state
        return (j >= 0) & (jnp.max(later) > SB_UNDERFLOW_LOG)

    def step(state):
        j, o, later = state
        o, later = block(j, (o, later), False)
        return j - 1, o, later

    _, o, _ = lax.while_loop(more, step, (i - 1, o, later))
    o_ref[0] = o.astype(o_ref.dtype)


def _sb_attention(qkv):
    b, s, _ = qkv.shape
    h = N_HEADS
    return pl.pallas_call(
        _sb_kernel,
        grid=(b, h, s // SB_TILE),
        in_specs=[
            pl.BlockSpec((1, SB_TILE, LANES), lambda bi, hi, i: (bi, i, hi)),
            pl.BlockSpec((1, s, LANES), lambda bi, hi, i: (bi, 0, h + hi)),
            pl.BlockSpec((1, s, LANES), lambda bi, hi, i: (bi, 0, 2 * h + hi)),
        ],
        out_specs=pl.BlockSpec((1, SB_TILE, LANES), lambda bi, hi, i: (bi, i, hi)),
        out_shape=jax.ShapeDtypeStruct((b, s, h * HEAD_DIM), BF16),
        compiler_params=_cparams("parallel", "parallel", "arbitrary"),
        name="stick_breaking_attn",
    )(qkv, qkv, qkv)


MOBA_TQ = 1024
MOBA_TK = 1024
ONES_ROWS = 16


def _online_softmax_step_t(carry, s_t, v_t):
    m, acc = carry
    m_new = jnp.maximum(m, jnp.max(s_t, axis=0, keepdims=True))
    p_t = jnp.exp2(s_t - m_new)
    acc = jnp.exp2(m - m_new) * acc + _dot(v_t, p_t.astype(BF16))
    return m_new, acc


def _flash_loop_t(n_steps, q_aug, keys_fn, values_t_fn, carry):
    def body(n, carry):
        return _online_softmax_step_t(carry, _dot_nt(keys_fn(n), q_aug), values_t_fn(n))

    return lax.fori_loop(0, n_steps, body, carry)


def _transposed_values_with_ones(v):
    v_t = v.astype(F32).T.astype(BF16)
    return jnp.concatenate([v_t, jnp.ones((ONES_ROWS, v.shape[0]), BF16)], axis=0)


def _moba_kernel(q_ref, k_ref, v_ref, o_ref, kaug_ref, kmean_ref, vt_ref, *, n_blk):
    i = pl.program_id(2)
    t, bs = MOBA_TQ, MOBA_BLOCK
    s_len = k_ref.shape[1]

    @pl.when(i == 0)
    def _():
        k = k_ref[0]
        kaug_ref[:, :LANES] = k
        blk = lax.broadcasted_iota(jnp.int32, (s_len, LANES), 0) // bs
        lane = lax.broadcasted_iota(jnp.int32, (s_len, LANES), 1)
        kaug_ref[:, LANES:] = (blk == lane).astype(BF16)
        kmean_ref[...] = jnp.mean(k.astype(F32).reshape(n_blk, bs, LANES), axis=1)
        for c in range(s_len // MOBA_TK):
            vt_ref[c] = _transposed_values_with_ones(v_ref[0, c * MOBA_TK:(c + 1) * MOBA_TK, :])

    q = q_ref[0]
    km_hi, km_lo = _split_bf16(kmean_ref[...])
    gate = _dot_nt(km_hi, q) + _dot_nt(km_lo, q)
    blk_id = lax.broadcasted_iota(jnp.int32, (n_blk, t), 0)
    cur = i * (t // bs) + lax.broadcasted_iota(jnp.int32, (n_blk, t), 1) // bs
    past = blk_id < cur
    gate = jnp.where(past, gate, -jnp.inf)
    rank = jnp.zeros((n_blk, t), jnp.int32)
    for m in range(n_blk):
        gm = gate[m:m + 1, :]
        beats = (gm > gate) | ((gm == gate) & (m < blk_id))
        rank = rank + beats.astype(jnp.int32)
    visible = ((rank < MOBA_TOPK) & past) | (blk_id == cur)
    sel_bias = jnp.where(visible, 0.0, NEG_INF)
    sel_bias = jnp.concatenate([sel_bias, jnp.zeros((LANES - n_blk, t), F32)], axis=0)
    q_aug = jnp.concatenate([q, sel_bias.T.astype(BF16)], axis=1)

    tk = MOBA_TK

    def keys(n):
        return kaug_ref[pl.ds(pl.multiple_of(n * tk, tk), tk), :]

    def values_t(n):
        return vt_ref[n]

    own = i * (t // tk)
    key = lax.broadcasted_iota(jnp.int32, (tk, t), 0)
    query = i * t + lax.broadcasted_iota(jnp.int32, (tk, t), 1)

    def causal_scores_t(n):
        return jnp.where(n * tk + key <= query, _dot_nt(keys(n), q_aug), NEG_INF)

    s_t = causal_scores_t(own)
    m0 = jnp.max(s_t, axis=0, keepdims=True)
    carry = (m0, _dot(values_t(own), jnp.exp2(s_t - m0).astype(BF16)))
    for extra in range(1, t // tk):
        carry = _online_softmax_step_t(carry, causal_scores_t(own + extra), values_t(own + extra))

    _, acc = _flash_loop_t(own, q_aug, keys, values_t, carry)
    o_t = acc[:HEAD_DIM] * (1.0 / acc[HEAD_DIM:HEAD_DIM + 1])
    o_ref[0] = o_t.T.astype(o_ref.dtype)


def _moba_attention(qkv):
    b, s, _ = qkv.shape
    h = N_HEADS
    assert s % MOBA_TQ == 0 and MOBA_TQ % MOBA_BLOCK == 0
    n_blk = s // MOBA_BLOCK
    assert n_blk % 8 == 0 and n_blk <= LANES
    return pl.pallas_call(
        functools.partial(_moba_kernel, n_blk=n_blk),
        grid=(b, h, s // MOBA_TQ),
        in_specs=[
            pl.BlockSpec((1, MOBA_TQ, LANES), lambda bi, hi, i: (bi, i, hi)),
            pl.BlockSpec((1, s, LANES), lambda bi, hi, i: (bi, 0, h + hi)),
            pl.BlockSpec((1, s, LANES), lambda bi, hi, i: (bi, 0, 2 * h + hi)),
        ],
        out_specs=pl.BlockSpec((1, MOBA_TQ, LANES), lambda bi, hi, i: (bi, i, hi)),
        out_shape=jax.ShapeDtypeStruct((b, s, h * HEAD_DIM), BF16),
        scratch_shapes=[pltpu.VMEM((s, 2 * LANES), BF16), pltpu.VMEM((n_blk, LANES), F32),
                        pltpu.VMEM((s // MOBA_TK, HEAD_DIM + ONES_ROWS, MOBA_TK), BF16)],
        compiler_params=_cparams("parallel", "parallel", "arbitrary"),
        name="moba_attn",
    )(qkv, qkv, qkv)


NSA_TQ = 256
NSA_TK = 1024
CMP_HALF = NSA_CMP_STRIDE * HEAD_DIM


def _cmp_kernel(xk_ref, xv_ref, pos_ref, w_ref, ok_ref, ov_ref):
    n16 = xk_ref.shape[2]
    for t, (x_ref, o_ref) in enumerate(((xk_ref, ok_ref), (xv_ref, ov_ref))):
        x = x_ref[0, 0].astype(F32)
        first = _dot((x + pos_ref[t, 0:1, :]).astype(BF16), w_ref[t, 0])
        second = _dot((x + pos_ref[t, 1:2, :]).astype(BF16), w_ref[t, 1])
        out = first + pltpu.roll(second, n16 - 1, 0)
        o_ref[0, 0] = (out if t == 0 else out.T).astype(o_ref.dtype)


def _nsa_compress(kc, vc, cmp_pos, w_cmp):
    b, g, n16, _ = kc.shape
    pos = cmp_pos.reshape(2, 2, CMP_HALF).astype(F32)
    w = w_cmp.reshape(2, 2, CMP_HALF, HEAD_DIM).astype(BF16)
    x_spec = pl.BlockSpec((1, 1, n16, CMP_HALF), lambda bi, gi: (bi, gi, 0, 0))
    return pl.pallas_call(
        _cmp_kernel,
        grid=(b, g),
        in_specs=[x_spec, x_spec, _const_spec(pos.shape), _const_spec(w.shape)],
        out_specs=[pl.BlockSpec((1, 1, n16, HEAD_DIM), lambda bi, gi: (bi, gi, 0, 0)),
                   pl.BlockSpec((1, 1, HEAD_DIM, n16), lambda bi, gi: (bi, gi, 0, 0))],
        out_shape=[jax.ShapeDtypeStruct((b, g, n16, HEAD_DIM), BF16),
                   jax.ShapeDtypeStruct((b, g, HEAD_DIM, n16), BF16)],
        compiler_params=_cparams("parallel", "parallel"),
        name="nsa_compress",
    )(kc, vc, pos, w)


def _top_k_mask_t(score_t, k):
    n, q = score_t.shape
    work = score_t
    taken = jnp.zeros((1, q), F32)
    level = jnp.full((1, q), jnp.inf, F32)
    above = jnp.zeros((1, q), F32)
    for _ in range(k):
        best = jnp.max(work, axis=0, keepdims=True)
        hit = work == best
        active = taken < k
        level = jnp.where(active, best, level)
        above = jnp.where(active, taken, above)
        taken = taken + jnp.sum(hit.astype(F32), axis=0, keepdims=True)
        work = jnp.where(hit, -jnp.inf, work)
    tie = score_t == level
    lower = (lax.broadcasted_iota(jnp.int32, (n, n), 1)
             < lax.broadcasted_iota(jnp.int32, (n, n), 0)).astype(BF16)
    ties_before = _dot(lower, tie.astype(BF16))
    return (score_t > level) | (tie & (ties_before < k - above))


def _nsa_kernel(q_ref, kc_ref, vc_ref, ks_ref, vs_ref, kw_ref, vw_ref, gate_ref, ov_ref, o_ref,
                ksaug_ref, vst_ref, vwt_ref, *, n_sel):
    i = pl.program_id(2)
    tq, tk, hg = NSA_TQ, NSA_TK, NSA_HEADS_PER_GROUP
    s_len = ks_ref.shape[1]
    n_cmp_pad = kc_ref.shape[2]
    c0 = i * tq

    @pl.when(i == 0)
    def _():
        ksaug_ref[:, :LANES] = ks_ref[0]
        blk = lax.broadcasted_iota(jnp.int32, (s_len, LANES), 0) // NSA_SEL_BLOCK
        lane = lax.broadcasted_iota(jnp.int32, (s_len, LANES), 1)
        ksaug_ref[:, LANES:] = (blk == lane).astype(BF16)
        for c in range(s_len // tk):
            vst_ref[c] = _transposed_values_with_ones(vs_ref[0, c * tk:(c + 1) * tk, :])
        for c in range(s_len // tq):
            vwt_ref[c] = _transposed_values_with_ones(vw_ref[0, c * tq:(c + 1) * tq, :])

    q_all = q_ref[0]
    q_heads = [q_all[:, h * LANES:(h + 1) * LANES] for h in range(hg)]

    def key_iota(n):
        return lax.broadcasted_iota(jnp.int32, (n, tq), 0)

    def q_pos(n):
        return c0 + lax.broadcasted_iota(jnp.int32, (n, tq), 1)

    kc = kc_ref[0, 0]
    cmp_mask = key_iota(n_cmp_pad) * NSA_CMP_STRIDE + (NSA_CMP_BLOCK - 1) <= q_pos(n_cmp_pad)
    o_c, p_sum = [], jnp.zeros((n_cmp_pad, tq), F32)
    for qh in q_heads:
        z = jnp.where(cmp_mask, _dot_nt(kc, qh), NEG_INF)
        e = jnp.where(cmp_mask, jnp.exp2(z - jnp.max(z, axis=0, keepdims=True)), 0.0)
        l = jnp.sum(e, axis=0, keepdims=True)
        inv = 1.0 / jnp.where(l > 0.0, l, 1.0)
        o_c.append(_dot(vc_ref[0, 0], e.astype(BF16)) * inv)
        p_sum = p_sum + e * inv

    ps_hi, ps_lo = _split_bf16(p_sum)
    imp = _dot(ov_ref[...], ps_hi) + _dot(ov_ref[...], ps_lo)
    blk = key_iota(LANES)
    cur = q_pos(LANES) // NSA_SEL_BLOCK
    forced = (blk == 0) | (blk == cur) | (blk == cur - 1)
    score = jnp.where(blk <= cur, imp + NSA_FORCE_BONUS * forced.astype(F32), -1.0)
    score = jnp.where(blk < n_sel, score, -jnp.inf)
    chosen = _top_k_mask_t(score, min(NSA_SEL_TOPK, n_sel))
    sel_bias = jnp.where(chosen, 0.0, NEG_INF).T.astype(BF16)
    q_aug = jnp.concatenate([jnp.concatenate(q_heads, axis=0),
                             jnp.concatenate([sel_bias] * hg, axis=0)], axis=1)

    def keys(j):
        return ksaug_ref[pl.ds(pl.multiple_of(j * tk, tk), tk), :]

    def values_t(j):
        return vst_ref[j]

    jd = c0 // tk
    causal = jd * tk + key_iota(tk) <= q_pos(tk)
    s_t = jnp.where(jnp.concatenate([causal] * hg, axis=1), _dot_nt(keys(jd), q_aug), NEG_INF)
    m0 = jnp.max(s_t, axis=0, keepdims=True)
    carry = (m0, _dot(values_t(jd), jnp.exp2(s_t - m0).astype(BF16)))
    _, acc = _flash_loop_t(jd, q_aug, keys, values_t, carry)
    o_s = acc[:HEAD_DIM] * (1.0 / acc[HEAD_DIM:HEAD_DIM + 1])

    span = tq + NSA_WINDOW
    wstart = pl.multiple_of(jnp.maximum(c0 - NSA_WINDOW, 0), tq)
    w_chunk = wstart // tq
    gap = q_pos(span) - (wstart + key_iota(span))
    win_mask = (gap >= 0) & (gap < NSA_WINDOW)
    kw = kw_ref[0, pl.ds(wstart, span), :]
    o_w = []
    for qh in q_heads:
        z = jnp.where(win_mask, _dot_nt(kw, qh), NEG_INF)
        e = jnp.exp2(z - jnp.max(z, axis=0, keepdims=True)).astype(BF16)
        acc_w = _dot(vwt_ref[w_chunk], e[0:tq])
        for c in range(1, span // tq):
            acc_w = acc_w + _dot(vwt_ref[w_chunk + c], e[c * tq:(c + 1) * tq])
        o_w.append(acc_w[:HEAD_DIM] * (1.0 / acc_w[HEAD_DIM:HEAD_DIM + 1]))

    gates_t = (1.0 / (1.0 + jnp.exp(-gate_ref[0]))).T
    for h in range(hg):
        branches = (o_c[h], o_s[:, h * tq:(h + 1) * tq], o_w[h])
        out = jnp.zeros((HEAD_DIM, tq), F32)
        for branch, o_b in enumerate(branches):
            r = branch * hg + h
            out = out + gates_t[r:r + 1, :] * o_b
        o_ref[0, :, h * LANES:(h + 1) * LANES] = out.T.astype(o_ref.dtype)


def _nsa_attention(main, gate_logits, kc_cmp, vc_cmp_t, overlap_t):
    b, s, _ = main.shape
    g, hg = NSA_KV_GROUPS, NSA_HEADS_PER_GROUP
    n_sel = s // NSA_SEL_BLOCK
    assert s % NSA_TK == 0 and n_sel <= LANES
    n16 = kc_cmp.shape[2]
    q_blocks = N_HEADS

    def kv_spec(which):
        return pl.BlockSpec((1, s, LANES), lambda bi, gi, i: (bi, 0, q_blocks + which * g + gi),
                            pipeline_mode=pl.Buffered(1))

    return pl.pallas_call(
        functools.partial(_nsa_kernel, n_sel=n_sel),
        grid=(b, g, s // NSA_TQ),
        in_specs=[
            pl.BlockSpec((1, NSA_TQ, hg * LANES), lambda bi, gi, i: (bi, i, gi)),
            pl.BlockSpec((1, 1, n16, HEAD_DIM), lambda bi, gi, i: (bi, gi, 0, 0)),
            pl.BlockSpec((1, 1, HEAD_DIM, n16), lambda bi, gi, i: (bi, gi, 0, 0)),
            kv_spec(2), kv_spec(3), kv_spec(4), kv_spec(5),
            pl.BlockSpec((1, NSA_TQ, LANES), lambda bi, gi, i: (bi, i, gi)),
            _const_spec(overlap_t.shape),
        ],
        out_specs=pl.BlockSpec((1, NSA_TQ, hg * LANES), lambda bi, gi, i: (bi, i, gi)),
        out_shape=jax.ShapeDtypeStruct((b, s, N_HEADS * HEAD_DIM), BF16),
        scratch_shapes=[pltpu.VMEM((s, 2 * LANES), BF16),
                        pltpu.VMEM((s // NSA_TK, HEAD_DIM + ONES_ROWS, NSA_TK), BF16),
                        pltpu.VMEM((s // NSA_TQ, HEAD_DIM + ONES_ROWS, NSA_TQ), BF16)],
        compiler_params=_cparams("parallel", "parallel", "arbitrary"),
        name="nsa_attn",
    )(main, kc_cmp, vc_cmp_t, main, main, main, main, gate_logits, overlap_t)


def _nsa_overlap_t(seq):
    n_cmp = (seq - NSA_CMP_BLOCK) // NSA_CMP_STRIDE + 1
    n_sel = seq // NSA_SEL_BLOCK
    cmp_start = np.arange(seq // NSA_CMP_STRIDE) * NSA_CMP_STRIDE
    sel_start = np.arange(LANES) * NSA_SEL_BLOCK
    ov = ((cmp_start[:, None] < sel_start[None, :] + NSA_SEL_BLOCK)
          & (cmp_start[:, None] + NSA_CMP_BLOCK > sel_start[None, :]))
    ov &= (np.arange(seq // NSA_CMP_STRIDE)[:, None] < n_cmp) & (np.arange(LANES)[None, :] < n_sel)
    return jnp.asarray(ov.T, dtype=BF16)


def _glu_proj_kernel(x_ref, g_ref, w_ref, o_ref):
    h = _rms(x_ref[...], g_ref[...]).astype(BF16)
    d = o_ref.shape[1]
    for c in range(d // PROJ_CHUNK):
        a = _dot(h, w_ref[:, c * PROJ_CHUNK:(c + 1) * PROJ_CHUNK])
        gate = _dot(h, w_ref[:, d + c * PROJ_CHUNK:d + (c + 1) * PROJ_CHUNK])
        o_ref[:, c * PROJ_CHUNK:(c + 1) * PROJ_CHUNK] = (a / (1.0 + jnp.exp(-gate))).astype(o_ref.dtype)


def _glu_project(x2d, gain, w):
    t_rows, d = x2d.shape
    return pl.pallas_call(
        _glu_proj_kernel,
        grid=(t_rows // ROW_TILE,),
        in_specs=[pl.BlockSpec((ROW_TILE, d), lambda i: (i, 0)), _const_spec((1, d)), _const_spec(w.shape)],
        out_specs=pl.BlockSpec((ROW_TILE, d), lambda i: (i, 0)),
        out_shape=jax.ShapeDtypeStruct((t_rows, d), BF16),
        compiler_params=_cparams("parallel"),
        name="norm_glu_proj",
    )(x2d, gain.reshape(1, d), w)


CONV_TILE = 256
SUBLANES = 8
CONV_PHASE_ROWS = CONV_TILE + CONV_HALO - SUBLANES


def _conv_kernel(u_ref, halo_ref, dw_ref, db_ref, lg_ref, lb_ref, o_ref, ext_ref, phase_ref):
    i = pl.program_id(1)
    halo = halo_ref[0].astype(F32)
    ext_ref[0:CONV_HALO, :] = jnp.where(i == 0, 0.0, halo)
    ext_ref[CONV_HALO:, :] = u_ref[0].astype(F32)
    for b in range(1, SUBLANES):
        phase_ref[b - 1] = ext_ref[pl.ds(b, CONV_PHASE_ROWS), :]
    lead = CONV_HALO - (CONV_WIDTH - 1)
    for r in range(CONV_TILE // CONV_ROWS):
        acc = jnp.zeros((CONV_ROWS, D_MODEL), F32) + db_ref[...]
        for w in range(CONV_WIDTH):
            shift = (lead + w) % SUBLANES
            start = r * CONV_ROWS + lead + w - shift
            src = ext_ref if shift == 0 else phase_ref.at[shift - 1]
            acc = acc + dw_ref[w:w + 1, :] * src[pl.ds(start, CONV_ROWS), :]
        mu = jnp.mean(acc, axis=-1, keepdims=True)
        cen = acc - mu
        var = jnp.mean(cen * cen, axis=-1, keepdims=True)
        un = cen * lax.rsqrt(var + NORM_EPS) * lg_ref[...] + lb_ref[...]
        o_ref[0, r * CONV_ROWS:(r + 1) * CONV_ROWS, :] = (un / (1.0 + jnp.exp(-un))).astype(o_ref.dtype)


def _conv_ln_swish(u, dw_w, dw_b, ln_g, ln_b):
    b, s, d = u.shape
    halo_per_tile = CONV_TILE // CONV_HALO
    dw = jnp.concatenate([dw_w, jnp.zeros((1, d), F32)], axis=0)
    return pl.pallas_call(
        _conv_kernel,
        grid=(b, s // CONV_TILE),
        in_specs=[
            pl.BlockSpec((1, CONV_TILE, d), lambda bi, i: (bi, i, 0)),
            pl.BlockSpec((1, CONV_HALO, d), lambda bi, i: (bi, jnp.maximum(i * halo_per_tile - 1, 0), 0)),
            _const_spec(dw.shape), _const_spec((1, d)), _const_spec((1, d)), _const_spec((1, d)),
        ],
        out_specs=pl.BlockSpec((1, CONV_TILE, d), lambda bi, i: (bi, i, 0)),
        out_shape=jax.ShapeDtypeStruct((b, s, d), BF16),
        scratch_shapes=[pltpu.VMEM((CONV_TILE + CONV_HALO, d), F32),
                        pltpu.VMEM((SUBLANES - 1, CONV_PHASE_ROWS, d), F32)],
        compiler_params=_cparams("parallel", "parallel"),
        name="conv_ln_swish",
    )(u, u, dw, dw_b.reshape(1, d), ln_g.reshape(1, d), ln_b.reshape(1, d))


def _rope_tables(seq):
    half = HEAD_DIM // 2
    inv_freq = ROPE_THETA ** (-jnp.arange(half, dtype=F32) / half)
    ang = jnp.arange(seq, dtype=F32)[:, None] * inv_freq[None, :]
    cos, sin = jnp.cos(ang), jnp.sin(ang)
    return jnp.concatenate([cos, cos], axis=1), jnp.concatenate([-sin, sin], axis=1)


def _nsa_mixer(x2d, norm_gain, w_in, q_gain, k_gain, cmp_pos, w_cmp, tables, batch, seq):
    g, dh = NSA_KV_GROUPS, HEAD_DIM
    n_main = (N_HEADS + 6 * g) * dh
    scale = dh ** -0.5 * LOG2_E
    w_main = w_in[:, :n_main].astype(BF16)
    hg = NSA_HEADS_PER_GROUP
    w_g = w_in[:, n_main:].reshape(-1, 3, g, hg).transpose(0, 2, 1, 3).reshape(-1, g, 3 * hg)
    w_gate = jnp.pad(w_g, ((0, 0), (0, 0), (0, LANES - 3 * hg))).reshape(-1, g * LANES).astype(BF16)
    ones = jnp.ones((dh,), F32)
    head_gains = jnp.stack([q_gain * scale] * N_HEADS + [k_gain[0]] * g + [ones] * g
                           + [k_gain[1]] * g + [ones] * g + [k_gain[2]] * g + [ones] * g)
    rope_blocks = [True] * N_HEADS + [True] * g + [False] * g + [True] * g + [False] * g + [True] * g + [False] * g
    main, gate_logits = _project(x2d, norm_gain, w_main, *tables, head_gains, rope_blocks, seq, w_gate=w_gate)
    main = main.reshape(batch, seq, n_main)
    gate_logits = gate_logits.reshape(batch, seq, g * LANES)

    def cmp_layout(col0):
        t = main[:, :, col0:col0 + g * dh].reshape(batch, seq // NSA_CMP_STRIDE, NSA_CMP_STRIDE, g, dh)
        return t.transpose(0, 3, 1, 2, 4).reshape(batch, g, seq // NSA_CMP_STRIDE, CMP_HALF)

    kc_cmp, vc_cmp_t = _nsa_compress(cmp_layout(N_HEADS * dh), cmp_layout((N_HEADS + g) * dh), cmp_pos, w_cmp)
    out = _nsa_attention(main, gate_logits, kc_cmp, vc_cmp_t, _nsa_overlap_t(seq))
    return out.reshape(batch * seq, N_HEADS * dh)


def _qkv_weight_with_scaled_q(w_in):
    n_q = N_HEADS * HEAD_DIM
    return jnp.concatenate([w_in[:, :n_q] * HEAD_DIM ** -0.5, w_in[:, n_q:]], axis=1).astype(BF16)


def _sb_mixer(x2d, norm_gain, w_in, tables, batch, seq):
    n = 3 * N_HEADS * HEAD_DIM
    head_gains = jnp.ones((n // LANES, HEAD_DIM), F32)
    qkv = _project(x2d, norm_gain, _qkv_weight_with_scaled_q(w_in), *tables, head_gains,
                   [False] * (n // LANES), seq)
    return _sb_attention(qkv.reshape(batch, seq, n)).reshape(batch * seq, N_HEADS * HEAD_DIM)


def _conv_mixer(x2d, norm_gain, w_in, dw_w, dw_b, ln_g, ln_b, batch, seq):
    u = _glu_project(x2d, norm_gain, w_in.astype(BF16))
    a = _conv_ln_swish(u.reshape(batch, seq, D_MODEL), dw_w, dw_b, ln_g, ln_b)
    return a.reshape(batch * seq, D_MODEL)


def _moba_mixer(x2d, norm_gain, w_in, q_gain, k_gain, tables, batch, seq):
    n = 3 * N_HEADS * HEAD_DIM
    ones = jnp.ones((HEAD_DIM,), F32)
    head_gains = jnp.stack([q_gain * (HEAD_DIM ** -0.5 * LOG2_E)] * N_HEADS + [k_gain] * N_HEADS + [ones] * N_HEADS)
    rope_blocks = [True] * (2 * N_HEADS) + [False] * N_HEADS
    qkv = _project(x2d, norm_gain, w_in.astype(BF16), *tables, head_gains, rope_blocks, seq)
    return _moba_attention(qkv.reshape(batch, seq, n)).reshape(batch * seq, N_HEADS * HEAD_DIM)


def kernel(x, attn_norm, mlp_norm, mlp_w_up, mlp_w_down, nsa_w_in, nsa_q_norm, nsa_k_norm, nsa_cmp_pos, nsa_w_cmp, nsa_w_out, sb_w_in, sb_w_out, conv_w_in, conv_dw_w, conv_dw_b, conv_ln_g, conv_ln_b, conv_w_out, moba_w_in, moba_q_norm, moba_k_norm, moba_w_out):
    batch, seq, d = x.shape
    depth = attn_norm.shape[0]
    tables = _rope_tables(seq)
    x2d = x.reshape(batch * seq, d)
    for i in range(depth):
        m, j = i % 4, i // 4
        if m == 0:
            a = _nsa_mixer(x2d, attn_norm[i], nsa_w_in[j], nsa_q_norm[j], nsa_k_norm[j], nsa_cmp_pos[j],
                           nsa_w_cmp[j], tables, batch, seq)
            w_out = nsa_w_out[j]
        elif m == 1:
            a = _sb_mixer(x2d, attn_norm[i], sb_w_in[j], tables, batch, seq)
            w_out = sb_w_out[j]
        elif m == 2:
            a = _conv_mixer(x2d, attn_norm[i], conv_w_in[j], conv_dw_w[j], conv_dw_b[j], conv_ln_g[j],
                            conv_ln_b[j], batch, seq)
            w_out = conv_w_out[j]
        else:
            a = _moba_mixer(x2d, attn_norm[i], moba_w_in[j], moba_q_norm[j], moba_k_norm[j], tables, batch, seq)
            w_out = moba_w_out[j]
        x2d = _mixer_out_and_mlp(x2d, a, w_out.astype(BF16), mlp_norm[i], mlp_w_up[i].astype(BF16),
                                 mlp_w_down[i].astype(BF16))
    return x2d.reshape(batch, seq, d)
```

```python
import functools

import numpy as np
import jax
import jax.numpy as jnp
from jax import lax
from jax.experimental import pallas as pl
from jax.experimental.pallas import tpu as pltpu

F32 = jnp.float32
BF16 = jnp.bfloat16

D_MODEL = 1024
N_HEADS = 8
HEAD_DIM = 128
D_FF = 4 * D_MODEL
ROPE_THETA = 10000.0
NORM_EPS = 1e-6
NEG_INF = -1e30
LOG2_E = 1.4426950408889634

NSA_KV_GROUPS = 2
NSA_HEADS_PER_GROUP = N_HEADS // NSA_KV_GROUPS
NSA_CMP_BLOCK = 32
NSA_CMP_STRIDE = 16
NSA_SEL_BLOCK = 64
NSA_SEL_TOPK = 16
NSA_WINDOW = 512
NSA_FORCE_BONUS = 1000.0

CONV_WIDTH = 31
MOBA_BLOCK = 256
MOBA_TOPK = 3

LANES = 128
VMEM_LIMIT_BYTES = 56 * 1024 * 1024
ROW_TILE = 512
PROJ_CHUNK = 512
FF_CHUNK = 1024
CONV_HALO = 32
CONV_ROWS = 32


def _cparams(*sem):
    return pltpu.CompilerParams(dimension_semantics=sem, vmem_limit_bytes=VMEM_LIMIT_BYTES)


def _const_spec(shape):
    zeros = (0,) * len(shape)
    return pl.BlockSpec(shape, lambda *_: zeros, pipeline_mode=pl.Buffered(1))


def _rms(x, gain):
    return x * lax.rsqrt(jnp.mean(x * x, axis=-1, keepdims=True) + NORM_EPS) * gain


def _dot(a, b):
    return jnp.dot(a, b, preferred_element_type=F32)


def _dot_nt(a, b):
    return lax.dot_general(a, b, (((1,), (1,)), ((), ())), preferred_element_type=F32)


def _split_bf16(x):
    hi = x.astype(BF16)
    lo = (x - hi.astype(F32)).astype(BF16)
    return hi, lo


def _proj_kernel(*refs, rope_blocks, has_gate):
    if has_gate:
        x_ref, g_ref, w_ref, cos_ref, sin_ref, hg_ref, wg_ref, o_ref, og_ref = refs
    else:
        x_ref, g_ref, w_ref, cos_ref, sin_ref, hg_ref, o_ref = refs
    h = _rms(x_ref[...], g_ref[...]).astype(BF16)
    blocks_per_chunk = PROJ_CHUNK // LANES
    for c in range(len(rope_blocks) // blocks_per_chunk):
        y = _dot(h, w_ref[:, c * PROJ_CHUNK:(c + 1) * PROJ_CHUNK])
        chunk_flags = rope_blocks[c * blocks_per_chunk:(c + 1) * blocks_per_chunk]
        if not any(chunk_flags):
            o_ref[:, c * PROJ_CHUNK:(c + 1) * PROJ_CHUNK] = y.astype(o_ref.dtype)
            continue
        for k, flag in enumerate(chunk_flags):
            b = c * blocks_per_chunk + k
            yb = y[:, k * LANES:(k + 1) * LANES]
            if flag:
                yb = _rms(yb, hg_ref[b:b + 1, :])
                yb = yb * cos_ref[...] + pltpu.roll(yb, HEAD_DIM // 2, 1) * sin_ref[...]
            o_ref[:, b * LANES:(b + 1) * LANES] = yb.astype(o_ref.dtype)
    if has_gate:
        og_ref[...] = _dot(h, wg_ref[...])


def _project(x2d, gain, w, cos_t, sin_t, head_gains, rope_blocks, seq, w_gate=None):
    t_rows, d = x2d.shape
    n = w.shape[1]
    assert n % PROJ_CHUNK == 0 and len(rope_blocks) == n // LANES and seq % ROW_TILE == 0
    seq_tiles = seq // ROW_TILE
    has_gate = w_gate is not None
    in_specs = [
        pl.BlockSpec((ROW_TILE, d), lambda i: (i, 0)),
        _const_spec((1, d)),
        _const_spec((d, n)),
        pl.BlockSpec((ROW_TILE, LANES), lambda i: (i % seq_tiles, 0)),
        pl.BlockSpec((ROW_TILE, LANES), lambda i: (i % seq_tiles, 0)),
        _const_spec(head_gains.shape),
    ]
    args = [x2d, gain.reshape(1, d), w, cos_t, sin_t, head_gains]
    out_shape = [jax.ShapeDtypeStruct((t_rows, n), BF16)]
    out_specs = [pl.BlockSpec((ROW_TILE, n), lambda i: (i, 0))]
    if has_gate:
        in_specs.append(_const_spec(w_gate.shape))
        args.append(w_gate)
        out_shape.append(jax.ShapeDtypeStruct((t_rows, w_gate.shape[1]), F32))
        out_specs.append(pl.BlockSpec((ROW_TILE, w_gate.shape[1]), lambda i: (i, 0)))
    outs = pl.pallas_call(
        functools.partial(_proj_kernel, rope_blocks=tuple(rope_blocks), has_gate=has_gate),
        grid=(t_rows // ROW_TILE,),
        in_specs=in_specs,
        out_specs=out_specs,
        out_shape=out_shape,
        compiler_params=_cparams("parallel"),
        name="norm_proj",
    )(*args)
    return outs if has_gate else outs[0]


def _mlp_kernel(x_ref, a_ref, wo_ref, g_ref, wup_ref, wdn_ref, o_ref):
    x1 = x_ref[...] + _dot(a_ref[...], wo_ref[...])
    h = _rms(x1, g_ref[...]).astype(BF16)
    acc = x1
    for c in range(D_FF // FF_CHUNK):
        u = _dot(h, wup_ref[:, c * FF_CHUNK:(c + 1) * FF_CHUNK])
        act = jnp.square(jnp.maximum(u, 0.0)).astype(BF16)
        acc = acc + _dot(act, wdn_ref[c * FF_CHUNK:(c + 1) * FF_CHUNK, :])
    o_ref[...] = acc


def _mixer_out_and_mlp(x2d, a2d, w_out, gain, w_up, w_down):
    t_rows, d = x2d.shape
    return pl.pallas_call(
        _mlp_kernel,
        grid=(t_rows // ROW_TILE,),
        in_specs=[
            pl.BlockSpec((ROW_TILE, d), lambda i: (i, 0)),
            pl.BlockSpec((ROW_TILE, d), lambda i: (i, 0)),
            _const_spec((d, d)),
            _const_spec((1, d)),
            _const_spec((d, D_FF)),
            _const_spec((D_FF, d)),
        ],
        out_specs=pl.BlockSpec((ROW_TILE, d), lambda i: (i, 0)),
        out_shape=jax.ShapeDtypeStruct((t_rows, d), F32),
        compiler_params=_cparams("parallel"),
        name="outproj_mlp",
    )(x2d, a2d, w_out, gain.reshape(1, d), w_up, w_down)


SB_TILE = 256
SB_UNDERFLOW_LOG = -104.0


def _sb_kernel(q_ref, k_ref, v_ref, o_ref):
    i = pl.program_id(2)
    t = SB_TILE
    q = q_ref[0]
    row = lax.broadcasted_iota(jnp.int32, (t, t), 0)
    col = lax.broadcasted_iota(jnp.int32, (t, t), 1)
    below = row > col
    suffix_ones = below.astype(BF16)

    def block(j, carry, diagonal):
        o, later = carry
        start = pl.multiple_of(j * t, t)
        kj = k_ref[0, pl.ds(start, t), :]
        vj = v_ref[0, pl.ds(start, t), :]
        z = _dot_nt(q, kj)
        softplus = jnp.maximum(z, 0.0) + jnp.log(1.0 + jnp.exp(-jnp.abs(z)))
        log_om = -softplus
        if diagonal:
            log_om = jnp.where(below, log_om, 0.0)
        hi, lo = _split_bf16(log_om)
        between = _dot(hi, suffix_ones) + _dot(lo, suffix_ones) + later
        a = jnp.exp(z - softplus + between)
        if diagonal:
            a = jnp.where(below, a, 0.0)
        o = o + _dot(a.astype(BF16), vj)
        later = later + jnp.sum(log_om, axis=1, keepdims=True)
        return o, later

    carry = (jnp.zeros((t, HEAD_DIM), F32), jnp.zeros((t, 1), F32))
    o, later = block(i, carry, True)

    def more(state):
        j, _, later = state
        return (j >= 0) & (jnp.max(later) > SB_UNDERFLOW_LOG)

    def step(state):
        j, o, later = state
        o, later = block(j, (o, later), False)
        return j - 1, o, later

    _, o, _ = lax.while_loop(more, step, (i - 1, o, later))
    o_ref[0] = o.astype(o_ref.dtype)


def _sb_attention(qkv):
    b, s, _ = qkv.shape
    h = N_HEADS
    return pl.pallas_call(
        _sb_kernel,
        grid=(b, h, s // SB_TILE),
        in_specs=[
            pl.BlockSpec((1, SB_TILE, LANES), lambda bi, hi, i: (bi, i, hi)),
            pl.BlockSpec((1, s, LANES), lambda bi, hi, i: (bi, 0, h + hi)),
            pl.BlockSpec((1, s, LANES), lambda bi, hi, i: (bi, 0, 2 * h + hi)),
        ],
        out_specs=pl.BlockSpec((1, SB_TILE, LANES), lambda bi, hi, i: (bi, i, hi)),
        out_shape=jax.ShapeDtypeStruct((b, s, h * HEAD_DIM), BF16),
        compiler_params=_cparams("parallel", "parallel", "arbitrary"),
        name="stick_breaking_attn",
    )(qkv, qkv, qkv)


MOBA_TQ = 2048
MOBA_TK = 1024
ONES_ROWS = 16


def _online_softmax_step_t(carry, s_t, v_t):
    m, acc = carry
    m_new = jnp.maximum(m, jnp.max(s_t, axis=0, keepdims=True))
    p_t = jnp.exp2(s_t - m_new)
    acc = jnp.exp2(m - m_new) * acc + _dot(v_t, p_t.astype(BF16))
    return m_new, acc


def _flash_loop_t(n_steps, q_aug, keys_fn, values_t_fn, carry):
    def body(n, carry):
        return _online_softmax_step_t(carry, _dot_nt(keys_fn(n), q_aug), values_t_fn(n))

    return lax.fori_loop(0, n_steps, body, carry)


def _transposed_values_with_ones(v):
    v_t = v.astype(F32).T.astype(BF16)
    return jnp.concatenate([v_t, jnp.ones((ONES_ROWS, v.shape[0]), BF16)], axis=0)


def _moba_kernel(q_ref, k_ref, v_ref, o_ref, kaug_ref, kmean_ref, vt_ref, *, n_blk):
    i = pl.program_id(2)
    t, bs = MOBA_TQ, MOBA_BLOCK
    s_len = k_ref.shape[1]

    @pl.when(i == 0)
    def _():
        k = k_ref[0]
        kaug_ref[:, :LANES] = k
        blk = lax.broadcasted_iota(jnp.int32, (s_len, LANES), 0) // bs
        lane = lax.broadcasted_iota(jnp.int32, (s_len, LANES), 1)
        kaug_ref[:, LANES:] = (blk == lane).astype(BF16)
        kmean_ref[...] = jnp.mean(k.astype(F32).reshape(n_blk, bs, LANES), axis=1)
        for c in range(s_len // MOBA_TK):
            vt_ref[c] = _transposed_values_with_ones(v_ref[0, c * MOBA_TK:(c + 1) * MOBA_TK, :])

    q = q_ref[0]
    km_hi, km_lo = _split_bf16(kmean_ref[...])
    gate = _dot_nt(km_hi, q) + _dot_nt(km_lo, q)
    blk_id = lax.broadcasted_iota(jnp.int32, (n_blk, t), 0)
    cur = i * (t // bs) + lax.broadcasted_iota(jnp.int32, (n_blk, t), 1) // bs
    past = blk_id < cur
    gate = jnp.where(past, gate, -jnp.inf)
    rank = jnp.zeros((n_blk, t), jnp.int32)
    for m in range(n_blk):
        gm = gate[m:m + 1, :]
        beats = (gm > gate) | ((gm == gate) & (m < blk_id))
        rank = rank + beats.astype(jnp.int32)
    visible = ((rank < MOBA_TOPK) & past) | (blk_id == cur)
    sel_bias = jnp.where(visible, 0.0, NEG_INF)
    sel_bias = jnp.concatenate([sel_bias, jnp.zeros((LANES - n_blk, t), F32)], axis=0)
    q_aug = jnp.concatenate([q, sel_bias.T.astype(BF16)], axis=1)

    tk = MOBA_TK

    def keys(n):
        return kaug_ref[pl.ds(pl.multiple_of(n * tk, tk), tk), :]

    def values_t(n):
        return vt_ref[n]

    own = i * (t // tk)
    key = lax.broadcasted_iota(jnp.int32, (tk, t), 0)
    query = i * t + lax.broadcasted_iota(jnp.int32, (tk, t), 1)

    def causal_scores_t(n):
        return jnp.where(n * tk + key <= query, _dot_nt(keys(n), q_aug), NEG_INF)

    s_t = causal_scores_t(own)
    m0 = jnp.max(s_t, axis=0, keepdims=True)
    carry = (m0, _dot(values_t(own), jnp.exp2(s_t - m0).astype(BF16)))
    for extra in range(1, t // tk):
        carry = _online_softmax_step_t(carry, causal_scores_t(own + extra), values_t(own + extra))

    _, acc = _flash_loop_t(own, q_aug, keys, values_t, carry)
    o_t = acc[:HEAD_DIM] * (1.0 / acc[HEAD_DIM:HEAD_DIM + 1])
    o_ref[0] = o_t.T.astype(o_ref.dtype)


def _moba_attention(qkv):
    b, s, _ = qkv.shape
    h = N_HEADS
    assert s % MOBA_TQ == 0 and MOBA_TQ % MOBA_BLOCK == 0
    n_blk = s // MOBA_BLOCK
    assert n_blk % 8 == 0 and n_blk <= LANES
    return pl.pallas_call(
        functools.partial(_moba_kernel, n_blk=n_blk),
        grid=(b, h, s // MOBA_TQ),
        in_specs=[
            pl.BlockSpec((1, MOBA_TQ, LANES), lambda bi, hi, i: (bi, i, hi)),
            pl.BlockSpec((1, s, LANES), lambda bi, hi, i: (bi, 0, h + hi)),
            pl.BlockSpec((1, s, LANES), lambda bi, hi, i: (bi, 0, 2 * h + hi)),
        ],
        out_specs=pl.BlockSpec((1, MOBA_TQ, LANES), lambda bi, hi, i: (bi, i, hi)),
        out_shape=jax.ShapeDtypeStruct((b, s, h * HEAD_DIM), BF16),
        scratch_shapes=[pltpu.VMEM((s, 2 * LANES), BF16), pltpu.VMEM((n_blk, LANES), F32),
                        pltpu.VMEM((s // MOBA_TK, HEAD_DIM + ONES_ROWS, MOBA_TK), BF16)],
        compiler_params=_cparams("parallel", "parallel", "arbitrary"),
        name="moba_attn",
    )(qkv, qkv, qkv)


NSA_TQ = 512
NSA_TK = 1024
CMP_HALF = NSA_CMP_STRIDE * HEAD_DIM


def _cmp_kernel(xk_ref, xv_ref, pos_ref, w_ref, ok_ref, ov_ref):
    n16 = xk_ref.shape[2]
    for t, (x_ref, o_ref) in enumerate(((xk_ref, ok_ref), (xv_ref, ov_ref))):
        x = x_ref[0, 0].astype(F32)
        first = _dot((x + pos_ref[t, 0:1, :]).astype(BF16), w_ref[t, 0])
        second = _dot((x + pos_ref[t, 1:2, :]).astype(BF16), w_ref[t, 1])
        out = first + pltpu.roll(second, n16 - 1, 0)
        o_ref[0, 0] = (out if t == 0 else out.T).astype(o_ref.dtype)


def _nsa_compress(kc, vc, cmp_pos, w_cmp):
    b, g, n16, _ = kc.shape
    pos = cmp_pos.reshape(2, 2, CMP_HALF).astype(F32)
    w = w_cmp.reshape(2, 2, CMP_HALF, HEAD_DIM).astype(BF16)
    x_spec = pl.BlockSpec((1, 1, n16, CMP_HALF), lambda bi, gi: (bi, gi, 0, 0))
    return pl.pallas_call(
        _cmp_kernel,
        grid=(b, g),
        in_specs=[x_spec, x_spec, _const_spec(pos.shape), _const_spec(w.shape)],
        out_specs=[pl.BlockSpec((1, 1, n16, HEAD_DIM), lambda bi, gi: (bi, gi, 0, 0)),
                   pl.BlockSpec((1, 1, HEAD_DIM, n16), lambda bi, gi: (bi, gi, 0, 0))],
        out_shape=[jax.ShapeDtypeStruct((b, g, n16, HEAD_DIM), BF16),
                   jax.ShapeDtypeStruct((b, g, HEAD_DIM, n16), BF16)],
        compiler_params=_cparams("parallel", "parallel"),
        name="nsa_compress",
    )(kc, vc, pos, w)


def _top_k_mask_t(score_t, k):
    n, q = score_t.shape
    work = score_t
    taken = jnp.zeros((1, q), F32)
    level = jnp.full((1, q), jnp.inf, F32)
    above = jnp.zeros((1, q), F32)
    for _ in range(k):
        best = jnp.max(work, axis=0, keepdims=True)
        hit = work == best
        active = taken < k
        level = jnp.where(active, best, level)
        above = jnp.where(active, taken, above)
        taken = taken + jnp.sum(hit.astype(F32), axis=0, keepdims=True)
        work = jnp.where(hit, -jnp.inf, work)
    tie = score_t == level
    lower = (lax.broadcasted_iota(jnp.int32, (n, n), 1)
             < lax.broadcasted_iota(jnp.int32, (n, n), 0)).astype(BF16)
    ties_before = _dot(lower, tie.astype(BF16))
    return (score_t > level) | (tie & (ties_before < k - above))


def _nsa_kernel(q_ref, kc_ref, vc_ref, ks_ref, vs_ref, kw_ref, vw_ref, gate_ref, ov_ref, o_ref,
                ksaug_ref, vst_ref, vwt_ref, *, n_sel):
    i = pl.program_id(2)
    tq, tk, hg = NSA_TQ, NSA_TK, NSA_HEADS_PER_GROUP
    s_len = ks_ref.shape[1]
    n_cmp_pad = kc_ref.shape[2]
    c0 = i * tq

    @pl.when(i == 0)
    def _():
        ksaug_ref[:, :LANES] = ks_ref[0]
        blk = lax.broadcasted_iota(jnp.int32, (s_len, LANES), 0) // NSA_SEL_BLOCK
        lane = lax.broadcasted_iota(jnp.int32, (s_len, LANES), 1)
        ksaug_ref[:, LANES:] = (blk == lane).astype(BF16)
        for c in range(s_len // tk):
            vst_ref[c] = _transposed_values_with_ones(vs_ref[0, c * tk:(c + 1) * tk, :])
        for c in range(s_len // tq):
            vwt_ref[c] = _transposed_values_with_ones(vw_ref[0, c * tq:(c + 1) * tq, :])

    q_all = q_ref[0]
    q_heads = [q_all[:, h * LANES:(h + 1) * LANES] for h in range(hg)]

    def key_iota(n):
        return lax.broadcasted_iota(jnp.int32, (n, tq), 0)

    def q_pos(n):
        return c0 + lax.broadcasted_iota(jnp.int32, (n, tq), 1)

    kc = kc_ref[0, 0]
    cmp_mask = key_iota(n_cmp_pad) * NSA_CMP_STRIDE + (NSA_CMP_BLOCK - 1) <= q_pos(n_cmp_pad)
    o_c, p_sum = [], jnp.zeros((n_cmp_pad, tq), F32)
    for qh in q_heads:
        z = jnp.where(cmp_mask, _dot_nt(kc, qh), NEG_INF)
        e = jnp.where(cmp_mask, jnp.exp2(z - jnp.max(z, axis=0, keepdims=True)), 0.0)
        l = jnp.sum(e, axis=0, keepdims=True)
        inv = 1.0 / jnp.where(l > 0.0, l, 1.0)
        o_c.append(_dot(vc_ref[0, 0], e.astype(BF16)) * inv)
        p_sum = p_sum + e * inv

    ps_hi, ps_lo = _split_bf16(p_sum)
    imp = _dot(ov_ref[...], ps_hi) + _dot(ov_ref[...], ps_lo)
    blk = key_iota(LANES)
    cur = q_pos(LANES) // NSA_SEL_BLOCK
    forced = (blk == 0) | (blk == cur) | (blk == cur - 1)
    score = jnp.where(blk <= cur, imp + NSA_FORCE_BONUS * forced.astype(F32), -1.0)
    score = jnp.where(blk < n_sel, score, -jnp.inf)
    chosen = _top_k_mask_t(score, min(NSA_SEL_TOPK, n_sel))
    sel_bias = jnp.where(chosen, 0.0, NEG_INF).T.astype(BF16)
    q_aug = jnp.concatenate([jnp.concatenate(q_heads, axis=0),
                             jnp.concatenate([sel_bias] * hg, axis=0)], axis=1)

    def keys(j):
        return ksaug_ref[pl.ds(pl.multiple_of(j * tk, tk), tk), :]

    def values_t(j):
        return vst_ref[j]

    jd = c0 // tk
    causal = jd * tk + key_iota(tk) <= q_pos(tk)
    s_t = jnp.where(jnp.concatenate([causal] * hg, axis=1), _dot_nt(keys(jd), q_aug), NEG_INF)
    m0 = jnp.max(s_t, axis=0, keepdims=True)
    carry = (m0, _dot(values_t(jd), jnp.exp2(s_t - m0).astype(BF16)))
    _, acc = _flash_loop_t(jd, q_aug, keys, values_t, carry)
    o_s = acc[:HEAD_DIM] * (1.0 / acc[HEAD_DIM:HEAD_DIM + 1])

    span = tq + NSA_WINDOW
    wstart = pl.multiple_of(jnp.maximum(c0 - NSA_WINDOW, 0), tq)
    w_chunk = wstart // tq
    gap = q_pos(span) - (wstart + key_iota(span))
    win_mask = (gap >= 0) & (gap < NSA_WINDOW)
    kw = kw_ref[0, pl.ds(wstart, span), :]
    o_w = []
    for qh in q_heads:
        z = jnp.where(win_mask, _dot_nt(kw, qh), NEG_INF)
        e = jnp.exp2(z - jnp.max(z, axis=0, keepdims=True)).astype(BF16)
        acc_w = _dot(vwt_ref[w_chunk], e[0:tq])
        for c in range(1, span // tq):
            acc_w = acc_w + _dot(vwt_ref[w_chunk + c], e[c * tq:(c + 1) * tq])
        o_w.append(acc_w[:HEAD_DIM] * (1.0 / acc_w[HEAD_DIM:HEAD_DIM + 1]))

    gates_t = (1.0 / (1.0 + jnp.exp(-gate_ref[0]))).T
    for h in range(hg):
        branches = (o_c[h], o_s[:, h * tq:(h + 1) * tq], o_w[h])
        out = jnp.zeros((HEAD_DIM, tq), F32)
        for branch, o_b in enumerate(branches):
            r = branch * hg + h
            out = out + gates_t[r:r + 1, :] * o_b
        o_ref[0, :, h * LANES:(h + 1) * LANES] = out.T.astype(o_ref.dtype)


def _nsa_attention(main, gate_logits, kc_cmp, vc_cmp_t, overlap_t):
    b, s, _ = main.shape
    g, hg = NSA_KV_GROUPS, NSA_HEADS_PER_GROUP
    n_sel = s // NSA_SEL_BLOCK
    assert s % NSA_TK == 0 and n_sel <= LANES
    n16 = kc_cmp.shape[2]
    q_blocks = N_HEADS

    def kv_spec(which):
        return pl.BlockSpec((1, s, LANES), lambda bi, gi, i: (bi, 0, q_blocks + which * g + gi),
                            pipeline_mode=pl.Buffered(1))

    return pl.pallas_call(
        functools.partial(_nsa_kernel, n_sel=n_sel),
        grid=(b, g, s // NSA_TQ),
        in_specs=[
            pl.BlockSpec((1, NSA_TQ, hg * LANES), lambda bi, gi, i: (bi, i, gi)),
            pl.BlockSpec((1, 1, n16, HEAD_DIM), lambda bi, gi, i: (bi, gi, 0, 0)),
            pl.BlockSpec((1, 1, HEAD_DIM, n16), lambda bi, gi, i: (bi, gi, 0, 0)),
            kv_spec(2), kv_spec(3), kv_spec(4), kv_spec(5),
            pl.BlockSpec((1, NSA_TQ, LANES), lambda bi, gi, i: (bi, i, gi)),
            _const_spec(overlap_t.shape),
        ],
        out_specs=pl.BlockSpec((1, NSA_TQ, hg * LANES), lambda bi, gi, i: (bi, i, gi)),
        out_shape=jax.ShapeDtypeStruct((b, s, N_HEADS * HEAD_DIM), BF16),
        scratch_shapes=[pltpu.VMEM((s, 2 * LANES), BF16),
                        pltpu.VMEM((s // NSA_TK, HEAD_DIM + ONES_ROWS, NSA_TK), BF16),
                        pltpu.VMEM((s // NSA_TQ, HEAD_DIM + ONES_ROWS, NSA_TQ), BF16)],
        compiler_params=_cparams("parallel", "parallel", "arbitrary"),
        name="nsa_attn",
    )(main, kc_cmp, vc_cmp_t, main, main, main, main, gate_logits, overlap_t)


def _nsa_overlap_t(seq):
    n_cmp = (seq - NSA_CMP_BLOCK) // NSA_CMP_STRIDE + 1
    n_sel = seq // NSA_SEL_BLOCK
    cmp_start = np.arange(seq // NSA_CMP_STRIDE) * NSA_CMP_STRIDE
    sel_start = np.arange(LANES) * NSA_SEL_BLOCK
    ov = ((cmp_start[:, None] < sel_start[None, :] + NSA_SEL_BLOCK)
          & (cmp_start[:, None] + NSA_CMP_BLOCK > sel_start[None, :]))
    ov &= (np.arange(seq // NSA_CMP_STRIDE)[:, None] < n_cmp) & (np.arange(LANES)[None, :] < n_sel)
    return jnp.asarray(ov.T, dtype=BF16)


def _glu_proj_kernel(x_ref, g_ref, w_ref, o_ref):
    h = _rms(x_ref[...], g_ref[...]).astype(BF16)
    d = o_ref.shape[1]
    for c in range(d // PROJ_CHUNK):
        a = _dot(h, w_ref[:, c * PROJ_CHUNK:(c + 1) * PROJ_CHUNK])
        gate = _dot(h, w_ref[:, d + c * PROJ_CHUNK:d + (c + 1) * PROJ_CHUNK])
        o_ref[:, c * PROJ_CHUNK:(c + 1) * PROJ_CHUNK] = (a / (1.0 + jnp.exp(-gate))).astype(o_ref.dtype)


def _glu_project(x2d, gain, w):
    t_rows, d = x2d.shape
    return pl.pallas_call(
        _glu_proj_kernel,
        grid=(t_rows // ROW_TILE,),
        in_specs=[pl.BlockSpec((ROW_TILE, d), lambda i: (i, 0)), _const_spec((1, d)), _const_spec(w.shape)],
        out_specs=pl.BlockSpec((ROW_TILE, d), lambda i: (i, 0)),
        out_shape=jax.ShapeDtypeStruct((t_rows, d), BF16),
        compiler_params=_cparams("parallel"),
        name="norm_glu_proj",
    )(x2d, gain.reshape(1, d), w)


CONV_TILE = 256
SUBLANES = 8
CONV_PHASE_ROWS = CONV_TILE + CONV_HALO - SUBLANES


def _conv_kernel(u_ref, halo_ref, dw_ref, db_ref, lg_ref, lb_ref, o_ref, ext_ref, phase_ref):
    i = pl.program_id(1)
    halo = halo_ref[0].astype(F32)
    ext_ref[0:CONV_HALO, :] = jnp.where(i == 0, 0.0, halo)
    ext_ref[CONV_HALO:, :] = u_ref[0].astype(F32)
    for b in range(1, SUBLANES):
        phase_ref[b - 1] = ext_ref[pl.ds(b, CONV_PHASE_ROWS), :]
    lead = CONV_HALO - (CONV_WIDTH - 1)
    for r in range(CONV_TILE // CONV_ROWS):
        acc = jnp.zeros((CONV_ROWS, D_MODEL), F32) + db_ref[...]
        for w in range(CONV_WIDTH):
            shift = (lead + w) % SUBLANES
            start = r * CONV_ROWS + lead + w - shift
            src = ext_ref if shift == 0 else phase_ref.at[shift - 1]
            acc = acc + dw_ref[w:w + 1, :] * src[pl.ds(start, CONV_ROWS), :]
        mu = jnp.mean(acc, axis=-1, keepdims=True)
        cen = acc - mu
        var = jnp.mean(cen * cen, axis=-1, keepdims=True)
        un = cen * lax.rsqrt(var + NORM_EPS) * lg_ref[...] + lb_ref[...]
        o_ref[0, r * CONV_ROWS:(r + 1) * CONV_ROWS, :] = (un / (1.0 + jnp.exp(-un))).astype(o_ref.dtype)


def _conv_ln_swish(u, dw_w, dw_b, ln_g, ln_b):
    b, s, d = u.shape
    halo_per_tile = CONV_TILE // CONV_HALO
    dw = jnp.concatenate([dw_w, jnp.zeros((1, d), F32)], axis=0)
    return pl.pallas_call(
        _conv_kernel,
        grid=(b, s // CONV_TILE),
        in_specs=[
            pl.BlockSpec((1, CONV_TILE, d), lambda bi, i: (bi, i, 0)),
            pl.BlockSpec((1, CONV_HALO, d), lambda bi, i: (bi, jnp.maximum(i * halo_per_tile - 1, 0), 0)),
            _const_spec(dw.shape), _const_spec((1, d)), _const_spec((1, d)), _const_spec((1, d)),
        ],
        out_specs=pl.BlockSpec((1, CONV_TILE, d), lambda bi, i: (bi, i, 0)),
        out_shape=jax.ShapeDtypeStruct((b, s, d), BF16),
        scratch_shapes=[pltpu.VMEM((CONV_TILE + CONV_HALO, d), F32),
                        pltpu.VMEM((SUBLANES - 1, CONV_PHASE_ROWS, d), F32)],
        compiler_params=_cparams("parallel", "parallel"),
        name="conv_ln_swish",
    )(u, u, dw, dw_b.reshape(1, d), ln_g.reshape(1, d), ln_b.reshape(1, d))


def _rope_tables(seq):
    half = HEAD_DIM // 2
    inv_freq = ROPE_THETA ** (-jnp.arange(half, dtype=F32) / half)
    ang = jnp.arange(seq, dtype=F32)[:, None] * inv_freq[None, :]
    cos, sin = jnp.cos(ang), jnp.sin(ang)
    return jnp.concatenate([cos, cos], axis=1), jnp.concatenate([-sin, sin], axis=1)


def _nsa_mixer(x2d, norm_gain, w_in, q_gain, k_gain, cmp_pos, w_cmp, tables, batch, seq):
    g, dh = NSA_KV_GROUPS, HEAD_DIM
    n_main = (N_HEADS + 6 * g) * dh
    scale = dh ** -0.5 * LOG2_E
    w_main = w_in[:, :n_main].astype(BF16)
    hg = NSA_HEADS_PER_GROUP
    w_g = w_in[:, n_main:].reshape(-1, 3, g, hg).transpose(0, 2, 1, 3).reshape(-1, g, 3 * hg)
    w_gate = jnp.pad(w_g, ((0, 0), (0, 0), (0, LANES - 3 * hg))).reshape(-1, g * LANES).astype(BF16)
    ones = jnp.ones((dh,), F32)
    head_gains = jnp.stack([q_gain * scale] * N_HEADS + [k_gain[0]] * g + [ones] * g
                           + [k_gain[1]] * g + [ones] * g + [k_gain[2]] * g + [ones] * g)
    rope_blocks = [True] * N_HEADS + [True] * g + [False] * g + [True] * g + [False] * g + [True] * g + [False] * g
    main, gate_logits = _project(x2d, norm_gain, w_main, *tables, head_gains, rope_blocks, seq, w_gate=w_gate)
    main = main.reshape(batch, seq, n_main)
    gate_logits = gate_logits.reshape(batch, seq, g * LANES)

    def cmp_layout(col0):
        t = main[:, :, col0:col0 + g * dh].reshape(batch, seq // NSA_CMP_STRIDE, NSA_CMP_STRIDE, g, dh)
        return t.transpose(0, 3, 1, 2, 4).reshape(batch, g, seq // NSA_CMP_STRIDE, CMP_HALF)

    kc_cmp, vc_cmp_t = _nsa_compress(cmp_layout(N_HEADS * dh), cmp_layout((N_HEADS + g) * dh), cmp_pos, w_cmp)
    out = _nsa_attention(main, gate_logits, kc_cmp, vc_cmp_t, _nsa_overlap_t(seq))
    return out.reshape(batch * seq, N_HEADS * dh)


def _qkv_weight_with_scaled_q(w_in):
    n_q = N_HEADS * HEAD_DIM
    return jnp.concatenate([w_in[:, :n_q] * HEAD_DIM ** -0.5, w_in[:, n_q:]], axis=1).astype(BF16)


def _sb_mixer(x2d, norm_gain, w_in, tables, batch, seq):
    n = 3 * N_HEADS * HEAD_DIM
    head_gains = jnp.ones((n // LANES, HEAD_DIM), F32)
    qkv = _project(x2d, norm_gain, _qkv_weight_with_scaled_q(w_in), *tables, head_gains,
                   [False] * (n // LANES), seq)
    return _sb_attention(qkv.reshape(batch, seq, n)).reshape(batch * seq, N_HEADS * HEAD_DIM)


def _conv_mixer(x2d, norm_gain, w_in, dw_w, dw_b, ln_g, ln_b, batch, seq):
    u = _glu_project(x2d, norm_gain, w_in.astype(BF16))
    a = _conv_ln_swish(u.reshape(batch, seq, D_MODEL), dw_w, dw_b, ln_g, ln_b)
    return a.reshape(batch * seq, D_MODEL)


def _moba_mixer(x2d, norm_gain, w_in, q_gain, k_gain, tables, batch, seq):
    n = 3 * N_HEADS * HEAD_DIM
    ones = jnp.ones((HEAD_DIM,), F32)
    head_gains = jnp.stack([q_gain * (HEAD_DIM ** -0.5 * LOG2_E)] * N_HEADS + [k_gain] * N_HEADS + [ones] * N_HEADS)
    rope_blocks = [True] * (2 * N_HEADS) + [False] * N_HEADS
    qkv = _project(x2d, norm_gain, w_in.astype(BF16), *tables, head_gains, rope_blocks, seq)
    return _moba_attention(qkv.reshape(batch, seq, n)).reshape(batch * seq, N_HEADS * HEAD_DIM)


def kernel(x, attn_norm, mlp_norm, mlp_w_up, mlp_w_down, nsa_w_in, nsa_q_norm, nsa_k_norm, nsa_cmp_pos, nsa_w_cmp, nsa_w_out, sb_w_in, sb_w_out, conv_w_in, conv_dw_w, conv_dw_b, conv_ln_g, conv_ln_b, conv_w_out, moba_w_in, moba_q_norm, moba_k_norm, moba_w_out):
    batch, seq, d = x.shape
    depth = attn_norm.shape[0]
    tables = _rope_tables(seq)
    x2d = x.reshape(batch * seq, d)
    for i in range(depth):
        m, j = i % 4, i // 4
        if m == 0:
            a = _nsa_mixer(x2d, attn_norm[i], nsa_w_in[j], nsa_q_norm[j], nsa_k_norm[j], nsa_cmp_pos[j],
                           nsa_w_cmp[j], tables, batch, seq)
            w_out = nsa_w_out[j]
        elif m == 1:
            a = _sb_mixer(x2d, attn_norm[i], sb_w_in[j], tables, batch, seq)
            w_out = sb_w_out[j]
        elif m == 2:
            a = _conv_mixer(x2d, attn_norm[i], conv_w_in[j], conv_dw_w[j], conv_dw_b[j], conv_ln_g[j],
                            conv_ln_b[j], batch, seq)
            w_out = conv_w_out[j]
        else:
            a = _moba_mixer(x2d, attn_norm[i], moba_w_in[j], moba_q_norm[j], moba_k_norm[j], tables, batch, seq)
            w_out = moba_w_out[j]
        x2d = _mixer_out_and_mlp(x2d, a, w_out.astype(BF16), mlp_norm[i], mlp_w_up[i].astype(BF16),
                                 mlp_w_down[i].astype(BF16))
    return x2d.reshape(batch, seq, d)
```

```python
import functools

import numpy as np
import jax
import jax.numpy as jnp
from jax import lax
from jax.experimental import pallas as pl
from jax.experimental.pallas import tpu as pltpu

F32 = jnp.float32
BF16 = jnp.bfloat16

D_MODEL = 1024
N_HEADS = 8
HEAD_DIM = 128
D_FF = 4 * D_MODEL
ROPE_THETA = 10000.0
NORM_EPS = 1e-6
NEG_INF = -1e30
LOG2_E = 1.4426950408889634

NSA_KV_GROUPS = 2
NSA_HEADS_PER_GROUP = N_HEADS // NSA_KV_GROUPS
NSA_CMP_BLOCK = 32
NSA_CMP_STRIDE = 16
NSA_SEL_BLOCK = 64
NSA_SEL_TOPK = 16
NSA_WINDOW = 512
NSA_FORCE_BONUS = 1000.0

CONV_WIDTH = 31
MOBA_BLOCK = 256
MOBA_TOPK = 3

LANES = 128
VMEM_LIMIT_BYTES = 56 * 1024 * 1024
ROW_TILE = 512
PROJ_CHUNK = 512
FF_CHUNK = 1024
CONV_HALO = 32
CONV_ROWS = 32


def _cparams(*sem):
    return pltpu.CompilerParams(dimension_semantics=sem, vmem_limit_bytes=VMEM_LIMIT_BYTES)


def _const_spec(shape):
    zeros = (0,) * len(shape)
    return pl.BlockSpec(shape, lambda *_: zeros, pipeline_mode=pl.Buffered(1))


def _rms(x, gain):
    return x * lax.rsqrt(jnp.mean(x * x, axis=-1, keepdims=True) + NORM_EPS) * gain


def _dot(a, b):
    return jnp.dot(a, b, preferred_element_type=F32)


def _dot_nt(a, b):
    return lax.dot_general(a, b, (((1,), (1,)), ((), ())), preferred_element_type=F32)


def _split_bf16(x):
    hi = x.astype(BF16)
    lo = (x - hi.astype(F32)).astype(BF16)
    return hi, lo


def _proj_kernel(*refs, rope_blocks, has_gate):
    if has_gate:
        x_ref, g_ref, w_ref, cos_ref, sin_ref, hg_ref, wg_ref, o_ref, og_ref = refs
    else:
        x_ref, g_ref, w_ref, cos_ref, sin_ref, hg_ref, o_ref = refs
    h = _rms(x_ref[...], g_ref[...]).astype(BF16)
    blocks_per_chunk = PROJ_CHUNK // LANES
    for c in range(len(rope_blocks) // blocks_per_chunk):
        y = _dot(h, w_ref[:, c * PROJ_CHUNK:(c + 1) * PROJ_CHUNK])
        chunk_flags = rope_blocks[c * blocks_per_chunk:(c + 1) * blocks_per_chunk]
        if not any(chunk_flags):
            o_ref[:, c * PROJ_CHUNK:(c + 1) * PROJ_CHUNK] = y.astype(o_ref.dtype)
            continue
        for k, flag in enumerate(chunk_flags):
            b = c * blocks_per_chunk + k
            yb = y[:, k * LANES:(k + 1) * LANES]
            if flag:
                yb = _rms(yb, hg_ref[b:b + 1, :])
                yb = yb * cos_ref[...] + pltpu.roll(yb, HEAD_DIM // 2, 1) * sin_ref[...]
            o_ref[:, b * LANES:(b + 1) * LANES] = yb.astype(o_ref.dtype)
    if has_gate:
        og_ref[...] = _dot(h, wg_ref[...])


def _project(x2d, gain, w, cos_t, sin_t, head_gains, rope_blocks, seq, w_gate=None):
    t_rows, d = x2d.shape
    n = w.shape[1]
    assert n % PROJ_CHUNK == 0 and len(rope_blocks) == n // LANES and seq % ROW_TILE == 0
    seq_tiles = seq // ROW_TILE
    has_gate = w_gate is not None
    in_specs = [
        pl.BlockSpec((ROW_TILE, d), lambda i: (i, 0)),
        _const_spec((1, d)),
        _const_spec((d, n)),
        pl.BlockSpec((ROW_TILE, LANES), lambda i: (i % seq_tiles, 0)),
        pl.BlockSpec((ROW_TILE, LANES), lambda i: (i % seq_tiles, 0)),
        _const_spec(head_gains.shape),
    ]
    args = [x2d, gain.reshape(1, d), w, cos_t, sin_t, head_gains]
    out_shape = [jax.ShapeDtypeStruct((t_rows, n), BF16)]
    out_specs = [pl.BlockSpec((ROW_TILE, n), lambda i: (i, 0))]
    if has_gate:
        in_specs.append(_const_spec(w_gate.shape))
        args.append(w_gate)
        out_shape.append(jax.ShapeDtypeStruct((t_rows, w_gate.shape[1]), F32))
        out_specs.append(pl.BlockSpec((ROW_TILE, w_gate.shape[1]), lambda i: (i, 0)))
    outs = pl.pallas_call(
        functools.partial(_proj_kernel, rope_blocks=tuple(rope_blocks), has_gate=has_gate),
        grid=(t_rows // ROW_TILE,),
        in_specs=in_specs,
        out_specs=out_specs,
        out_shape=out_shape,
        compiler_params=_cparams("parallel"),
        name="norm_proj",
    )(*args)
    return outs if has_gate else outs[0]


def _mlp_kernel(x_ref, a_ref, wo_ref, g_ref, wup_ref, wdn_ref, o_ref):
    x1 = x_ref[...] + _dot(a_ref[...], wo_ref[...])
    h = _rms(x1, g_ref[...]).astype(BF16)
    acc = x1
    for c in range(D_FF // FF_CHUNK):
        u = _dot(h, wup_ref[:, c * FF_CHUNK:(c + 1) * FF_CHUNK])
        act = jnp.square(jnp.maximum(u, 0.0)).astype(BF16)
        acc = acc + _dot(act, wdn_ref[c * FF_CHUNK:(c + 1) * FF_CHUNK, :])
    o_ref[...] = acc


def _mixer_out_and_mlp(x2d, a2d, w_out, gain, w_up, w_down):
    t_rows, d = x2d.shape
    return pl.pallas_call(
        _mlp_kernel,
        grid=(t_rows // ROW_TILE,),
        in_specs=[
            pl.BlockSpec((ROW_TILE, d), lambda i: (i, 0)),
            pl.BlockSpec((ROW_TILE, d), lambda i: (i, 0)),
            _const_spec((d, d)),
            _const_spec((1, d)),
            _const_spec((d, D_FF)),
            _const_spec((D_FF, d)),
        ],
        out_specs=pl.BlockSpec((ROW_TILE, d), lambda i: (i, 0)),
        out_shape=jax.ShapeDtypeStruct((t_rows, d), F32),
        compiler_params=_cparams("parallel"),
        name="outproj_mlp",
    )(x2d, a2d, w_out, gain.reshape(1, d), w_up, w_down)


SB_TQ = 1024
SB_TK = 256
SB_UNDERFLOW_LOG = -104.0


def _sb_kernel(q_ref, k_ref, v_ref, o_ref):
    i = pl.program_id(2)
    tq, tk = SB_TQ, SB_TK
    own_blocks = tq // tk
    suffix_ones = (lax.broadcasted_iota(jnp.int32, (tk, tk), 0)
                   > lax.broadcasted_iota(jnp.int32, (tk, tk), 1)).astype(BF16)

    def block(j, q, carry, diagonal):
        o, later = carry
        start = pl.multiple_of(j * tk, tk)
        kj = k_ref[0, pl.ds(start, tk), :]
        vj = v_ref[0, pl.ds(start, tk), :]
        z = _dot_nt(q, kj)
        softplus = jnp.maximum(z, 0.0) + jnp.log(1.0 + jnp.exp(-jnp.abs(z)))
        log_om = -softplus
        if diagonal:
            below = (lax.broadcasted_iota(jnp.int32, z.shape, 0)
                     > lax.broadcasted_iota(jnp.int32, z.shape, 1))
            log_om = jnp.where(below, log_om, 0.0)
        hi, lo = _split_bf16(log_om)
        between = _dot(hi, suffix_ones) + _dot(lo, suffix_ones) + later
        a = jnp.exp(z - softplus + between)
        if diagonal:
            a = jnp.where(below, a, 0.0)
        o = o + _dot(a.astype(BF16), vj)
        later = later + jnp.sum(log_om, axis=1, keepdims=True)
        return o, later

    o = jnp.zeros((tq, HEAD_DIM), F32)
    later = jnp.zeros((tq, 1), F32)
    for b in reversed(range(own_blocks)):
        r0 = b * tk
        o_b, later_b = block(i * own_blocks + b, q_ref[0, r0:, :], (o[r0:], later[r0:]), True)
        if b > 0:
            o = jnp.concatenate([o[:r0], o_b], axis=0)
            later = jnp.concatenate([later[:r0], later_b], axis=0)
        else:
            o, later = o_b, later_b

    def more(state):
        j, _, later = state
        return (j >= 0) & (jnp.max(later) > SB_UNDERFLOW_LOG)

    def step(state):
        j, o, later = state
        o, later = block(j, q_ref[0], (o, later), False)
        return j - 1, o, later

    _, o, _ = lax.while_loop(more, step, (i * own_blocks - 1, o, later))
    o_ref[0] = o.astype(o_ref.dtype)


def _sb_attention(qkv):
    b, s, _ = qkv.shape
    h = N_HEADS
    return pl.pallas_call(
        _sb_kernel,
        grid=(b, h, s // SB_TQ),
        in_specs=[
            pl.BlockSpec((1, SB_TQ, LANES), lambda bi, hi, i: (bi, i, hi)),
            pl.BlockSpec((1, s, LANES), lambda bi, hi, i: (bi, 0, h + hi)),
            pl.BlockSpec((1, s, LANES), lambda bi, hi, i: (bi, 0, 2 * h + hi)),
        ],
        out_specs=pl.BlockSpec((1, SB_TQ, LANES), lambda bi, hi, i: (bi, i, hi)),
        out_shape=jax.ShapeDtypeStruct((b, s, h * HEAD_DIM), BF16),
        compiler_params=_cparams("parallel", "parallel", "arbitrary"),
        name="stick_breaking_attn",
    )(qkv, qkv, qkv)


MOBA_TQ = 2048
MOBA_TK = 1024
ONES_ROWS = 16


def _online_softmax_step_t(carry, s_t, v_t):
    m, acc = carry
    m_new = jnp.maximum(m, jnp.max(s_t, axis=0, keepdims=True))
    p_t = jnp.exp2(s_t - m_new)
    acc = jnp.exp2(m - m_new) * acc + _dot(v_t, p_t.astype(BF16))
    return m_new, acc


def _flash_loop_t(n_steps, q_aug, keys_fn, values_t_fn, carry):
    def body(n, carry):
        return _online_softmax_step_t(carry, _dot_nt(keys_fn(n), q_aug), values_t_fn(n))

    return lax.fori_loop(0, n_steps, body, carry)


def _transposed_values_with_ones(v):
    v_t = v.astype(F32).T.astype(BF16)
    return jnp.concatenate([v_t, jnp.ones((ONES_ROWS, v.shape[0]), BF16)], axis=0)


def _moba_kernel(q_ref, k_ref, v_ref, o_ref, kaug_ref, kmean_ref, vt_ref, *, n_blk):
    i = pl.program_id(2)
    t, bs = MOBA_TQ, MOBA_BLOCK
    s_len = k_ref.shape[1]

    @pl.when(i == 0)
    def _():
        k = k_ref[0]
        kaug_ref[:, :LANES] = k
        blk = lax.broadcasted_iota(jnp.int32, (s_len, LANES), 0) // bs
        lane = lax.broadcasted_iota(jnp.int32, (s_len, LANES), 1)
        kaug_ref[:, LANES:] = (blk == lane).astype(BF16)
        kmean_ref[...] = jnp.mean(k.astype(F32).reshape(n_blk, bs, LANES), axis=1)
        for c in range(s_len // MOBA_TK):
            vt_ref[c] = _transposed_values_with_ones(v_ref[0, c * MOBA_TK:(c + 1) * MOBA_TK, :])

    q = q_ref[0]
    km_hi, km_lo = _split_bf16(kmean_ref[...])
    gate = _dot_nt(km_hi, q) + _dot_nt(km_lo, q)
    blk_id = lax.broadcasted_iota(jnp.int32, (n_blk, t), 0)
    cur = i * (t // bs) + lax.broadcasted_iota(jnp.int32, (n_blk, t), 1) // bs
    past = blk_id < cur
    gate = jnp.where(past, gate, -jnp.inf)
    rank = jnp.zeros((n_blk, t), jnp.int32)
    for m in range(n_blk):
        gm = gate[m:m + 1, :]
        beats = (gm > gate) | ((gm == gate) & (m < blk_id))
        rank = rank + beats.astype(jnp.int32)
    visible = ((rank < MOBA_TOPK) & past) | (blk_id == cur)
    sel_bias = jnp.where(visible, 0.0, NEG_INF)
    sel_bias = jnp.concatenate([sel_bias, jnp.zeros((LANES - n_blk, t), F32)], axis=0)
    q_aug = jnp.concatenate([q, sel_bias.T.astype(BF16)], axis=1)

    tk = MOBA_TK

    def keys(n):
        return kaug_ref[pl.ds(pl.multiple_of(n * tk, tk), tk), :]

    def values_t(n):
        return vt_ref[n]

    own = i * (t // tk)
    key = lax.broadcasted_iota(jnp.int32, (tk, t), 0)
    query = i * t + lax.broadcasted_iota(jnp.int32, (tk, t), 1)

    def causal_scores_t(n):
        return jnp.where(n * tk + key <= query, _dot_nt(keys(n), q_aug), NEG_INF)

    s_t = causal_scores_t(own)
    m0 = jnp.max(s_t, axis=0, keepdims=True)
    carry = (m0, _dot(values_t(own), jnp.exp2(s_t - m0).astype(BF16)))
    for extra in range(1, t // tk):
        carry = _online_softmax_step_t(carry, causal_scores_t(own + extra), values_t(own + extra))

    _, acc = _flash_loop_t(own, q_aug, keys, values_t, carry)
    o_t = acc[:HEAD_DIM] * (1.0 / acc[HEAD_DIM:HEAD_DIM + 1])
    o_ref[0] = o_t.T.astype(o_ref.dtype)


def _moba_attention(qkv):
    b, s, _ = qkv.shape
    h = N_HEADS
    assert s % MOBA_TQ == 0 and MOBA_TQ % MOBA_BLOCK == 0
    n_blk = s // MOBA_BLOCK
    assert n_blk % 8 == 0 and n_blk <= LANES
    return pl.pallas_call(
        functools.partial(_moba_kernel, n_blk=n_blk),
        grid=(b, h, s // MOBA_TQ),
        in_specs=[
            pl.BlockSpec((1, MOBA_TQ, LANES), lambda bi, hi, i: (bi, i, hi)),
            pl.BlockSpec((1, s, LANES), lambda bi, hi, i: (bi, 0, h + hi)),
            pl.BlockSpec((1, s, LANES), lambda bi, hi, i: (bi, 0, 2 * h + hi)),
        ],
        out_specs=pl.BlockSpec((1, MOBA_TQ, LANES), lambda bi, hi, i: (bi, i, hi)),
        out_shape=jax.ShapeDtypeStruct((b, s, h * HEAD_DIM), BF16),
        scratch_shapes=[pltpu.VMEM((s, 2 * LANES), BF16), pltpu.VMEM((n_blk, LANES), F32),
                        pltpu.VMEM((s // MOBA_TK, HEAD_DIM + ONES_ROWS, MOBA_TK), BF16)],
        compiler_params=_cparams("parallel", "parallel", "arbitrary"),
        name="moba_attn",
    )(qkv, qkv, qkv)


NSA_TQ = 512
NSA_TK = 1024
CMP_HALF = NSA_CMP_STRIDE * HEAD_DIM


def _cmp_kernel(xk_ref, xv_ref, pos_ref, w_ref, ok_ref, ov_ref):
    n16 = xk_ref.shape[2]
    for t, (x_ref, o_ref) in enumerate(((xk_ref, ok_ref), (xv_ref, ov_ref))):
        x = x_ref[0, 0].astype(F32)
        first = _dot((x + pos_ref[t, 0:1, :]).astype(BF16), w_ref[t, 0])
        second = _dot((x + pos_ref[t, 1:2, :]).astype(BF16), w_ref[t, 1])
        out = first + pltpu.roll(second, n16 - 1, 0)
        o_ref[0, 0] = (out if t == 0 else out.T).astype(o_ref.dtype)


def _nsa_compress(kc, vc, cmp_pos, w_cmp):
    b, g, n16, _ = kc.shape
    pos = cmp_pos.reshape(2, 2, CMP_HALF).astype(F32)
    w = w_cmp.reshape(2, 2, CMP_HALF, HEAD_DIM).astype(BF16)
    x_spec = pl.BlockSpec((1, 1, n16, CMP_HALF), lambda bi, gi: (bi, gi, 0, 0))
    return pl.pallas_call(
        _cmp_kernel,
        grid=(b, g),
        in_specs=[x_spec, x_spec, _const_spec(pos.shape), _const_spec(w.shape)],
        out_specs=[pl.BlockSpec((1, 1, n16, HEAD_DIM), lambda bi, gi: (bi, gi, 0, 0)),
                   pl.BlockSpec((1, 1, HEAD_DIM, n16), lambda bi, gi: (bi, gi, 0, 0))],
        out_shape=[jax.ShapeDtypeStruct((b, g, n16, HEAD_DIM), BF16),
                   jax.ShapeDtypeStruct((b, g, HEAD_DIM, n16), BF16)],
        compiler_params=_cparams("parallel", "parallel"),
        name="nsa_compress",
    )(kc, vc, pos, w)


def _top_k_mask_t(score_t, k):
    n, q = score_t.shape
    work = score_t
    taken = jnp.zeros((1, q), F32)
    level = jnp.full((1, q), jnp.inf, F32)
    above = jnp.zeros((1, q), F32)
    for _ in range(k):
        best = jnp.max(work, axis=0, keepdims=True)
        hit = work == best
        active = taken < k
        level = jnp.where(active, best, level)
        above = jnp.where(active, taken, above)
        taken = taken + jnp.sum(hit.astype(F32), axis=0, keepdims=True)
        work = jnp.where(hit, -jnp.inf, work)
    tie = score_t == level
    lower = (lax.broadcasted_iota(jnp.int32, (n, n), 1)
             < lax.broadcasted_iota(jnp.int32, (n, n), 0)).astype(BF16)
    ties_before = _dot(lower, tie.astype(BF16))
    return (score_t > level) | (tie & (ties_before < k - above))


def _nsa_kernel(q_ref, kc_ref, vc_ref, ks_ref, vs_ref, kw_ref, vw_ref, gate_ref, ov_ref, o_ref,
                ksaug_ref, vst_ref, vwt_ref, *, n_sel):
    i = pl.program_id(2)
    tq, tk, hg = NSA_TQ, NSA_TK, NSA_HEADS_PER_GROUP
    s_len = ks_ref.shape[1]
    n_cmp_pad = kc_ref.shape[2]
    c0 = i * tq

    @pl.when(i == 0)
    def _():
        ksaug_ref[:, :LANES] = ks_ref[0]
        blk = lax.broadcasted_iota(jnp.int32, (s_len, LANES), 0) // NSA_SEL_BLOCK
        lane = lax.broadcasted_iota(jnp.int32, (s_len, LANES), 1)
        ksaug_ref[:, LANES:] = (blk == lane).astype(BF16)
        for c in range(s_len // tk):
            vst_ref[c] = _transposed_values_with_ones(vs_ref[0, c * tk:(c + 1) * tk, :])
        for c in range(s_len // tq):
            vwt_ref[c] = _transposed_values_with_ones(vw_ref[0, c * tq:(c + 1) * tq, :])

    q_all = q_ref[0]
    q_heads = [q_all[:, h * LANES:(h + 1) * LANES] for h in range(hg)]

    def key_iota(n):
        return lax.broadcasted_iota(jnp.int32, (n, tq), 0)

    def q_pos(n):
        return c0 + lax.broadcasted_iota(jnp.int32, (n, tq), 1)

    kc = kc_ref[0, 0]
    cmp_mask = key_iota(n_cmp_pad) * NSA_CMP_STRIDE + (NSA_CMP_BLOCK - 1) <= q_pos(n_cmp_pad)
    o_c, p_sum = [], jnp.zeros((n_cmp_pad, tq), F32)
    for qh in q_heads:
        z = jnp.where(cmp_mask, _dot_nt(kc, qh), NEG_INF)
        e = jnp.where(cmp_mask, jnp.exp2(z - jnp.max(z, axis=0, keepdims=True)), 0.0)
        l = jnp.sum(e, axis=0, keepdims=True)
        inv = 1.0 / jnp.where(l > 0.0, l, 1.0)
        o_c.append(_dot(vc_ref[0, 0], e.astype(BF16)) * inv)
        p_sum = p_sum + e * inv

    ps_hi, ps_lo = _split_bf16(p_sum)
    imp = _dot(ov_ref[...], ps_hi) + _dot(ov_ref[...], ps_lo)
    blk = key_iota(LANES)
    cur = q_pos(LANES) // NSA_SEL_BLOCK
    forced = (blk == 0) | (blk == cur) | (blk == cur - 1)
    score = jnp.where(blk <= cur, imp + NSA_FORCE_BONUS * forced.astype(F32), -1.0)
    score = jnp.where(blk < n_sel, score, -jnp.inf)
    chosen = _top_k_mask_t(score, min(NSA_SEL_TOPK, n_sel))
    sel_bias = jnp.where(chosen, 0.0, NEG_INF).T.astype(BF16)
    q_aug = jnp.concatenate([jnp.concatenate(q_heads, axis=0),
                             jnp.concatenate([sel_bias] * hg, axis=0)], axis=1)

    def keys(j):
        return ksaug_ref[pl.ds(pl.multiple_of(j * tk, tk), tk), :]

    def values_t(j):
        return vst_ref[j]

    jd = c0 // tk
    causal = jd * tk + key_iota(tk) <= q_pos(tk)
    s_t = jnp.where(jnp.concatenate([causal] * hg, axis=1), _dot_nt(keys(jd), q_aug), NEG_INF)
    m0 = jnp.max(s_t, axis=0, keepdims=True)
    carry = (m0, _dot(values_t(jd), jnp.exp2(s_t - m0).astype(BF16)))
    _, acc = _flash_loop_t(jd, q_aug, keys, values_t, carry)
    o_s = acc[:HEAD_DIM] * (1.0 / acc[HEAD_DIM:HEAD_DIM + 1])

    span = tq + NSA_WINDOW
    wstart = pl.multiple_of(jnp.maximum(c0 - NSA_WINDOW, 0), tq)
    w_chunk = wstart // tq
    gap = q_pos(span) - (wstart + key_iota(span))
    win_mask = (gap >= 0) & (gap < NSA_WINDOW)
    kw = kw_ref[0, pl.ds(wstart, span), :]
    o_w = []
    for qh in q_heads:
        z = jnp.where(win_mask, _dot_nt(kw, qh), NEG_INF)
        e = jnp.exp2(z - jnp.max(z, axis=0, keepdims=True)).astype(BF16)
        acc_w = _dot(vwt_ref[w_chunk], e[0:tq])
        for c in range(1, span // tq):
            acc_w = acc_w + _dot(vwt_ref[w_chunk + c], e[c * tq:(c + 1) * tq])
        o_w.append(acc_w[:HEAD_DIM] * (1.0 / acc_w[HEAD_DIM:HEAD_DIM + 1]))

    gates_t = (1.0 / (1.0 + jnp.exp(-gate_ref[0]))).T
    for h in range(hg):
        branches = (o_c[h], o_s[:, h * tq:(h + 1) * tq], o_w[h])
        out = jnp.zeros((HEAD_DIM, tq), F32)
        for branch, o_b in enumerate(branches):
            r = branch * hg + h
            out = out + gates_t[r:r + 1, :] * o_b
        o_ref[0, :, h * LANES:(h + 1) * LANES] = out.T.astype(o_ref.dtype)


def _nsa_attention(main, gate_logits, kc_cmp, vc_cmp_t, overlap_t):
    b, s, _ = main.shape
    g, hg = NSA_KV_GROUPS, NSA_HEADS_PER_GROUP
    n_sel = s // NSA_SEL_BLOCK
    assert s % NSA_TK == 0 and n_sel <= LANES
    n16 = kc_cmp.shape[2]
    q_blocks = N_HEADS

    def kv_spec(which):
        return pl.BlockSpec((1, s, LANES), lambda bi, gi, i: (bi, 0, q_blocks + which * g + gi),
                            pipeline_mode=pl.Buffered(1))

    return pl.pallas_call(
        functools.partial(_nsa_kernel, n_sel=n_sel),
        grid=(b, g, s // NSA_TQ),
        in_specs=[
            pl.BlockSpec((1, NSA_TQ, hg * LANES), lambda bi, gi, i: (bi, i, gi)),
            pl.BlockSpec((1, 1, n16, HEAD_DIM), lambda bi, gi, i: (bi, gi, 0, 0)),
            pl.BlockSpec((1, 1, HEAD_DIM, n16), lambda bi, gi, i: (bi, gi, 0, 0)),
            kv_spec(2), kv_spec(3), kv_spec(4), kv_spec(5),
            pl.BlockSpec((1, NSA_TQ, LANES), lambda bi, gi, i: (bi, i, gi)),
            _const_spec(overlap_t.shape),
        ],
        out_specs=pl.BlockSpec((1, NSA_TQ, hg * LANES), lambda bi, gi, i: (bi, i, gi)),
        out_shape=jax.ShapeDtypeStruct((b, s, N_HEADS * HEAD_DIM), BF16),
        scratch_shapes=[pltpu.VMEM((s, 2 * LANES), BF16),
                        pltpu.VMEM((s // NSA_TK, HEAD_DIM + ONES_ROWS, NSA_TK), BF16),
                        pltpu.VMEM((s // NSA_TQ, HEAD_DIM + ONES_ROWS, NSA_TQ), BF16)],
        compiler_params=_cparams("parallel", "parallel", "arbitrary"),
        name="nsa_attn",
    )(main, kc_cmp, vc_cmp_t, main, main, main, main, gate_logits, overlap_t)


def _nsa_overlap_t(seq):
    n_cmp = (seq - NSA_CMP_BLOCK) // NSA_CMP_STRIDE + 1
    n_sel = seq // NSA_SEL_BLOCK
    cmp_start = np.arange(seq // NSA_CMP_STRIDE) * NSA_CMP_STRIDE
    sel_start = np.arange(LANES) * NSA_SEL_BLOCK
    ov = ((cmp_start[:, None] < sel_start[None, :] + NSA_SEL_BLOCK)
          & (cmp_start[:, None] + NSA_CMP_BLOCK > sel_start[None, :]))
    ov &= (np.arange(seq // NSA_CMP_STRIDE)[:, None] < n_cmp) & (np.arange(LANES)[None, :] < n_sel)
    return jnp.asarray(ov.T, dtype=BF16)


def _glu_proj_kernel(x_ref, g_ref, w_ref, o_ref):
    h = _rms(x_ref[...], g_ref[...]).astype(BF16)
    d = o_ref.shape[1]
    for c in range(d // PROJ_CHUNK):
        a = _dot(h, w_ref[:, c * PROJ_CHUNK:(c + 1) * PROJ_CHUNK])
        gate = _dot(h, w_ref[:, d + c * PROJ_CHUNK:d + (c + 1) * PROJ_CHUNK])
        o_ref[:, c * PROJ_CHUNK:(c + 1) * PROJ_CHUNK] = (a / (1.0 + jnp.exp(-gate))).astype(o_ref.dtype)


def _glu_project(x2d, gain, w):
    t_rows, d = x2d.shape
    return pl.pallas_call(
        _glu_proj_kernel,
        grid=(t_rows // ROW_TILE,),
        in_specs=[pl.BlockSpec((ROW_TILE, d), lambda i: (i, 0)), _const_spec((1, d)), _const_spec(w.shape)],
        out_specs=pl.BlockSpec((ROW_TILE, d), lambda i: (i, 0)),
        out_shape=jax.ShapeDtypeStruct((t_rows, d), BF16),
        compiler_params=_cparams("parallel"),
        name="norm_glu_proj",
    )(x2d, gain.reshape(1, d), w)


CONV_TILE = 256
SUBLANES = 8
CONV_PHASE_ROWS = CONV_TILE + CONV_HALO - SUBLANES


def _conv_kernel(u_ref, halo_ref, dw_ref, db_ref, lg_ref, lb_ref, o_ref, ext_ref, phase_ref):
    i = pl.program_id(1)
    halo = halo_ref[0].astype(F32)
    ext_ref[0:CONV_HALO, :] = jnp.where(i == 0, 0.0, halo)
    ext_ref[CONV_HALO:, :] = u_ref[0].astype(F32)
    for b in range(1, SUBLANES):
        phase_ref[b - 1] = ext_ref[pl.ds(b, CONV_PHASE_ROWS), :]
    lead = CONV_HALO - (CONV_WIDTH - 1)
    for r in range(CONV_TILE // CONV_ROWS):
        acc = jnp.zeros((CONV_ROWS, D_MODEL), F32) + db_ref[...]
        for w in range(CONV_WIDTH):
            shift = (lead + w) % SUBLANES
            start = r * CONV_ROWS + lead + w - shift
            src = ext_ref if shift == 0 else phase_ref.at[shift - 1]
            acc = acc + dw_ref[w:w + 1, :] * src[pl.ds(start, CONV_ROWS), :]
        mu = jnp.mean(acc, axis=-1, keepdims=True)
        cen = acc - mu
        var = jnp.mean(cen * cen, axis=-1, keepdims=True)
        un = cen * lax.rsqrt(var + NORM_EPS) * lg_ref[...] + lb_ref[...]
        o_ref[0, r * CONV_ROWS:(r + 1) * CONV_ROWS, :] = (un / (1.0 + jnp.exp(-un))).astype(o_ref.dtype)


def _conv_ln_swish(u, dw_w, dw_b, ln_g, ln_b):
    b, s, d = u.shape
    halo_per_tile = CONV_TILE // CONV_HALO
    dw = jnp.concatenate([dw_w, jnp.zeros((1, d), F32)], axis=0)
    return pl.pallas_call(
        _conv_kernel,
        grid=(b, s // CONV_TILE),
        in_specs=[
            pl.BlockSpec((1, CONV_TILE, d), lambda bi, i: (bi, i, 0)),
            pl.BlockSpec((1, CONV_HALO, d), lambda bi, i: (bi, jnp.maximum(i * halo_per_tile - 1, 0), 0)),
            _const_spec(dw.shape), _const_spec((1, d)), _const_spec((1, d)), _const_spec((1, d)),
        ],
        out_specs=pl.BlockSpec((1, CONV_TILE, d), lambda bi, i: (bi, i, 0)),
        out_shape=jax.ShapeDtypeStruct((b, s, d), BF16),
        scratch_shapes=[pltpu.VMEM((CONV_TILE + CONV_HALO, d), F32),
                        pltpu.VMEM((SUBLANES - 1, CONV_PHASE_ROWS, d), F32)],
        compiler_params=_cparams("parallel", "parallel"),
        name="conv_ln_swish",
    )(u, u, dw, dw_b.reshape(1, d), ln_g.reshape(1, d), ln_b.reshape(1, d))


def _rope_tables(seq):
    half = HEAD_DIM // 2
    inv_freq = ROPE_THETA ** (-jnp.arange(half, dtype=F32) / half)
    ang = jnp.arange(seq, dtype=F32)[:, None] * inv_freq[None, :]
    cos, sin = jnp.cos(ang), jnp.sin(ang)
    return jnp.concatenate([cos, cos], axis=1), jnp.concatenate([-sin, sin], axis=1)


def _nsa_mixer(x2d, norm_gain, w_in, q_gain, k_gain, cmp_pos, w_cmp, tables, batch, seq):
    g, dh = NSA_KV_GROUPS, HEAD_DIM
    n_main = (N_HEADS + 6 * g) * dh
    scale = dh ** -0.5 * LOG2_E
    w_main = w_in[:, :n_main].astype(BF16)
    hg = NSA_HEADS_PER_GROUP
    w_g = w_in[:, n_main:].reshape(-1, 3, g, hg).transpose(0, 2, 1, 3).reshape(-1, g, 3 * hg)
    w_gate = jnp.pad(w_g, ((0, 0), (0, 0), (0, LANES - 3 * hg))).reshape(-1, g * LANES).astype(BF16)
    ones = jnp.ones((dh,), F32)
    head_gains = jnp.stack([q_gain * scale] * N_HEADS + [k_gain[0]] * g + [ones] * g
                           + [k_gain[1]] * g + [ones] * g + [k_gain[2]] * g + [ones] * g)
    rope_blocks = [True] * N_HEADS + [True] * g + [False] * g + [True] * g + [False] * g + [True] * g + [False] * g
    main, gate_logits = _project(x2d, norm_gain, w_main, *tables, head_gains, rope_blocks, seq, w_gate=w_gate)
    main = main.reshape(batch, seq, n_main)
    gate_logits = gate_logits.reshape(batch, seq, g * LANES)

    def cmp_layout(col0):
        t = main[:, :, col0:col0 + g * dh].reshape(batch, seq // NSA_CMP_STRIDE, NSA_CMP_STRIDE, g, dh)
        return t.transpose(0, 3, 1, 2, 4).reshape(batch, g, seq // NSA_CMP_STRIDE, CMP_HALF)

    kc_cmp, vc_cmp_t = _nsa_compress(cmp_layout(N_HEADS * dh), cmp_layout((N_HEADS + g) * dh), cmp_pos, w_cmp)
    out = _nsa_attention(main, gate_logits, kc_cmp, vc_cmp_t, _nsa_overlap_t(seq))
    return out.reshape(batch * seq, N_HEADS * dh)


def _qkv_weight_with_scaled_q(w_in):
    n_q = N_HEADS * HEAD_DIM
    return jnp.concatenate([w_in[:, :n_q] * HEAD_DIM ** -0.5, w_in[:, n_q:]], axis=1).astype(BF16)


def _sb_mixer(x2d, norm_gain, w_in, tables, batch, seq):
    n = 3 * N_HEADS * HEAD_DIM
    head_gains = jnp.ones((n // LANES, HEAD_DIM), F32)
    qkv = _project(x2d, norm_gain, _qkv_weight_with_scaled_q(w_in), *tables, head_gains,
                   [False] * (n // LANES), seq)
    return _sb_attention(qkv.reshape(batch, seq, n)).reshape(batch * seq, N_HEADS * HEAD_DIM)


def _conv_mixer(x2d, norm_gain, w_in, dw_w, dw_b, ln_g, ln_b, batch, seq):
    u = _glu_project(x2d, norm_gain, w_in.astype(BF16))
    a = _conv_ln_swish(u.reshape(batch, seq, D_MODEL), dw_w, dw_b, ln_g, ln_b)
    return a.reshape(batch * seq, D_MODEL)


def _moba_mixer(x2d, norm_gain, w_in, q_gain, k_gain, tables, batch, seq):
    n = 3 * N_HEADS * HEAD_DIM
    ones = jnp.ones((HEAD_DIM,), F32)
    head_gains = jnp.stack([q_gain * (HEAD_DIM ** -0.5 * LOG2_E)] * N_HEADS + [k_gain] * N_HEADS + [ones] * N_HEADS)
    rope_blocks = [True] * (2 * N_HEADS) + [False] * N_HEADS
    qkv = _project(x2d, norm_gain, w_in.astype(BF16), *tables, head_gains, rope_blocks, seq)
    return _moba_attention(qkv.reshape(batch, seq, n)).reshape(batch * seq, N_HEADS * HEAD_DIM)


def kernel(x, attn_norm, mlp_norm, mlp_w_up, mlp_w_down, nsa_w_in, nsa_q_norm, nsa_k_norm, nsa_cmp_pos, nsa_w_cmp, nsa_w_out, sb_w_in, sb_w_out, conv_w_in, conv_dw_w, conv_dw_b, conv_ln_g, conv_ln_b, conv_w_out, moba_w_in, moba_q_norm, moba_k_norm, moba_w_out):
    batch, seq, d = x.shape
    depth = attn_norm.shape[0]
    tables = _rope_tables(seq)
    x2d = x.reshape(batch * seq, d)
    for i in range(depth):
        m, j = i % 4, i // 4
        if m == 0:
            a = _nsa_mixer(x2d, attn_norm[i], nsa_w_in[j], nsa_q_norm[j], nsa_k_norm[j], nsa_cmp_pos[j],
                           nsa_w_cmp[j], tables, batch, seq)
            w_out = nsa_w_out[j]
        elif m == 1:
            a = _sb_mixer(x2d, attn_norm[i], sb_w_in[j], tables, batch, seq)
            w_out = sb_w_out[j]
        elif m == 2:
            a = _conv_mixer(x2d, attn_norm[i], conv_w_in[j], conv_dw_w[j], conv_dw_b[j], conv_ln_g[j],
                            conv_ln_b[j], batch, seq)
            w_out = conv_w_out[j]
        else:
            a = _moba_mixer(x2d, attn_norm[i], moba_w_in[j], moba_q_norm[j], moba_k_norm[j], tables, batch, seq)
            w_out = moba_w_out[j]
        x2d = _mixer_out_and_mlp(x2d, a, w_out.astype(BF16), mlp_norm[i], mlp_w_up[i].astype(BF16),
                                 mlp_w_down[i].astype(BF16))
    return x2d.reshape(batch, seq, d)
```

```python
import functools

import numpy as np
import jax
import jax.numpy as jnp
from jax import lax
from jax.experimental import pallas as pl
from jax.experimental.pallas import tpu as pltpu

F32 = jnp.float32
BF16 = jnp.bfloat16

D_MODEL = 1024
N_HEADS = 8
HEAD_DIM = 128
D_FF = 4 * D_MODEL
ROPE_THETA = 10000.0
NORM_EPS = 1e-6
NEG_INF = -1e30
LOG2_E = 1.4426950408889634

NSA_KV_GROUPS = 2
NSA_HEADS_PER_GROUP = N_HEADS // NSA_KV_GROUPS
NSA_CMP_BLOCK = 32
NSA_CMP_STRIDE = 16
NSA_SEL_BLOCK = 64
NSA_SEL_TOPK = 16
NSA_WINDOW = 512
NSA_FORCE_BONUS = 1000.0

CONV_WIDTH = 31
MOBA_BLOCK = 256
MOBA_TOPK = 3

LANES = 128
VMEM_LIMIT_BYTES = 56 * 1024 * 1024
ROW_TILE = 512
PROJ_CHUNK = 512
FF_CHUNK = 1024
CONV_HALO = 32
CONV_ROWS = 32


def _cparams(*sem):
    return pltpu.CompilerParams(dimension_semantics=sem, vmem_limit_bytes=VMEM_LIMIT_BYTES)


def _const_spec(shape):
    zeros = (0,) * len(shape)
    return pl.BlockSpec(shape, lambda *_: zeros, pipeline_mode=pl.Buffered(1))


def _rms(x, gain):
    return x * lax.rsqrt(jnp.mean(x * x, axis=-1, keepdims=True) + NORM_EPS) * gain


def _dot(a, b):
    return jnp.dot(a, b, preferred_element_type=F32)


def _dot_nt(a, b):
    return lax.dot_general(a, b, (((1,), (1,)), ((), ())), preferred_element_type=F32)


def _split_bf16(x):
    hi = x.astype(BF16)
    lo = (x - hi.astype(F32)).astype(BF16)
    return hi, lo


def _proj_kernel(*refs, rope_blocks, has_gate):
    if has_gate:
        x_ref, g_ref, w_ref, cos_ref, sin_ref, hg_ref, wg_ref, o_ref, og_ref = refs
    else:
        x_ref, g_ref, w_ref, cos_ref, sin_ref, hg_ref, o_ref = refs
    h = _rms(x_ref[...], g_ref[...]).astype(BF16)
    blocks_per_chunk = PROJ_CHUNK // LANES
    for c in range(len(rope_blocks) // blocks_per_chunk):
        y = _dot(h, w_ref[:, c * PROJ_CHUNK:(c + 1) * PROJ_CHUNK])
        chunk_flags = rope_blocks[c * blocks_per_chunk:(c + 1) * blocks_per_chunk]
        if not any(chunk_flags):
            o_ref[:, c * PROJ_CHUNK:(c + 1) * PROJ_CHUNK] = y.astype(o_ref.dtype)
            continue
        for k, flag in enumerate(chunk_flags):
            b = c * blocks_per_chunk + k
            yb = y[:, k * LANES:(k + 1) * LANES]
            if flag:
                yb = _rms(yb, hg_ref[b:b + 1, :])
                yb = yb * cos_ref[...] + pltpu.roll(yb, HEAD_DIM // 2, 1) * sin_ref[...]
            o_ref[:, b * LANES:(b + 1) * LANES] = yb.astype(o_ref.dtype)
    if has_gate:
        og_ref[...] = _dot(h, wg_ref[...])


def _project(x2d, gain, w, cos_t, sin_t, head_gains, rope_blocks, seq, w_gate=None):
    t_rows, d = x2d.shape
    n = w.shape[1]
    assert n % PROJ_CHUNK == 0 and len(rope_blocks) == n // LANES and seq % ROW_TILE == 0
    seq_tiles = seq // ROW_TILE
    has_gate = w_gate is not None
    in_specs = [
        pl.BlockSpec((ROW_TILE, d), lambda i: (i, 0)),
        _const_spec((1, d)),
        _const_spec((d, n)),
        pl.BlockSpec((ROW_TILE, LANES), lambda i: (i % seq_tiles, 0)),
        pl.BlockSpec((ROW_TILE, LANES), lambda i: (i % seq_tiles, 0)),
        _const_spec(head_gains.shape),
    ]
    args = [x2d, gain.reshape(1, d), w, cos_t, sin_t, head_gains]
    out_shape = [jax.ShapeDtypeStruct((t_rows, n), BF16)]
    out_specs = [pl.BlockSpec((ROW_TILE, n), lambda i: (i, 0))]
    if has_gate:
        in_specs.append(_const_spec(w_gate.shape))
        args.append(w_gate)
        out_shape.append(jax.ShapeDtypeStruct((t_rows, w_gate.shape[1]), F32))
        out_specs.append(pl.BlockSpec((ROW_TILE, w_gate.shape[1]), lambda i: (i, 0)))
    outs = pl.pallas_call(
        functools.partial(_proj_kernel, rope_blocks=tuple(rope_blocks), has_gate=has_gate),
        grid=(t_rows // ROW_TILE,),
        in_specs=in_specs,
        out_specs=out_specs,
        out_shape=out_shape,
        compiler_params=_cparams("parallel"),
        name="norm_proj",
    )(*args)
    return outs if has_gate else outs[0]


def _mlp_kernel(x_ref, a_ref, wo_ref, g_ref, wup_ref, wdn_ref, o_ref):
    x1 = x_ref[...] + _dot(a_ref[...], wo_ref[...])
    h = _rms(x1, g_ref[...]).astype(BF16)
    acc = x1
    for c in range(D_FF // FF_CHUNK):
        u = _dot(h, wup_ref[:, c * FF_CHUNK:(c + 1) * FF_CHUNK])
        act = jnp.square(jnp.maximum(u, 0.0)).astype(BF16)
        acc = acc + _dot(act, wdn_ref[c * FF_CHUNK:(c + 1) * FF_CHUNK, :])
    o_ref[...] = acc


def _mixer_out_and_mlp(x2d, a2d, w_out, gain, w_up, w_down):
    t_rows, d = x2d.shape
    return pl.pallas_call(
        _mlp_kernel,
        grid=(t_rows // ROW_TILE,),
        in_specs=[
            pl.BlockSpec((ROW_TILE, d), lambda i: (i, 0)),
            pl.BlockSpec((ROW_TILE, d), lambda i: (i, 0)),
            _const_spec((d, d)),
            _const_spec((1, d)),
            _const_spec((d, D_FF)),
            _const_spec((D_FF, d)),
        ],
        out_specs=pl.BlockSpec((ROW_TILE, d), lambda i: (i, 0)),
        out_shape=jax.ShapeDtypeStruct((t_rows, d), F32),
        compiler_params=_cparams("parallel"),
        name="outproj_mlp",
    )(x2d, a2d, w_out, gain.reshape(1, d), w_up, w_down)


SB_TQ = 1024
SB_TK = 256
SB_UNDERFLOW_LOG = -104.0


def _sb_kernel(q_ref, k_ref, v_ref, o_ref):
    i = pl.program_id(2)
    tq, tk = SB_TQ, SB_TK
    own_blocks = tq // tk
    suffix_ones = (lax.broadcasted_iota(jnp.int32, (tk, tk), 0)
                   > lax.broadcasted_iota(jnp.int32, (tk, tk), 1)).astype(BF16)

    def block(j, q, carry, diagonal):
        o, later = carry
        start = pl.multiple_of(j * tk, tk)
        kj = k_ref[0, pl.ds(start, tk), :]
        vj = v_ref[0, pl.ds(start, tk), :]
        z = _dot_nt(q, kj)
        softplus = jnp.maximum(z, 0.0) + jnp.log(1.0 + jnp.exp(-jnp.abs(z)))
        log_om = -softplus
        if diagonal:
            below = (lax.broadcasted_iota(jnp.int32, z.shape, 0)
                     > lax.broadcasted_iota(jnp.int32, z.shape, 1))
            log_om = jnp.where(below, log_om, 0.0)
        hi, lo = _split_bf16(log_om)
        between = _dot(hi, suffix_ones) + _dot(lo, suffix_ones) + later
        a = jnp.exp(z - softplus + between)
        if diagonal:
            a = jnp.where(below, a, 0.0)
        o = o + _dot(a.astype(BF16), vj)
        later = later + jnp.sum(log_om, axis=1, keepdims=True)
        return o, later

    o = jnp.zeros((tq, HEAD_DIM), F32)
    later = jnp.zeros((tq, 1), F32)
    for b in reversed(range(own_blocks)):
        r0 = b * tk
        o_b, later_b = block(i * own_blocks + b, q_ref[0, r0:, :], (o[r0:], later[r0:]), True)
        if b > 0:
            o = jnp.concatenate([o[:r0], o_b], axis=0)
            later = jnp.concatenate([later[:r0], later_b], axis=0)
        else:
            o, later = o_b, later_b

    def more(state):
        j, _, later = state
        return (j >= 0) & (jnp.max(later) > SB_UNDERFLOW_LOG)

    def step(state):
        j, o, later = state
        o, later = block(j, q_ref[0], (o, later), False)
        return j - 1, o, later

    _, o, _ = lax.while_loop(more, step, (i * own_blocks - 1, o, later))
    o_ref[0] = o.astype(o_ref.dtype)


def _sb_attention(qkv):
    b, s, _ = qkv.shape
    h = N_HEADS
    return pl.pallas_call(
        _sb_kernel,
        grid=(b, h, s // SB_TQ),
        in_specs=[
            pl.BlockSpec((1, SB_TQ, LANES), lambda bi, hi, i: (bi, i, hi)),
            pl.BlockSpec((1, s, LANES), lambda bi, hi, i: (bi, 0, h + hi)),
            pl.BlockSpec((1, s, LANES), lambda bi, hi, i: (bi, 0, 2 * h + hi)),
        ],
        out_specs=pl.BlockSpec((1, SB_TQ, LANES), lambda bi, hi, i: (bi, i, hi)),
        out_shape=jax.ShapeDtypeStruct((b, s, h * HEAD_DIM), BF16),
        compiler_params=_cparams("parallel", "parallel", "arbitrary"),
        name="stick_breaking_attn",
    )(qkv, qkv, qkv)


MOBA_TQ = 2048
MOBA_TK = 1024
MOBA_OWN_TK = 512
ONES_ROWS = 16


def _online_softmax_step_t(carry, s_t, v_t):
    m, acc = carry
    m_new = jnp.maximum(m, jnp.max(s_t, axis=0, keepdims=True))
    p_t = jnp.exp2(s_t - m_new)
    acc = jnp.exp2(m - m_new) * acc + _dot(v_t, p_t.astype(BF16))
    return m_new, acc


def _flash_loop_t(n_steps, q_aug, keys_fn, values_t_fn, carry):
    def body(n, carry):
        return _online_softmax_step_t(carry, _dot_nt(keys_fn(n), q_aug), values_t_fn(n))

    return lax.fori_loop(0, n_steps, body, carry)


def _transposed_values_with_ones(v):
    v_t = v.astype(F32).T.astype(BF16)
    return jnp.concatenate([v_t, jnp.ones((ONES_ROWS, v.shape[0]), BF16)], axis=0)


def _moba_kernel(q_ref, k_ref, v_ref, o_ref, kaug_ref, kmean_ref, vt_ref, *, n_blk):
    i = pl.program_id(2)
    t, bs = MOBA_TQ, MOBA_BLOCK
    s_len = k_ref.shape[1]

    @pl.when(i == 0)
    def _():
        k = k_ref[0]
        kaug_ref[:, :LANES] = k
        blk = lax.broadcasted_iota(jnp.int32, (s_len, LANES), 0) // bs
        lane = lax.broadcasted_iota(jnp.int32, (s_len, LANES), 1)
        kaug_ref[:, LANES:] = (blk == lane).astype(BF16)
        kmean_ref[...] = jnp.mean(k.astype(F32).reshape(n_blk, bs, LANES), axis=1)
        for c in range(s_len // MOBA_TK):
            vt_ref[c] = _transposed_values_with_ones(v_ref[0, c * MOBA_TK:(c + 1) * MOBA_TK, :])

    q = q_ref[0]
    km_hi, km_lo = _split_bf16(kmean_ref[...])
    gate = _dot_nt(km_hi, q) + _dot_nt(km_lo, q)
    blk_id = lax.broadcasted_iota(jnp.int32, (n_blk, t), 0)
    cur = i * (t // bs) + lax.broadcasted_iota(jnp.int32, (n_blk, t), 1) // bs
    past = blk_id < cur
    gate = jnp.where(past, gate, -jnp.inf)
    visible = (_top_k_mask_t(gate, MOBA_TOPK) & past) | (blk_id == cur)
    sel_bias = jnp.where(visible, 0.0, NEG_INF)
    sel_bias = jnp.concatenate([sel_bias, jnp.zeros((LANES - n_blk, t), F32)], axis=0)
    q_aug = jnp.concatenate([q, sel_bias.T.astype(BF16)], axis=1)

    tk = MOBA_TK

    def keys(n):
        return kaug_ref[pl.ds(pl.multiple_of(n * tk, tk), tk), :]

    def values_t(n):
        return vt_ref[n]

    own = i * (t // tk)
    to = MOBA_OWN_TK
    causal = (lax.broadcasted_iota(jnp.int32, (to, t), 0) <= lax.broadcasted_iota(jnp.int32, (to, t), 1))
    carry = None
    for u in range(t // to):
        r0 = u * to
        k_u = kaug_ref[pl.ds(pl.multiple_of(i * t + r0, to), to), :]
        v_u = vt_ref[own + r0 // tk][:, r0 % tk:r0 % tk + to]
        s_t = jnp.where(causal[:, :t - r0], _dot_nt(k_u, q_aug[r0:]), NEG_INF)
        if carry is None:
            m0 = jnp.max(s_t, axis=0, keepdims=True)
            carry = (m0, _dot(v_u, jnp.exp2(s_t - m0).astype(BF16)))
        else:
            m_u, acc_u = _online_softmax_step_t((carry[0][:, r0:], carry[1][:, r0:]), s_t, v_u)
            carry = (jnp.concatenate([carry[0][:, :r0], m_u], axis=1),
                     jnp.concatenate([carry[1][:, :r0], acc_u], axis=1))

    _, acc = _flash_loop_t(own, q_aug, keys, values_t, carry)
    o_t = acc[:HEAD_DIM] * (1.0 / acc[HEAD_DIM:HEAD_DIM + 1])
    o_ref[0] = o_t.T.astype(o_ref.dtype)


def _moba_attention(qkv):
    b, s, _ = qkv.shape
    h = N_HEADS
    assert s % MOBA_TQ == 0 and MOBA_TQ % MOBA_BLOCK == 0
    n_blk = s // MOBA_BLOCK
    assert n_blk % 8 == 0 and n_blk <= LANES
    return pl.pallas_call(
        functools.partial(_moba_kernel, n_blk=n_blk),
        grid=(b, h, s // MOBA_TQ),
        in_specs=[
            pl.BlockSpec((1, MOBA_TQ, LANES), lambda bi, hi, i: (bi, i, hi)),
            pl.BlockSpec((1, s, LANES), lambda bi, hi, i: (bi, 0, h + hi)),
            pl.BlockSpec((1, s, LANES), lambda bi, hi, i: (bi, 0, 2 * h + hi)),
        ],
        out_specs=pl.BlockSpec((1, MOBA_TQ, LANES), lambda bi, hi, i: (bi, i, hi)),
        out_shape=jax.ShapeDtypeStruct((b, s, h * HEAD_DIM), BF16),
        scratch_shapes=[pltpu.VMEM((s, 2 * LANES), BF16), pltpu.VMEM((n_blk, LANES), F32),
                        pltpu.VMEM((s // MOBA_TK, HEAD_DIM + ONES_ROWS, MOBA_TK), BF16)],
        compiler_params=_cparams("parallel", "parallel", "arbitrary"),
        name="moba_attn",
    )(qkv, qkv, qkv)


NSA_TQ = 512
NSA_TK = 1024
def _cmp_kernel(*refs):
    st = NSA_CMP_STRIDE
    x_refs, (pos_ref, w_ref, ok_ref, ov_ref) = refs[:2 * st], refs[2 * st:]
    n16 = ok_ref.shape[2]
    for t, o_ref in enumerate((ok_ref, ov_ref)):
        first = jnp.zeros((n16, HEAD_DIM), F32)
        second = jnp.zeros((n16, HEAD_DIM), F32)
        for l in range(st):
            x = x_refs[t * st + l][0].astype(F32)
            first = first + _dot((x + pos_ref[t, l:l + 1, :]).astype(BF16), w_ref[t, l])
            second = second + _dot((x + pos_ref[t, st + l:st + l + 1, :]).astype(BF16), w_ref[t, st + l])
        out = first + pltpu.roll(second, n16 - 1, 0)
        o_ref[0, 0] = (out if t == 0 else out.T).astype(o_ref.dtype)


def _nsa_compress(main, cmp_pos, w_cmp):
    b, s, n_main = main.shape
    g, st = NSA_KV_GROUPS, NSA_CMP_STRIDE
    n16 = s // st
    blocks = n_main // LANES
    slabs = main.reshape(b, n16, st * n_main)

    def slab_spec(first_block, l):
        return pl.BlockSpec((1, n16, LANES), lambda bi, gi: (bi, 0, l * blocks + first_block + gi))

    x_specs = [slab_spec(N_HEADS + t * g, l) for t in range(2) for l in range(st)]
    pos = cmp_pos.astype(F32)
    w = w_cmp.astype(BF16)
    return pl.pallas_call(
        _cmp_kernel,
        grid=(b, g),
        in_specs=x_specs + [_const_spec(pos.shape), _const_spec(w.shape)],
        out_specs=[pl.BlockSpec((1, 1, n16, HEAD_DIM), lambda bi, gi: (bi, gi, 0, 0)),
                   pl.BlockSpec((1, 1, HEAD_DIM, n16), lambda bi, gi: (bi, gi, 0, 0))],
        out_shape=[jax.ShapeDtypeStruct((b, g, n16, HEAD_DIM), BF16),
                   jax.ShapeDtypeStruct((b, g, HEAD_DIM, n16), BF16)],
        compiler_params=_cparams("parallel", "parallel"),
        name="nsa_compress",
    )(*([slabs] * (2 * st)), pos, w)


def _top_k_mask_t(score_t, k):
    n, q = score_t.shape
    work = score_t
    taken = jnp.zeros((1, q), F32)
    level = jnp.full((1, q), jnp.inf, F32)
    above = jnp.zeros((1, q), F32)
    for _ in range(k):
        best = jnp.max(work, axis=0, keepdims=True)
        hit = work == best
        active = taken < k
        level = jnp.where(active, best, level)
        above = jnp.where(active, taken, above)
        taken = taken + jnp.sum(hit.astype(F32), axis=0, keepdims=True)
        work = jnp.where(hit, -jnp.inf, work)
    tie = score_t == level
    lower = (lax.broadcasted_iota(jnp.int32, (n, n), 1)
             < lax.broadcasted_iota(jnp.int32, (n, n), 0)).astype(BF16)
    ties_before = _dot(lower, tie.astype(BF16))
    return (score_t > level) | (tie & (ties_before < k - above))


def _nsa_kernel(q_ref, kc_ref, vc_ref, ks_ref, vs_ref, kw_ref, vw_ref, gate_ref, ov_ref, o_ref,
                ksaug_ref, vst_ref, vwt_ref, *, n_sel):
    i = pl.program_id(2)
    tq, tk, hg = NSA_TQ, NSA_TK, NSA_HEADS_PER_GROUP
    s_len = ks_ref.shape[1]
    n_cmp_pad = kc_ref.shape[2]
    c0 = i * tq

    @pl.when(i == 0)
    def _():
        ksaug_ref[:, :LANES] = ks_ref[0]
        blk = lax.broadcasted_iota(jnp.int32, (s_len, LANES), 0) // NSA_SEL_BLOCK
        lane = lax.broadcasted_iota(jnp.int32, (s_len, LANES), 1)
        ksaug_ref[:, LANES:] = (blk == lane).astype(BF16)
        for c in range(s_len // tk):
            vst_ref[c] = _transposed_values_with_ones(vs_ref[0, c * tk:(c + 1) * tk, :])
        for c in range(s_len // tq):
            vwt_ref[c] = _transposed_values_with_ones(vw_ref[0, c * tq:(c + 1) * tq, :])

    q_all = q_ref[0]
    q_heads = [q_all[:, h * LANES:(h + 1) * LANES] for h in range(hg)]

    def key_iota(n):
        return lax.broadcasted_iota(jnp.int32, (n, tq), 0)

    def q_pos(n):
        return c0 + lax.broadcasted_iota(jnp.int32, (n, tq), 1)

    kc = kc_ref[0, 0]
    cmp_mask = key_iota(n_cmp_pad) * NSA_CMP_STRIDE + (NSA_CMP_BLOCK - 1) <= q_pos(n_cmp_pad)
    o_c, p_sum = [], jnp.zeros((n_cmp_pad, tq), F32)
    for qh in q_heads:
        z = jnp.where(cmp_mask, _dot_nt(kc, qh), NEG_INF)
        e = jnp.where(cmp_mask, jnp.exp2(z - jnp.max(z, axis=0, keepdims=True)), 0.0)
        l = jnp.sum(e, axis=0, keepdims=True)
        inv = 1.0 / jnp.where(l > 0.0, l, 1.0)
        o_c.append(_dot(vc_ref[0, 0], e.astype(BF16)) * inv)
        p_sum = p_sum + e * inv

    ps_hi, ps_lo = _split_bf16(p_sum)
    imp = _dot(ov_ref[...], ps_hi) + _dot(ov_ref[...], ps_lo)
    blk = key_iota(LANES)
    cur = q_pos(LANES) // NSA_SEL_BLOCK
    forced = (blk == 0) | (blk == cur) | (blk == cur - 1)
    score = jnp.where(blk <= cur, imp + NSA_FORCE_BONUS * forced.astype(F32), -1.0)
    score = jnp.where(blk < n_sel, score, -jnp.inf)
    chosen = _top_k_mask_t(score, min(NSA_SEL_TOPK, n_sel))
    sel_bias = jnp.where(chosen, 0.0, NEG_INF).T.astype(BF16)
    q_aug = jnp.concatenate([jnp.concatenate(q_heads, axis=0),
                             jnp.concatenate([sel_bias] * hg, axis=0)], axis=1)

    def keys(j):
        return ksaug_ref[pl.ds(pl.multiple_of(j * tk, tk), tk), :]

    def values_t(j):
        return vst_ref[j]

    jd = c0 // tk
    causal = jd * tk + key_iota(tk) <= q_pos(tk)
    s_t = jnp.where(jnp.concatenate([causal] * hg, axis=1), _dot_nt(keys(jd), q_aug), NEG_INF)
    m0 = jnp.max(s_t, axis=0, keepdims=True)
    carry = (m0, _dot(values_t(jd), jnp.exp2(s_t - m0).astype(BF16)))
    _, acc = _flash_loop_t(jd, q_aug, keys, values_t, carry)
    o_s = acc[:HEAD_DIM] * (1.0 / acc[HEAD_DIM:HEAD_DIM + 1])

    span = tq + NSA_WINDOW
    wstart = pl.multiple_of(jnp.maximum(c0 - NSA_WINDOW, 0), tq)
    w_chunk = wstart // tq
    gap = q_pos(span) - (wstart + key_iota(span))
    win_mask = (gap >= 0) & (gap < NSA_WINDOW)
    kw = kw_ref[0, pl.ds(wstart, span), :]
    o_w = []
    for qh in q_heads:
        z = jnp.where(win_mask, _dot_nt(kw, qh), NEG_INF)
        e = jnp.exp2(z - jnp.max(z, axis=0, keepdims=True)).astype(BF16)
        acc_w = _dot(vwt_ref[w_chunk], e[0:tq])
        for c in range(1, span // tq):
            acc_w = acc_w + _dot(vwt_ref[w_chunk + c], e[c * tq:(c + 1) * tq])
        o_w.append(acc_w[:HEAD_DIM] * (1.0 / acc_w[HEAD_DIM:HEAD_DIM + 1]))

    gates_t = (1.0 / (1.0 + jnp.exp(-gate_ref[0]))).T
    for h in range(hg):
        branches = (o_c[h], o_s[:, h * tq:(h + 1) * tq], o_w[h])
        out = jnp.zeros((HEAD_DIM, tq), F32)
        for branch, o_b in enumerate(branches):
            r = branch * hg + h
            out = out + gates_t[r:r + 1, :] * o_b
        o_ref[0, :, h * LANES:(h + 1) * LANES] = out.T.astype(o_ref.dtype)


def _nsa_attention(main, gate_logits, kc_cmp, vc_cmp_t, overlap_t):
    b, s, _ = main.shape
    g, hg = NSA_KV_GROUPS, NSA_HEADS_PER_GROUP
    n_sel = s // NSA_SEL_BLOCK
    assert s % NSA_TK == 0 and n_sel <= LANES
    n16 = kc_cmp.shape[2]
    q_blocks = N_HEADS

    def kv_spec(which):
        return pl.BlockSpec((1, s, LANES), lambda bi, gi, i: (bi, 0, q_blocks + which * g + gi),
                            pipeline_mode=pl.Buffered(1))

    return pl.pallas_call(
        functools.partial(_nsa_kernel, n_sel=n_sel),
        grid=(b, g, s // NSA_TQ),
        in_specs=[
            pl.BlockSpec((1, NSA_TQ, hg * LANES), lambda bi, gi, i: (bi, i, gi)),
            pl.BlockSpec((1, 1, n16, HEAD_DIM), lambda bi, gi, i: (bi, gi, 0, 0)),
            pl.BlockSpec((1, 1, HEAD_DIM, n16), lambda bi, gi, i: (bi, gi, 0, 0)),
            kv_spec(2), kv_spec(3), kv_spec(4), kv_spec(5),
            pl.BlockSpec((1, NSA_TQ, LANES), lambda bi, gi, i: (bi, i, gi)),
            _const_spec(overlap_t.shape),
        ],
        out_specs=pl.BlockSpec((1, NSA_TQ, hg * LANES), lambda bi, gi, i: (bi, i, gi)),
        out_shape=jax.ShapeDtypeStruct((b, s, N_HEADS * HEAD_DIM), BF16),
        scratch_shapes=[pltpu.VMEM((s, 2 * LANES), BF16),
                        pltpu.VMEM((s // NSA_TK, HEAD_DIM + ONES_ROWS, NSA_TK), BF16),
                        pltpu.VMEM((s // NSA_TQ, HEAD_DIM + ONES_ROWS, NSA_TQ), BF16)],
        compiler_params=_cparams("parallel", "parallel", "arbitrary"),
        name="nsa_attn",
    )(main, kc_cmp, vc_cmp_t, main, main, main, main, gate_logits, overlap_t)


def _nsa_overlap_t(seq):
    n_cmp = (seq - NSA_CMP_BLOCK) // NSA_CMP_STRIDE + 1
    n_sel = seq // NSA_SEL_BLOCK
    cmp_start = np.arange(seq // NSA_CMP_STRIDE) * NSA_CMP_STRIDE
    sel_start = np.arange(LANES) * NSA_SEL_BLOCK
    ov = ((cmp_start[:, None] < sel_start[None, :] + NSA_SEL_BLOCK)
          & (cmp_start[:, None] + NSA_CMP_BLOCK > sel_start[None, :]))
    ov &= (np.arange(seq // NSA_CMP_STRIDE)[:, None] < n_cmp) & (np.arange(LANES)[None, :] < n_sel)
    return jnp.asarray(ov.T, dtype=BF16)


def _glu_proj_kernel(x_ref, g_ref, w_ref, o_ref):
    h = _rms(x_ref[...], g_ref[...]).astype(BF16)
    d = o_ref.shape[1]
    for c in range(d // PROJ_CHUNK):
        a = _dot(h, w_ref[:, c * PROJ_CHUNK:(c + 1) * PROJ_CHUNK])
        gate = _dot(h, w_ref[:, d + c * PROJ_CHUNK:d + (c + 1) * PROJ_CHUNK])
        o_ref[:, c * PROJ_CHUNK:(c + 1) * PROJ_CHUNK] = (a / (1.0 + jnp.exp(-gate))).astype(o_ref.dtype)


def _glu_project(x2d, gain, w):
    t_rows, d = x2d.shape
    return pl.pallas_call(
        _glu_proj_kernel,
        grid=(t_rows // ROW_TILE,),
        in_specs=[pl.BlockSpec((ROW_TILE, d), lambda i: (i, 0)), _const_spec((1, d)), _const_spec(w.shape)],
        out_specs=pl.BlockSpec((ROW_TILE, d), lambda i: (i, 0)),
        out_shape=jax.ShapeDtypeStruct((t_rows, d), BF16),
        compiler_params=_cparams("parallel"),
        name="norm_glu_proj",
    )(x2d, gain.reshape(1, d), w)


CONV_TILE = 256
SUBLANES = 8
CONV_PHASE_ROWS = CONV_TILE + CONV_HALO - SUBLANES


def _conv_kernel(u_ref, halo_ref, dw_ref, db_ref, lg_ref, lb_ref, o_ref, ext_ref, phase_ref):
    i = pl.program_id(1)
    halo = halo_ref[0].astype(F32)
    ext_ref[0:CONV_HALO, :] = jnp.where(i == 0, 0.0, halo)
    ext_ref[CONV_HALO:, :] = u_ref[0].astype(F32)
    for b in range(1, SUBLANES):
        phase_ref[b - 1] = ext_ref[pl.ds(b, CONV_PHASE_ROWS), :]
    lead = CONV_HALO - (CONV_WIDTH - 1)
    for r in range(CONV_TILE // CONV_ROWS):
        acc = jnp.zeros((CONV_ROWS, D_MODEL), F32) + db_ref[...]
        for w in range(CONV_WIDTH):
            shift = (lead + w) % SUBLANES
            start = r * CONV_ROWS + lead + w - shift
            src = ext_ref if shift == 0 else phase_ref.at[shift - 1]
            acc = acc + dw_ref[w:w + 1, :] * src[pl.ds(start, CONV_ROWS), :]
        mu = jnp.mean(acc, axis=-1, keepdims=True)
        cen = acc - mu
        var = jnp.mean(cen * cen, axis=-1, keepdims=True)
        un = cen * lax.rsqrt(var + NORM_EPS) * lg_ref[...] + lb_ref[...]
        o_ref[0, r * CONV_ROWS:(r + 1) * CONV_ROWS, :] = (un / (1.0 + jnp.exp(-un))).astype(o_ref.dtype)


def _conv_ln_swish(u, dw_w, dw_b, ln_g, ln_b):
    b, s, d = u.shape
    halo_per_tile = CONV_TILE // CONV_HALO
    dw = jnp.concatenate([dw_w, jnp.zeros((1, d), F32)], axis=0)
    return pl.pallas_call(
        _conv_kernel,
        grid=(b, s // CONV_TILE),
        in_specs=[
            pl.BlockSpec((1, CONV_TILE, d), lambda bi, i: (bi, i, 0)),
            pl.BlockSpec((1, CONV_HALO, d), lambda bi, i: (bi, jnp.maximum(i * halo_per_tile - 1, 0), 0)),
            _const_spec(dw.shape), _const_spec((1, d)), _const_spec((1, d)), _const_spec((1, d)),
        ],
        out_specs=pl.BlockSpec((1, CONV_TILE, d), lambda bi, i: (bi, i, 0)),
        out_shape=jax.ShapeDtypeStruct((b, s, d), BF16),
        scratch_shapes=[pltpu.VMEM((CONV_TILE + CONV_HALO, d), F32),
                        pltpu.VMEM((SUBLANES - 1, CONV_PHASE_ROWS, d), F32)],
        compiler_params=_cparams("parallel", "parallel"),
        name="conv_ln_swish",
    )(u, u, dw, dw_b.reshape(1, d), ln_g.reshape(1, d), ln_b.reshape(1, d))


def _rope_tables(seq):
    half = HEAD_DIM // 2
    inv_freq = ROPE_THETA ** (-jnp.arange(half, dtype=F32) / half)
    ang = jnp.arange(seq, dtype=F32)[:, None] * inv_freq[None, :]
    cos, sin = jnp.cos(ang), jnp.sin(ang)
    return jnp.concatenate([cos, cos], axis=1), jnp.concatenate([-sin, sin], axis=1)


def _nsa_mixer(x2d, norm_gain, w_in, q_gain, k_gain, cmp_pos, w_cmp, tables, batch, seq):
    g, dh = NSA_KV_GROUPS, HEAD_DIM
    n_main = (N_HEADS + 6 * g) * dh
    scale = dh ** -0.5 * LOG2_E
    w_main = w_in[:, :n_main].astype(BF16)
    hg = NSA_HEADS_PER_GROUP
    w_g = w_in[:, n_main:].reshape(-1, 3, g, hg).transpose(0, 2, 1, 3).reshape(-1, g, 3 * hg)
    w_gate = jnp.pad(w_g, ((0, 0), (0, 0), (0, LANES - 3 * hg))).reshape(-1, g * LANES).astype(BF16)
    ones = jnp.ones((dh,), F32)
    head_gains = jnp.stack([q_gain * scale] * N_HEADS + [k_gain[0]] * g + [ones] * g
                           + [k_gain[1]] * g + [ones] * g + [k_gain[2]] * g + [ones] * g)
    rope_blocks = [True] * N_HEADS + [True] * g + [False] * g + [True] * g + [False] * g + [True] * g + [False] * g
    main, gate_logits = _project(x2d, norm_gain, w_main, *tables, head_gains, rope_blocks, seq, w_gate=w_gate)
    main = main.reshape(batch, seq, n_main)
    gate_logits = gate_logits.reshape(batch, seq, g * LANES)

    kc_cmp, vc_cmp_t = _nsa_compress(main, cmp_pos, w_cmp)
    out = _nsa_attention(main, gate_logits, kc_cmp, vc_cmp_t, _nsa_overlap_t(seq))
    return out.reshape(batch * seq, N_HEADS * dh)


def _qkv_weight_with_scaled_q(w_in):
    n_q = N_HEADS * HEAD_DIM
    return jnp.concatenate([w_in[:, :n_q] * HEAD_DIM ** -0.5, w_in[:, n_q:]], axis=1).astype(BF16)


def _sb_mixer(x2d, norm_gain, w_in, tables, batch, seq):
    n = 3 * N_HEADS * HEAD_DIM
    head_gains = jnp.ones((n // LANES, HEAD_DIM), F32)
    qkv = _project(x2d, norm_gain, _qkv_weight_with_scaled_q(w_in), *tables, head_gains,
                   [False] * (n // LANES), seq)
    return _sb_attention(qkv.reshape(batch, seq, n)).reshape(batch * seq, N_HEADS * HEAD_DIM)


def _conv_mixer(x2d, norm_gain, w_in, dw_w, dw_b, ln_g, ln_b, batch, seq):
    u = _glu_project(x2d, norm_gain, w_in.astype(BF16))
    a = _conv_ln_swish(u.reshape(batch, seq, D_MODEL), dw_w, dw_b, ln_g, ln_b)
    return a.reshape(batch * seq, D_MODEL)


def _moba_mixer(x2d, norm_gain, w_in, q_gain, k_gain, tables, batch, seq):
    n = 3 * N_HEADS * HEAD_DIM
    ones = jnp.ones((HEAD_DIM,), F32)
    head_gains = jnp.stack([q_gain * (HEAD_DIM ** -0.5 * LOG2_E)] * N_HEADS + [k_gain] * N_HEADS + [ones] * N_HEADS)
    rope_blocks = [True] * (2 * N_HEADS) + [False] * N_HEADS
    qkv = _project(x2d, norm_gain, w_in.astype(BF16), *tables, head_gains, rope_blocks, seq)
    return _moba_attention(qkv.reshape(batch, seq, n)).reshape(batch * seq, N_HEADS * HEAD_DIM)


def kernel(x, attn_norm, mlp_norm, mlp_w_up, mlp_w_down, nsa_w_in, nsa_q_norm, nsa_k_norm, nsa_cmp_pos, nsa_w_cmp, nsa_w_out, sb_w_in, sb_w_out, conv_w_in, conv_dw_w, conv_dw_b, conv_ln_g, conv_ln_b, conv_w_out, moba_w_in, moba_q_norm, moba_k_norm, moba_w_out):
    batch, seq, d = x.shape
    depth = attn_norm.shape[0]
    tables = _rope_tables(seq)
    x2d = x.reshape(batch * seq, d)
    for i in range(depth):
        m, j = i % 4, i // 4
        if m == 0:
            a = _nsa_mixer(x2d, attn_norm[i], nsa_w_in[j], nsa_q_norm[j], nsa_k_norm[j], nsa_cmp_pos[j],
                           nsa_w_cmp[j], tables, batch, seq)
            w_out = nsa_w_out[j]
        elif m == 1:
            a = _sb_mixer(x2d, attn_norm[i], sb_w_in[j], tables, batch, seq)
            w_out = sb_w_out[j]
        elif m == 2:
            a = _conv_mixer(x2d, attn_norm[i], conv_w_in[j], conv_dw_w[j], conv_dw_b[j], conv_ln_g[j],
                            conv_ln_b[j], batch, seq)
            w_out = conv_w_out[j]
        else:
            a = _moba_mixer(x2d, attn_norm[i], moba_w_in[j], moba_q_norm[j], moba_k_norm[j], tables, batch, seq)
            w_out = moba_w_out[j]
        x2d = _mixer_out_and_mlp(x2d, a, w_out.astype(BF16), mlp_norm[i], mlp_w_up[i].astype(BF16),
                                 mlp_w_down[i].astype(BF16))
    return x2d.reshape(batch, seq, d)
```

```python
import functools

import numpy as np
import jax
import jax.numpy as jnp
from jax import lax
from jax.experimental import pallas as pl
from jax.experimental.pallas import tpu as pltpu

F32 = jnp.float32
BF16 = jnp.bfloat16

D_MODEL = 1024
N_HEADS = 8
HEAD_DIM = 128
D_FF = 4 * D_MODEL
ROPE_THETA = 10000.0
NORM_EPS = 1e-6
NEG_INF = -1e30
LOG2_E = 1.4426950408889634

NSA_KV_GROUPS = 2
NSA_HEADS_PER_GROUP = N_HEADS // NSA_KV_GROUPS
NSA_CMP_BLOCK = 32
NSA_CMP_STRIDE = 16
NSA_SEL_BLOCK = 64
NSA_SEL_TOPK = 16
NSA_WINDOW = 512
NSA_FORCE_BONUS = 1000.0

CONV_WIDTH = 31
MOBA_BLOCK = 256
MOBA_TOPK = 3

LANES = 128
VMEM_LIMIT_BYTES = 56 * 1024 * 1024
ROW_TILE = 512
PROJ_CHUNK = 512
FF_CHUNK = 1024
CONV_HALO = 32
CONV_ROWS = 32


def _cparams(*sem):
    return pltpu.CompilerParams(dimension_semantics=sem, vmem_limit_bytes=VMEM_LIMIT_BYTES)


def _const_spec(shape):
    zeros = (0,) * len(shape)
    return pl.BlockSpec(shape, lambda *_: zeros, pipeline_mode=pl.Buffered(1))


def _rms(x, gain):
    return x * lax.rsqrt(jnp.mean(x * x, axis=-1, keepdims=True) + NORM_EPS) * gain


def _dot(a, b):
    return jnp.dot(a, b, preferred_element_type=F32)


def _dot_nt(a, b):
    return lax.dot_general(a, b, (((1,), (1,)), ((), ())), preferred_element_type=F32)


def _split_bf16(x):
    hi = x.astype(BF16)
    lo = (x - hi.astype(F32)).astype(BF16)
    return hi, lo


def _proj_kernel(*refs, rope_blocks, nsa_extras, slab_chunk):
    if nsa_extras:
        x_ref, g_ref, w_ref, cos_ref, sin_ref, hg_ref, wg_ref, o_ref, og_ref, oslab_ref, slab_ref = refs
    else:
        x_ref, g_ref, w_ref, cos_ref, sin_ref, hg_ref, o_ref = refs
    h = _rms(x_ref[...], g_ref[...]).astype(BF16)
    blocks_per_chunk = PROJ_CHUNK // LANES
    for c in range(len(rope_blocks) // blocks_per_chunk):
        y = _dot(h, w_ref[:, c * PROJ_CHUNK:(c + 1) * PROJ_CHUNK])
        chunk_flags = rope_blocks[c * blocks_per_chunk:(c + 1) * blocks_per_chunk]
        keep_slabs = nsa_extras and c == slab_chunk
        if not any(chunk_flags) and not keep_slabs:
            o_ref[:, c * PROJ_CHUNK:(c + 1) * PROJ_CHUNK] = y.astype(o_ref.dtype)
            continue
        for k, flag in enumerate(chunk_flags):
            b = c * blocks_per_chunk + k
            yb = y[:, k * LANES:(k + 1) * LANES]
            if flag:
                yb = _rms(yb, hg_ref[b:b + 1, :])
                yb = yb * cos_ref[...] + pltpu.roll(yb, HEAD_DIM // 2, 1) * sin_ref[...]
            o_ref[:, b * LANES:(b + 1) * LANES] = yb.astype(o_ref.dtype)
            if keep_slabs:
                slab_ref[k] = yb
    if nsa_extras:
        og_ref[...] = _dot(h, wg_ref[...])
        st = NSA_CMP_STRIDE
        for l in range(st):
            for k in range(blocks_per_chunk):
                rows_l = slab_ref[k, pl.ds(l, ROW_TILE // st, stride=st), :]
                oslab_ref[l, :, k * LANES:(k + 1) * LANES] = rows_l.astype(oslab_ref.dtype)


def _project(x2d, gain, w, cos_t, sin_t, head_gains, rope_blocks, seq, w_gate=None, slab_chunk=None):
    t_rows, d = x2d.shape
    n = w.shape[1]
    assert n % PROJ_CHUNK == 0 and len(rope_blocks) == n // LANES and seq % ROW_TILE == 0
    seq_tiles = seq // ROW_TILE
    has_gate = w_gate is not None
    in_specs = [
        pl.BlockSpec((ROW_TILE, d), lambda i: (i, 0)),
        _const_spec((1, d)),
        _const_spec((d, n)),
        pl.BlockSpec((ROW_TILE, LANES), lambda i: (i % seq_tiles, 0)),
        pl.BlockSpec((ROW_TILE, LANES), lambda i: (i % seq_tiles, 0)),
        _const_spec(head_gains.shape),
    ]
    args = [x2d, gain.reshape(1, d), w, cos_t, sin_t, head_gains]
    out_shape = [jax.ShapeDtypeStruct((t_rows, n), BF16)]
    out_specs = [pl.BlockSpec((ROW_TILE, n), lambda i: (i, 0))]
    if has_gate:
        in_specs.append(_const_spec(w_gate.shape))
        args.append(w_gate)
        out_shape.append(jax.ShapeDtypeStruct((t_rows, w_gate.shape[1]), F32))
        out_specs.append(pl.BlockSpec((ROW_TILE, w_gate.shape[1]), lambda i: (i, 0)))
        st = NSA_CMP_STRIDE
        out_shape.append(jax.ShapeDtypeStruct((st, t_rows // st, PROJ_CHUNK), BF16))
        out_specs.append(pl.BlockSpec((st, ROW_TILE // st, PROJ_CHUNK), lambda i: (0, i, 0)))
    outs = pl.pallas_call(
        functools.partial(_proj_kernel, rope_blocks=tuple(rope_blocks), nsa_extras=has_gate,
                          slab_chunk=slab_chunk),
        grid=(t_rows // ROW_TILE,),
        in_specs=in_specs,
        out_specs=out_specs,
        out_shape=out_shape,
        scratch_shapes=[pltpu.VMEM((PROJ_CHUNK // LANES, ROW_TILE, LANES), F32)] if has_gate else [],
        compiler_params=_cparams("parallel"),
        name="norm_proj",
    )(*args)
    return outs if has_gate else outs[0]


def _mlp_kernel(x_ref, a_ref, wo_ref, g_ref, wup_ref, wdn_ref, o_ref):
    x1 = x_ref[...] + _dot(a_ref[...], wo_ref[...])
    h = _rms(x1, g_ref[...]).astype(BF16)
    acc = x1
    for c in range(D_FF // FF_CHUNK):
        u = _dot(h, wup_ref[:, c * FF_CHUNK:(c + 1) * FF_CHUNK])
        act = jnp.square(jnp.maximum(u, 0.0)).astype(BF16)
        acc = acc + _dot(act, wdn_ref[c * FF_CHUNK:(c + 1) * FF_CHUNK, :])
    o_ref[...] = acc


def _mixer_out_and_mlp(x2d, a2d, w_out, gain, w_up, w_down):
    t_rows, d = x2d.shape
    return pl.pallas_call(
        _mlp_kernel,
        grid=(t_rows // ROW_TILE,),
        in_specs=[
            pl.BlockSpec((ROW_TILE, d), lambda i: (i, 0)),
            pl.BlockSpec((ROW_TILE, d), lambda i: (i, 0)),
            _const_spec((d, d)),
            _const_spec((1, d)),
            _const_spec((d, D_FF)),
            _const_spec((D_FF, d)),
        ],
        out_specs=pl.BlockSpec((ROW_TILE, d), lambda i: (i, 0)),
        out_shape=jax.ShapeDtypeStruct((t_rows, d), F32),
        compiler_params=_cparams("parallel"),
        name="outproj_mlp",
    )(x2d, a2d, w_out, gain.reshape(1, d), w_up, w_down)


SB_TQ = 1024
SB_TK = 256
SB_UNDERFLOW_LOG = -104.0


def _sb_kernel(q_ref, k_ref, v_ref, o_ref):
    i = pl.program_id(2)
    tq, tk = SB_TQ, SB_TK
    own_blocks = tq // tk
    suffix_ones = (lax.broadcasted_iota(jnp.int32, (tk, tk), 0)
                   > lax.broadcasted_iota(jnp.int32, (tk, tk), 1)).astype(BF16)

    def block(j, q, carry, diagonal):
        o, later = carry
        start = pl.multiple_of(j * tk, tk)
        kj = k_ref[0, pl.ds(start, tk), :]
        vj = v_ref[0, pl.ds(start, tk), :]
        z = _dot_nt(q, kj)
        softplus = jnp.maximum(z, 0.0) + jnp.log(1.0 + jnp.exp(-jnp.abs(z)))
        log_om = -softplus
        if diagonal:
            below = (lax.broadcasted_iota(jnp.int32, z.shape, 0)
                     > lax.broadcasted_iota(jnp.int32, z.shape, 1))
            log_om = jnp.where(below, log_om, 0.0)
        hi, lo = _split_bf16(log_om)
        between = _dot(hi, suffix_ones) + _dot(lo, suffix_ones) + later
        a = jnp.exp(z - softplus + between)
        if diagonal:
            a = jnp.where(below, a, 0.0)
        o = o + _dot(a.astype(BF16), vj)
        later = later + jnp.sum(log_om, axis=1, keepdims=True)
        return o, later

    o = jnp.zeros((tq, HEAD_DIM), F32)
    later = jnp.zeros((tq, 1), F32)
    for b in reversed(range(own_blocks)):
        r0 = b * tk
        o_b, later_b = block(i * own_blocks + b, q_ref[0, r0:, :], (o[r0:], later[r0:]), True)
        if b > 0:
            o = jnp.concatenate([o[:r0], o_b], axis=0)
            later = jnp.concatenate([later[:r0], later_b], axis=0)
        else:
            o, later = o_b, later_b

    def more(state):
        j, _, later = state
        return (j >= 0) & (jnp.max(later) > SB_UNDERFLOW_LOG)

    def step(state):
        j, o, later = state
        o, later = block(j, q_ref[0], (o, later), False)
        return j - 1, o, later

    _, o, _ = lax.while_loop(more, step, (i * own_blocks - 1, o, later))
    o_ref[0] = o.astype(o_ref.dtype)


def _sb_attention(qkv):
    b, s, _ = qkv.shape
    h = N_HEADS
    return pl.pallas_call(
        _sb_kernel,
        grid=(b, h, s // SB_TQ),
        in_specs=[
            pl.BlockSpec((1, SB_TQ, LANES), lambda bi, hi, i: (bi, i, hi)),
            pl.BlockSpec((1, s, LANES), lambda bi, hi, i: (bi, 0, h + hi)),
            pl.BlockSpec((1, s, LANES), lambda bi, hi, i: (bi, 0, 2 * h + hi)),
        ],
        out_specs=pl.BlockSpec((1, SB_TQ, LANES), lambda bi, hi, i: (bi, i, hi)),
        out_shape=jax.ShapeDtypeStruct((b, s, h * HEAD_DIM), BF16),
        compiler_params=_cparams("parallel", "parallel", "arbitrary"),
        name="stick_breaking_attn",
    )(qkv, qkv, qkv)


MOBA_TQ = 2048
MOBA_TK = 1024
MOBA_OWN_TK = 512
ONES_ROWS = 16


def _online_softmax_step_t(carry, s_t, v_t):
    m, acc = carry
    m_new = jnp.maximum(m, jnp.max(s_t, axis=0, keepdims=True))
    p_t = jnp.exp2(s_t - m_new)
    acc = jnp.exp2(m - m_new) * acc + _dot(v_t, p_t.astype(BF16))
    return m_new, acc


def _flash_loop_t(n_steps, q_aug, keys_fn, values_t_fn, carry):
    def body(n, carry):
        return _online_softmax_step_t(carry, _dot_nt(keys_fn(n), q_aug), values_t_fn(n))

    return lax.fori_loop(0, n_steps, body, carry)


def _transposed_values_with_ones(v):
    v_t = v.astype(F32).T.astype(BF16)
    return jnp.concatenate([v_t, jnp.ones((ONES_ROWS, v.shape[0]), BF16)], axis=0)


def _moba_kernel(q_ref, k_ref, v_ref, o_ref, kaug_ref, kmean_ref, vt_ref, *, n_blk):
    i = pl.program_id(2)
    t, bs = MOBA_TQ, MOBA_BLOCK
    s_len = k_ref.shape[1]

    @pl.when(i == 0)
    def _():
        k = k_ref[0]
        kaug_ref[:, :LANES] = k
        blk = lax.broadcasted_iota(jnp.int32, (s_len, LANES), 0) // bs
        lane = lax.broadcasted_iota(jnp.int32, (s_len, LANES), 1)
        kaug_ref[:, LANES:] = (blk == lane).astype(BF16)
        kmean_ref[...] = jnp.mean(k.astype(F32).reshape(n_blk, bs, LANES), axis=1)
        for c in range(s_len // MOBA_TK):
            vt_ref[c] = _transposed_values_with_ones(v_ref[0, c * MOBA_TK:(c + 1) * MOBA_TK, :])

    q = q_ref[0]
    km_hi, km_lo = _split_bf16(kmean_ref[...])
    gate = _dot_nt(km_hi, q) + _dot_nt(km_lo, q)
    blk_id = lax.broadcasted_iota(jnp.int32, (n_blk, t), 0)
    cur = i * (t // bs) + lax.broadcasted_iota(jnp.int32, (n_blk, t), 1) // bs
    past = blk_id < cur
    gate = jnp.where(past, gate, -jnp.inf)
    visible = (_top_k_mask_t(gate, MOBA_TOPK) & past) | (blk_id == cur)
    sel_bias = jnp.where(visible, 0.0, NEG_INF)
    sel_bias = jnp.concatenate([sel_bias, jnp.zeros((LANES - n_blk, t), F32)], axis=0)
    q_aug = jnp.concatenate([q, sel_bias.T.astype(BF16)], axis=1)

    tk = MOBA_TK

    def keys(n):
        return kaug_ref[pl.ds(pl.multiple_of(n * tk, tk), tk), :]

    def values_t(n):
        return vt_ref[n]

    own = i * (t // tk)
    to = MOBA_OWN_TK
    causal = (lax.broadcasted_iota(jnp.int32, (to, t), 0) <= lax.broadcasted_iota(jnp.int32, (to, t), 1))
    carry = None
    for u in range(t // to):
        r0 = u * to
        k_u = kaug_ref[pl.ds(pl.multiple_of(i * t + r0, to), to), :]
        v_u = vt_ref[own + r0 // tk][:, r0 % tk:r0 % tk + to]
        s_t = jnp.where(causal[:, :t - r0], _dot_nt(k_u, q_aug[r0:]), NEG_INF)
        if carry is None:
            m0 = jnp.max(s_t, axis=0, keepdims=True)
            carry = (m0, _dot(v_u, jnp.exp2(s_t - m0).astype(BF16)))
        else:
            m_u, acc_u = _online_softmax_step_t((carry[0][:, r0:], carry[1][:, r0:]), s_t, v_u)
            carry = (jnp.concatenate([carry[0][:, :r0], m_u], axis=1),
                     jnp.concatenate([carry[1][:, :r0], acc_u], axis=1))

    _, acc = _flash_loop_t(own, q_aug, keys, values_t, carry)
    o_t = acc[:HEAD_DIM] * (1.0 / acc[HEAD_DIM:HEAD_DIM + 1])
    o_ref[0] = o_t.T.astype(o_ref.dtype)


def _moba_attention(qkv):
    b, s, _ = qkv.shape
    h = N_HEADS
    assert s % MOBA_TQ == 0 and MOBA_TQ % MOBA_BLOCK == 0
    n_blk = s // MOBA_BLOCK
    assert n_blk % 8 == 0 and n_blk <= LANES
    return pl.pallas_call(
        functools.partial(_moba_kernel, n_blk=n_blk),
        grid=(b, h, s // MOBA_TQ),
        in_specs=[
            pl.BlockSpec((1, MOBA_TQ, LANES), lambda bi, hi, i: (bi, i, hi)),
            pl.BlockSpec((1, s, LANES), lambda bi, hi, i: (bi, 0, h + hi)),
            pl.BlockSpec((1, s, LANES), lambda bi, hi, i: (bi, 0, 2 * h + hi)),
        ],
        out_specs=pl.BlockSpec((1, MOBA_TQ, LANES), lambda bi, hi, i: (bi, i, hi)),
        out_shape=jax.ShapeDtypeStruct((b, s, h * HEAD_DIM), BF16),
        scratch_shapes=[pltpu.VMEM((s, 2 * LANES), BF16), pltpu.VMEM((n_blk, LANES), F32),
                        pltpu.VMEM((s // MOBA_TK, HEAD_DIM + ONES_ROWS, MOBA_TK), BF16)],
        compiler_params=_cparams("parallel", "parallel", "arbitrary"),
        name="moba_attn",
    )(qkv, qkv, qkv)


NSA_TQ = 512
NSA_TK = 1024
def _cmp_kernel(*refs):
    st = NSA_CMP_STRIDE
    x_refs, (pos_ref, w_ref, ok_ref, ov_ref) = refs[:2 * st], refs[2 * st:]
    n16 = ok_ref.shape[2]
    for t, o_ref in enumerate((ok_ref, ov_ref)):
        first = jnp.zeros((n16, HEAD_DIM), F32)
        second = jnp.zeros((n16, HEAD_DIM), F32)
        for l in range(st):
            x = x_refs[t * st + l][0].astype(F32)
            first = first + _dot((x + pos_ref[t, l:l + 1, :]).astype(BF16), w_ref[t, l])
            second = second + _dot((x + pos_ref[t, st + l:st + l + 1, :]).astype(BF16), w_ref[t, st + l])
        out = first + pltpu.roll(second, n16 - 1, 0)
        o_ref[0, 0] = (out if t == 0 else out.T).astype(o_ref.dtype)


def _nsa_compress(slabs, batch, cmp_pos, w_cmp):
    g, st = NSA_KV_GROUPS, NSA_CMP_STRIDE
    b = batch
    n16 = slabs.shape[1] // b

    def slab_spec(first_block, l):
        return pl.BlockSpec((1, n16, LANES), lambda bi, gi: (l, bi, first_block + gi))

    x_specs = [slab_spec(t * g, l) for t in range(2) for l in range(st)]
    pos = cmp_pos.astype(F32)
    w = w_cmp.astype(BF16)
    return pl.pallas_call(
        _cmp_kernel,
        grid=(b, g),
        in_specs=x_specs + [_const_spec(pos.shape), _const_spec(w.shape)],
        out_specs=[pl.BlockSpec((1, 1, n16, HEAD_DIM), lambda bi, gi: (bi, gi, 0, 0)),
                   pl.BlockSpec((1, 1, HEAD_DIM, n16), lambda bi, gi: (bi, gi, 0, 0))],
        out_shape=[jax.ShapeDtypeStruct((b, g, n16, HEAD_DIM), BF16),
                   jax.ShapeDtypeStruct((b, g, HEAD_DIM, n16), BF16)],
        compiler_params=_cparams("parallel", "parallel"),
        name="nsa_compress",
    )(*([slabs] * (2 * st)), pos, w)


def _top_k_mask_t(score_t, k):
    n, q = score_t.shape
    work = score_t
    taken = jnp.zeros((1, q), F32)
    level = jnp.full((1, q), jnp.inf, F32)
    above = jnp.zeros((1, q), F32)
    for _ in range(k):
        best = jnp.max(work, axis=0, keepdims=True)
        hit = work == best
        active = taken < k
        level = jnp.where(active, best, level)
        above = jnp.where(active, taken, above)
        taken = taken + jnp.sum(hit.astype(F32), axis=0, keepdims=True)
        work = jnp.where(hit, -jnp.inf, work)
    tie = score_t == level
    lower = (lax.broadcasted_iota(jnp.int32, (n, n), 1)
             < lax.broadcasted_iota(jnp.int32, (n, n), 0)).astype(BF16)
    ties_before = _dot(lower, tie.astype(BF16))
    return (score_t > level) | (tie & (ties_before < k - above))


def _nsa_kernel(q_ref, kc_ref, vc_ref, ks_ref, vs_ref, kw_ref, vw_ref, gate_ref, ov_ref, o_ref,
                ksaug_ref, vst_ref, vwt_ref, *, n_sel):
    i = pl.program_id(2)
    tq, tk, hg = NSA_TQ, NSA_TK, NSA_HEADS_PER_GROUP
    s_len = ks_ref.shape[1]
    n_cmp_pad = kc_ref.shape[2]
    c0 = i * tq

    @pl.when(i == 0)
    def _():
        ksaug_ref[:, :LANES] = ks_ref[0]
        blk = lax.broadcasted_iota(jnp.int32, (s_len, LANES), 0) // NSA_SEL_BLOCK
        lane = lax.broadcasted_iota(jnp.int32, (s_len, LANES), 1)
        ksaug_ref[:, LANES:] = (blk == lane).astype(BF16)
        for c in range(s_len // tk):
            vst_ref[c] = _transposed_values_with_ones(vs_ref[0, c * tk:(c + 1) * tk, :])
        for c in range(s_len // tq):
            vwt_ref[c] = _transposed_values_with_ones(vw_ref[0, c * tq:(c + 1) * tq, :])

    q_all = q_ref[0]
    q_heads = [q_all[:, h * LANES:(h + 1) * LANES] for h in range(hg)]

    def key_iota(n):
        return lax.broadcasted_iota(jnp.int32, (n, tq), 0)

    def q_pos(n):
        return c0 + lax.broadcasted_iota(jnp.int32, (n, tq), 1)

    kc = kc_ref[0, 0]
    cmp_mask = key_iota(n_cmp_pad) * NSA_CMP_STRIDE + (NSA_CMP_BLOCK - 1) <= q_pos(n_cmp_pad)
    o_c, p_sum = [], jnp.zeros((n_cmp_pad, tq), F32)
    for qh in q_heads:
        z = jnp.where(cmp_mask, _dot_nt(kc, qh), NEG_INF)
        e = jnp.where(cmp_mask, jnp.exp2(z - jnp.max(z, axis=0, keepdims=True)), 0.0)
        l = jnp.sum(e, axis=0, keepdims=True)
        inv = 1.0 / jnp.where(l > 0.0, l, 1.0)
        o_c.append(_dot(vc_ref[0, 0], e.astype(BF16)) * inv)
        p_sum = p_sum + e * inv

    ps_hi, ps_lo = _split_bf16(p_sum)
    imp = _dot(ov_ref[...], ps_hi) + _dot(ov_ref[...], ps_lo)
    blk = key_iota(LANES)
    cur = q_pos(LANES) // NSA_SEL_BLOCK
    forced = (blk == 0) | (blk == cur) | (blk == cur - 1)
    score = jnp.where(blk <= cur, imp + NSA_FORCE_BONUS * forced.astype(F32), -1.0)
    score = jnp.where(blk < n_sel, score, -jnp.inf)
    chosen = _top_k_mask_t(score, min(NSA_SEL_TOPK, n_sel))
    sel_bias = jnp.where(chosen, 0.0, NEG_INF).T.astype(BF16)
    q_aug = jnp.concatenate([jnp.concatenate(q_heads, axis=0),
                             jnp.concatenate([sel_bias] * hg, axis=0)], axis=1)

    def keys(j):
        return ksaug_ref[pl.ds(pl.multiple_of(j * tk, tk), tk), :]

    def values_t(j):
        return vst_ref[j]

    jd = c0 // tk
    causal = jd * tk + key_iota(tk) <= q_pos(tk)
    s_t = jnp.where(jnp.concatenate([causal] * hg, axis=1), _dot_nt(keys(jd), q_aug), NEG_INF)
    m0 = jnp.max(s_t, axis=0, keepdims=True)
    carry = (m0, _dot(values_t(jd), jnp.exp2(s_t - m0).astype(BF16)))
    _, acc = _flash_loop_t(jd, q_aug, keys, values_t, carry)
    o_s = acc[:HEAD_DIM] * (1.0 / acc[HEAD_DIM:HEAD_DIM + 1])

    span = tq + NSA_WINDOW
    wstart = pl.multiple_of(jnp.maximum(c0 - NSA_WINDOW, 0), tq)
    w_chunk = wstart // tq
    gap = q_pos(span) - (wstart + key_iota(span))
    win_mask = (gap >= 0) & (gap < NSA_WINDOW)
    kw = kw_ref[0, pl.ds(wstart, span), :]
    o_w = []
    for qh in q_heads:
        z = jnp.where(win_mask, _dot_nt(kw, qh), NEG_INF)
        e = jnp.exp2(z - jnp.max(z, axis=0, keepdims=True)).astype(BF16)
        acc_w = _dot(vwt_ref[w_chunk], e[0:tq])
        for c in range(1, span // tq):
            acc_w = acc_w + _dot(vwt_ref[w_chunk + c], e[c * tq:(c + 1) * tq])
        o_w.append(acc_w[:HEAD_DIM] * (1.0 / acc_w[HEAD_DIM:HEAD_DIM + 1]))

    gates_t = (1.0 / (1.0 + jnp.exp(-gate_ref[0]))).T
    for h in range(hg):
        branches = (o_c[h], o_s[:, h * tq:(h + 1) * tq], o_w[h])
        out = jnp.zeros((HEAD_DIM, tq), F32)
        for branch, o_b in enumerate(branches):
            r = branch * hg + h
            out = out + gates_t[r:r + 1, :] * o_b
        o_ref[0, :, h * LANES:(h + 1) * LANES] = out.T.astype(o_ref.dtype)


def _nsa_attention(main, gate_logits, kc_cmp, vc_cmp_t, overlap_t):
    b, s, _ = main.shape
    g, hg = NSA_KV_GROUPS, NSA_HEADS_PER_GROUP
    n_sel = s // NSA_SEL_BLOCK
    assert s % NSA_TK == 0 and n_sel <= LANES
    n16 = kc_cmp.shape[2]
    q_blocks = N_HEADS

    def kv_spec(which):
        return pl.BlockSpec((1, s, LANES), lambda bi, gi, i: (bi, 0, q_blocks + which * g + gi),
                            pipeline_mode=pl.Buffered(1))

    return pl.pallas_call(
        functools.partial(_nsa_kernel, n_sel=n_sel),
        grid=(b, g, s // NSA_TQ),
        in_specs=[
            pl.BlockSpec((1, NSA_TQ, hg * LANES), lambda bi, gi, i: (bi, i, gi)),
            pl.BlockSpec((1, 1, n16, HEAD_DIM), lambda bi, gi, i: (bi, gi, 0, 0)),
            pl.BlockSpec((1, 1, HEAD_DIM, n16), lambda bi, gi, i: (bi, gi, 0, 0)),
            kv_spec(2), kv_spec(3), kv_spec(4), kv_spec(5),
            pl.BlockSpec((1, NSA_TQ, LANES), lambda bi, gi, i: (bi, i, gi)),
            _const_spec(overlap_t.shape),
        ],
        out_specs=pl.BlockSpec((1, NSA_TQ, hg * LANES), lambda bi, gi, i: (bi, i, gi)),
        out_shape=jax.ShapeDtypeStruct((b, s, N_HEADS * HEAD_DIM), BF16),
        scratch_shapes=[pltpu.VMEM((s, 2 * LANES), BF16),
                        pltpu.VMEM((s // NSA_TK, HEAD_DIM + ONES_ROWS, NSA_TK), BF16),
                        pltpu.VMEM((s // NSA_TQ, HEAD_DIM + ONES_ROWS, NSA_TQ), BF16)],
        compiler_params=_cparams("parallel", "parallel", "arbitrary"),
        name="nsa_attn",
    )(main, kc_cmp, vc_cmp_t, main, main, main, main, gate_logits, overlap_t)


def _nsa_overlap_t(seq):
    n_cmp = (seq - NSA_CMP_BLOCK) // NSA_CMP_STRIDE + 1
    n_sel = seq // NSA_SEL_BLOCK
    cmp_start = np.arange(seq // NSA_CMP_STRIDE) * NSA_CMP_STRIDE
    sel_start = np.arange(LANES) * NSA_SEL_BLOCK
    ov = ((cmp_start[:, None] < sel_start[None, :] + NSA_SEL_BLOCK)
          & (cmp_start[:, None] + NSA_CMP_BLOCK > sel_start[None, :]))
    ov &= (np.arange(seq // NSA_CMP_STRIDE)[:, None] < n_cmp) & (np.arange(LANES)[None, :] < n_sel)
    return jnp.asarray(ov.T, dtype=BF16)


def _glu_proj_kernel(x_ref, g_ref, w_ref, o_ref):
    h = _rms(x_ref[...], g_ref[...]).astype(BF16)
    d = o_ref.shape[1]
    for c in range(d // PROJ_CHUNK):
        a = _dot(h, w_ref[:, c * PROJ_CHUNK:(c + 1) * PROJ_CHUNK])
        gate = _dot(h, w_ref[:, d + c * PROJ_CHUNK:d + (c + 1) * PROJ_CHUNK])
        o_ref[:, c * PROJ_CHUNK:(c + 1) * PROJ_CHUNK] = (a / (1.0 + jnp.exp(-gate))).astype(o_ref.dtype)


def _glu_project(x2d, gain, w):
    t_rows, d = x2d.shape
    return pl.pallas_call(
        _glu_proj_kernel,
        grid=(t_rows // ROW_TILE,),
        in_specs=[pl.BlockSpec((ROW_TILE, d), lambda i: (i, 0)), _const_spec((1, d)), _const_spec(w.shape)],
        out_specs=pl.BlockSpec((ROW_TILE, d), lambda i: (i, 0)),
        out_shape=jax.ShapeDtypeStruct((t_rows, d), BF16),
        compiler_params=_cparams("parallel"),
        name="norm_glu_proj",
    )(x2d, gain.reshape(1, d), w)


CONV_TILE = 256
SUBLANES = 8
CONV_PHASE_ROWS = CONV_TILE + CONV_HALO - SUBLANES


def _conv_kernel(u_ref, halo_ref, dw_ref, db_ref, lg_ref, lb_ref, o_ref, ext_ref, phase_ref):
    i = pl.program_id(1)
    halo = halo_ref[0].astype(F32)
    ext_ref[0:CONV_HALO, :] = jnp.where(i == 0, 0.0, halo)
    ext_ref[CONV_HALO:, :] = u_ref[0].astype(F32)
    for b in range(1, SUBLANES):
        phase_ref[b - 1] = ext_ref[pl.ds(b, CONV_PHASE_ROWS), :]
    lead = CONV_HALO - (CONV_WIDTH - 1)
    for r in range(CONV_TILE // CONV_ROWS):
        acc = jnp.zeros((CONV_ROWS, D_MODEL), F32) + db_ref[...]
        for w in range(CONV_WIDTH):
            shift = (lead + w) % SUBLANES
            start = r * CONV_ROWS + lead + w - shift
            src = ext_ref if shift == 0 else phase_ref.at[shift - 1]
            acc = acc + dw_ref[w:w + 1, :] * src[pl.ds(start, CONV_ROWS), :]
        mu = jnp.mean(acc, axis=-1, keepdims=True)
        cen = acc - mu
        var = jnp.mean(cen * cen, axis=-1, keepdims=True)
        un = cen * lax.rsqrt(var + NORM_EPS) * lg_ref[...] + lb_ref[...]
        o_ref[0, r * CONV_ROWS:(r + 1) * CONV_ROWS, :] = (un / (1.0 + jnp.exp(-un))).astype(o_ref.dtype)


def _conv_ln_swish(u, dw_w, dw_b, ln_g, ln_b):
    b, s, d = u.shape
    halo_per_tile = CONV_TILE // CONV_HALO
    dw = jnp.concatenate([dw_w, jnp.zeros((1, d), F32)], axis=0)
    return pl.pallas_call(
        _conv_kernel,
        grid=(b, s // CONV_TILE),
        in_specs=[
            pl.BlockSpec((1, CONV_TILE, d), lambda bi, i: (bi, i, 0)),
            pl.BlockSpec((1, CONV_HALO, d), lambda bi, i: (bi, jnp.maximum(i * halo_per_tile - 1, 0), 0)),
            _const_spec(dw.shape), _const_spec((1, d)), _const_spec((1, d)), _const_spec((1, d)),
        ],
        out_specs=pl.BlockSpec((1, CONV_TILE, d), lambda bi, i: (bi, i, 0)),
        out_shape=jax.ShapeDtypeStruct((b, s, d), BF16),
        scratch_shapes=[pltpu.VMEM((CONV_TILE + CONV_HALO, d), F32),
                        pltpu.VMEM((SUBLANES - 1, CONV_PHASE_ROWS, d), F32)],
        compiler_params=_cparams("parallel", "parallel"),
        name="conv_ln_swish",
    )(u, u, dw, dw_b.reshape(1, d), ln_g.reshape(1, d), ln_b.reshape(1, d))


def _rope_tables(seq):
    half = HEAD_DIM // 2
    inv_freq = ROPE_THETA ** (-jnp.arange(half, dtype=F32) / half)
    ang = jnp.arange(seq, dtype=F32)[:, None] * inv_freq[None, :]
    cos, sin = jnp.cos(ang), jnp.sin(ang)
    return jnp.concatenate([cos, cos], axis=1), jnp.concatenate([-sin, sin], axis=1)


def _nsa_mixer(x2d, norm_gain, w_in, q_gain, k_gain, cmp_pos, w_cmp, tables, batch, seq):
    g, dh = NSA_KV_GROUPS, HEAD_DIM
    n_main = (N_HEADS + 6 * g) * dh
    scale = dh ** -0.5 * LOG2_E
    w_main = w_in[:, :n_main].astype(BF16)
    hg = NSA_HEADS_PER_GROUP
    w_g = w_in[:, n_main:].reshape(-1, 3, g, hg).transpose(0, 2, 1, 3).reshape(-1, g, 3 * hg)
    w_gate = jnp.pad(w_g, ((0, 0), (0, 0), (0, LANES - 3 * hg))).reshape(-1, g * LANES).astype(BF16)
    ones = jnp.ones((dh,), F32)
    head_gains = jnp.stack([q_gain * scale] * N_HEADS + [k_gain[0]] * g + [ones] * g
                           + [k_gain[1]] * g + [ones] * g + [k_gain[2]] * g + [ones] * g)
    rope_blocks = [True] * N_HEADS + [True] * g + [False] * g + [True] * g + [False] * g + [True] * g + [False] * g
    kc_vc_chunk = N_HEADS * dh // PROJ_CHUNK
    assert 2 * g * dh == PROJ_CHUNK
    main, gate_logits, slabs = _project(x2d, norm_gain, w_main, *tables, head_gains, rope_blocks, seq,
                                        w_gate=w_gate, slab_chunk=kc_vc_chunk)
    main = main.reshape(batch, seq, n_main)
    gate_logits = gate_logits.reshape(batch, seq, g * LANES)

    kc_cmp, vc_cmp_t = _nsa_compress(slabs, batch, cmp_pos, w_cmp)
    out = _nsa_attention(main, gate_logits, kc_cmp, vc_cmp_t, _nsa_overlap_t(seq))
    return out.reshape(batch * seq, N_HEADS * dh)


def _qkv_weight_with_scaled_q(w_in):
    n_q = N_HEADS * HEAD_DIM
    return jnp.concatenate([w_in[:, :n_q] * HEAD_DIM ** -0.5, w_in[:, n_q:]], axis=1).astype(BF16)


def _sb_mixer(x2d, norm_gain, w_in, tables, batch, seq):
    n = 3 * N_HEADS * HEAD_DIM
    head_gains = jnp.ones((n // LANES, HEAD_DIM), F32)
    qkv = _project(x2d, norm_gain, _qkv_weight_with_scaled_q(w_in), *tables, head_gains,
                   [False] * (n // LANES), seq)
    return _sb_attention(qkv.reshape(batch, seq, n)).reshape(batch * seq, N_HEADS * HEAD_DIM)


def _conv_mixer(x2d, norm_gain, w_in, dw_w, dw_b, ln_g, ln_b, batch, seq):
    u = _glu_project(x2d, norm_gain, w_in.astype(BF16))
    a = _conv_ln_swish(u.reshape(batch, seq, D_MODEL), dw_w, dw_b, ln_g, ln_b)
    return a.reshape(batch * seq, D_MODEL)


def _moba_mixer(x2d, norm_gain, w_in, q_gain, k_gain, tables, batch, seq):
    n = 3 * N_HEADS * HEAD_DIM
    ones = jnp.ones((HEAD_DIM,), F32)
    head_gains = jnp.stack([q_gain * (HEAD_DIM ** -0.5 * LOG2_E)] * N_HEADS + [k_gain] * N_HEADS + [ones] * N_HEADS)
    rope_blocks = [True] * (2 * N_HEADS) + [False] * N_HEADS
    qkv = _project(x2d, norm_gain, w_in.astype(BF16), *tables, head_gains, rope_blocks, seq)
    return _moba_attention(qkv.reshape(batch, seq, n)).reshape(batch * seq, N_HEADS * HEAD_DIM)


def kernel(x, attn_norm, mlp_norm, mlp_w_up, mlp_w_down, nsa_w_in, nsa_q_norm, nsa_k_norm, nsa_cmp_pos, nsa_w_cmp, nsa_w_out, sb_w_in, sb_w_out, conv_w_in, conv_dw_w, conv_dw_b, conv_ln_g, conv_ln_b, conv_w_out, moba_w_in, moba_q_norm, moba_k_norm, moba_w_out):
    batch, seq, d = x.shape
    depth = attn_norm.shape[0]
    tables = _rope_tables(seq)
    x2d = x.reshape(batch * seq, d)
    for i in range(depth):
        m, j = i % 4, i // 4
        if m == 0:
            a = _nsa_mixer(x2d, attn_norm[i], nsa_w_in[j], nsa_q_norm[j], nsa_k_norm[j], nsa_cmp_pos[j],
                           nsa_w_cmp[j], tables, batch, seq)
            w_out = nsa_w_out[j]
        elif m == 1:
            a = _sb_mixer(x2d, attn_norm[i], sb_w_in[j], tables, batch, seq)
            w_out = sb_w_out[j]
        elif m == 2:
            a = _conv_mixer(x2d, attn_norm[i], conv_w_in[j], conv_dw_w[j], conv_dw_b[j], conv_ln_g[j],
                            conv_ln_b[j], batch, seq)
            w_out = conv_w_out[j]
        else:
            a = _moba_mixer(x2d, attn_norm[i], moba_w_in[j], moba_q_norm[j], moba_k_norm[j], tables, batch, seq)
            w_out = moba_w_out[j]
        x2d = _mixer_out_and_mlp(x2d, a, w_out.astype(BF16), mlp_norm[i], mlp_w_up[i].astype(BF16),
                                 mlp_w_down[i].astype(BF16))
    return x2d.reshape(batch, seq, d)
```

```python
import functools

import numpy as np
import jax
import jax.numpy as jnp
from jax import lax
from jax.experimental import pallas as pl
from jax.experimental.pallas import tpu as pltpu

F32 = jnp.float32
BF16 = jnp.bfloat16

D_MODEL = 1024
N_HEADS = 8
HEAD_DIM = 128
D_FF = 4 * D_MODEL
ROPE_THETA = 10000.0
NORM_EPS = 1e-6
NEG_INF = -1e30
LOG2_E = 1.4426950408889634

NSA_KV_GROUPS = 2
NSA_HEADS_PER_GROUP = N_HEADS // NSA_KV_GROUPS
NSA_CMP_BLOCK = 32
NSA_CMP_STRIDE = 16
NSA_SEL_BLOCK = 64
NSA_SEL_TOPK = 16
NSA_WINDOW = 512
NSA_FORCE_BONUS = 1000.0

CONV_WIDTH = 31
MOBA_BLOCK = 256
MOBA_TOPK = 3

LANES = 128
VMEM_LIMIT_BYTES = 56 * 1024 * 1024
ROW_TILE = 512
PROJ_CHUNK = 512
FF_CHUNK = 1024
CONV_HALO = 32
CONV_ROWS = 32


def _cparams(*sem):
    return pltpu.CompilerParams(dimension_semantics=sem, vmem_limit_bytes=VMEM_LIMIT_BYTES)


def _const_spec(shape):
    zeros = (0,) * len(shape)
    return pl.BlockSpec(shape, lambda *_: zeros, pipeline_mode=pl.Buffered(1))


def _rms(x, gain):
    return x * lax.rsqrt(jnp.mean(x * x, axis=-1, keepdims=True) + NORM_EPS) * gain


def _dot(a, b):
    return jnp.dot(a, b, preferred_element_type=F32)


def _dot_nt(a, b):
    return lax.dot_general(a, b, (((1,), (1,)), ((), ())), preferred_element_type=F32)


def _split_bf16(x):
    hi = x.astype(BF16)
    lo = (x - hi.astype(F32)).astype(BF16)
    return hi, lo


def _proj_kernel(*refs, rope_blocks, nsa_extras, slab_chunk):
    if nsa_extras:
        x_ref, g_ref, w_ref, cos_ref, sin_ref, hg_ref, wg_ref, o_ref, og_ref, oslab_ref, slab_ref = refs
    else:
        x_ref, g_ref, w_ref, cos_ref, sin_ref, hg_ref, o_ref = refs
    h = _rms(x_ref[...], g_ref[...]).astype(BF16)
    blocks_per_chunk = PROJ_CHUNK // LANES
    for c in range(len(rope_blocks) // blocks_per_chunk):
        y = _dot(h, w_ref[:, c * PROJ_CHUNK:(c + 1) * PROJ_CHUNK])
        chunk_flags = rope_blocks[c * blocks_per_chunk:(c + 1) * blocks_per_chunk]
        keep_slabs = nsa_extras and c == slab_chunk
        if not any(chunk_flags) and not keep_slabs:
            o_ref[:, c * PROJ_CHUNK:(c + 1) * PROJ_CHUNK] = y.astype(o_ref.dtype)
            continue
        for k, flag in enumerate(chunk_flags):
            b = c * blocks_per_chunk + k
            yb = y[:, k * LANES:(k + 1) * LANES]
            if flag:
                yb = _rms(yb, hg_ref[b:b + 1, :])
                yb = yb * cos_ref[...] + pltpu.roll(yb, HEAD_DIM // 2, 1) * sin_ref[...]
            o_ref[:, b * LANES:(b + 1) * LANES] = yb.astype(o_ref.dtype)
            if keep_slabs:
                slab_ref[k] = yb
    if nsa_extras:
        og_ref[...] = _dot(h, wg_ref[...])
        st = NSA_CMP_STRIDE
        for l in range(st):
            for k in range(blocks_per_chunk):
                rows_l = slab_ref[k, pl.ds(l, ROW_TILE // st, stride=st), :]
                oslab_ref[l, :, k * LANES:(k + 1) * LANES] = rows_l.astype(oslab_ref.dtype)


def _project(x2d, gain, w, cos_t, sin_t, head_gains, rope_blocks, seq, w_gate=None, slab_chunk=None):
    t_rows, d = x2d.shape
    n = w.shape[1]
    assert n % PROJ_CHUNK == 0 and len(rope_blocks) == n // LANES and seq % ROW_TILE == 0
    seq_tiles = seq // ROW_TILE
    has_gate = w_gate is not None
    in_specs = [
        pl.BlockSpec((ROW_TILE, d), lambda i: (i, 0)),
        _const_spec((1, d)),
        _const_spec((d, n)),
        pl.BlockSpec((ROW_TILE, LANES), lambda i: (i % seq_tiles, 0)),
        pl.BlockSpec((ROW_TILE, LANES), lambda i: (i % seq_tiles, 0)),
        _const_spec(head_gains.shape),
    ]
    args = [x2d, gain.reshape(1, d), w, cos_t, sin_t, head_gains]
    out_shape = [jax.ShapeDtypeStruct((t_rows, n), BF16)]
    out_specs = [pl.BlockSpec((ROW_TILE, n), lambda i: (i, 0))]
    if has_gate:
        in_specs.append(_const_spec(w_gate.shape))
        args.append(w_gate)
        out_shape.append(jax.ShapeDtypeStruct((t_rows, w_gate.shape[1]), F32))
        out_specs.append(pl.BlockSpec((ROW_TILE, w_gate.shape[1]), lambda i: (i, 0)))
        st = NSA_CMP_STRIDE
        out_shape.append(jax.ShapeDtypeStruct((st, t_rows // st, PROJ_CHUNK), BF16))
        out_specs.append(pl.BlockSpec((st, ROW_TILE // st, PROJ_CHUNK), lambda i: (0, i, 0)))
    outs = pl.pallas_call(
        functools.partial(_proj_kernel, rope_blocks=tuple(rope_blocks), nsa_extras=has_gate,
                          slab_chunk=slab_chunk),
        grid=(t_rows // ROW_TILE,),
        in_specs=in_specs,
        out_specs=out_specs,
        out_shape=out_shape,
        scratch_shapes=[pltpu.VMEM((PROJ_CHUNK // LANES, ROW_TILE, LANES), F32)] if has_gate else [],
        compiler_params=_cparams("parallel"),
        name="norm_proj",
    )(*args)
    return outs if has_gate else outs[0]


def _mlp_kernel(x_ref, a_ref, wo_ref, g_ref, wup_ref, wdn_ref, o_ref):
    x1 = x_ref[...] + _dot(a_ref[...], wo_ref[...])
    h = _rms(x1, g_ref[...]).astype(BF16)
    acc = x1
    for c in range(D_FF // FF_CHUNK):
        u = _dot(h, wup_ref[0, :, c * FF_CHUNK:(c + 1) * FF_CHUNK])
        act = jnp.square(jnp.maximum(u, 0.0)).astype(BF16)
        acc = acc + _dot(act, wdn_ref[0, c * FF_CHUNK:(c + 1) * FF_CHUNK, :])
    o_ref[...] = acc


def _mixer_out_and_mlp(x2d, a2d, w_out, gain, w_up_all, w_down_all, layer):
    t_rows, d = x2d.shape

    def layer_spec(shape):
        return pl.BlockSpec((1,) + shape, lambda i: (layer, 0, 0), pipeline_mode=pl.Buffered(1))

    return pl.pallas_call(
        _mlp_kernel,
        grid=(t_rows // ROW_TILE,),
        in_specs=[
            pl.BlockSpec((ROW_TILE, d), lambda i: (i, 0)),
            pl.BlockSpec((ROW_TILE, d), lambda i: (i, 0)),
            _const_spec((d, d)),
            _const_spec((1, d)),
            layer_spec((d, D_FF)),
            layer_spec((D_FF, d)),
        ],
        out_specs=pl.BlockSpec((ROW_TILE, d), lambda i: (i, 0)),
        out_shape=jax.ShapeDtypeStruct((t_rows, d), F32),
        compiler_params=_cparams("parallel"),
        name="outproj_mlp",
    )(x2d, a2d, w_out, gain.reshape(1, d), w_up_all, w_down_all)


SB_TQ = 1024
SB_TK = 256
SB_UNDERFLOW_LOG = -104.0


def _sb_kernel(q_ref, k_ref, v_ref, o_ref):
    i = pl.program_id(2)
    tq, tk = SB_TQ, SB_TK
    own_blocks = tq // tk
    suffix_ones = (lax.broadcasted_iota(jnp.int32, (tk, tk), 0)
                   > lax.broadcasted_iota(jnp.int32, (tk, tk), 1)).astype(BF16)

    def block(j, q, carry, diagonal):
        o, later = carry
        start = pl.multiple_of(j * tk, tk)
        kj = k_ref[0, pl.ds(start, tk), :]
        vj = v_ref[0, pl.ds(start, tk), :]
        z = _dot_nt(q, kj)
        softplus = jnp.maximum(z, 0.0) + jnp.log(1.0 + jnp.exp(-jnp.abs(z)))
        log_om = -softplus
        if diagonal:
            below = (lax.broadcasted_iota(jnp.int32, z.shape, 0)
                     > lax.broadcasted_iota(jnp.int32, z.shape, 1))
            log_om = jnp.where(below, log_om, 0.0)
        hi, lo = _split_bf16(log_om)
        between = _dot(hi, suffix_ones) + _dot(lo, suffix_ones) + later
        a = jnp.exp(z - softplus + between)
        if diagonal:
            a = jnp.where(below, a, 0.0)
        o = o + _dot(a.astype(BF16), vj)
        later = later + jnp.sum(log_om, axis=1, keepdims=True)
        return o, later

    o = jnp.zeros((tq, HEAD_DIM), F32)
    later = jnp.zeros((tq, 1), F32)
    for b in reversed(range(own_blocks)):
        r0 = b * tk
        o_b, later_b = block(i * own_blocks + b, q_ref[0, r0:, :], (o[r0:], later[r0:]), True)
        if b > 0:
            o = jnp.concatenate([o[:r0], o_b], axis=0)
            later = jnp.concatenate([later[:r0], later_b], axis=0)
        else:
            o, later = o_b, later_b

    def more(state):
        j, _, later = state
        return (j >= 0) & (jnp.max(later) > SB_UNDERFLOW_LOG)

    def step(state):
        j, o, later = state
        o, later = block(j, q_ref[0], (o, later), False)
        return j - 1, o, later

    _, o, _ = lax.while_loop(more, step, (i * own_blocks - 1, o, later))
    o_ref[0] = o.astype(o_ref.dtype)


def _sb_attention(qkv):
    b, s, _ = qkv.shape
    h = N_HEADS
    return pl.pallas_call(
        _sb_kernel,
        grid=(b, h, s // SB_TQ),
        in_specs=[
            pl.BlockSpec((1, SB_TQ, LANES), lambda bi, hi, i: (bi, i, hi)),
            pl.BlockSpec((1, s, LANES), lambda bi, hi, i: (bi, 0, h + hi)),
            pl.BlockSpec((1, s, LANES), lambda bi, hi, i: (bi, 0, 2 * h + hi)),
        ],
        out_specs=pl.BlockSpec((1, SB_TQ, LANES), lambda bi, hi, i: (bi, i, hi)),
        out_shape=jax.ShapeDtypeStruct((b, s, h * HEAD_DIM), BF16),
        compiler_params=_cparams("parallel", "parallel", "arbitrary"),
        name="stick_breaking_attn",
    )(qkv, qkv, qkv)


MOBA_TQ = 2048
MOBA_TK = 1024
MOBA_OWN_TK = 512
ONES_ROWS = 16


def _online_softmax_step_t(carry, s_t, v_t):
    m, acc = carry
    m_new = jnp.maximum(m, jnp.max(s_t, axis=0, keepdims=True))
    p_t = jnp.exp2(s_t - m_new)
    acc = jnp.exp2(m - m_new) * acc + _dot(v_t, p_t.astype(BF16))
    return m_new, acc


def _flash_loop_t(n_steps, q_aug, keys_fn, values_t_fn, carry):
    def body(n, carry):
        return _online_softmax_step_t(carry, _dot_nt(keys_fn(n), q_aug), values_t_fn(n))

    return lax.fori_loop(0, n_steps, body, carry)


def _transposed_values_with_ones(v):
    v_t = v.astype(F32).T.astype(BF16)
    return jnp.concatenate([v_t, jnp.ones((ONES_ROWS, v.shape[0]), BF16)], axis=0)


def _moba_kernel(q_ref, k_ref, v_ref, o_ref, kaug_ref, kmean_ref, vt_ref, *, n_blk):
    i = pl.program_id(2)
    t, bs = MOBA_TQ, MOBA_BLOCK
    s_len = k_ref.shape[1]

    @pl.when(i == 0)
    def _():
        k = k_ref[0]
        kaug_ref[:, :LANES] = k
        blk = lax.broadcasted_iota(jnp.int32, (s_len, LANES), 0) // bs
        lane = lax.broadcasted_iota(jnp.int32, (s_len, LANES), 1)
        kaug_ref[:, LANES:] = (blk == lane).astype(BF16)
        kmean_ref[...] = jnp.mean(k.astype(F32).reshape(n_blk, bs, LANES), axis=1)
        for c in range(s_len // MOBA_TK):
            vt_ref[c] = _transposed_values_with_ones(v_ref[0, c * MOBA_TK:(c + 1) * MOBA_TK, :])

    q = q_ref[0]
    km_hi, km_lo = _split_bf16(kmean_ref[...])
    gate = _dot_nt(km_hi, q) + _dot_nt(km_lo, q)
    blk_id = lax.broadcasted_iota(jnp.int32, (n_blk, t), 0)
    cur = i * (t // bs) + lax.broadcasted_iota(jnp.int32, (n_blk, t), 1) // bs
    past = blk_id < cur
    gate = jnp.where(past, gate, -jnp.inf)
    visible = (_top_k_mask_t(gate, MOBA_TOPK) & past) | (blk_id == cur)
    sel_bias = jnp.where(visible, 0.0, NEG_INF)
    sel_bias = jnp.concatenate([sel_bias, jnp.zeros((LANES - n_blk, t), F32)], axis=0)
    q_aug = jnp.concatenate([q, sel_bias.T.astype(BF16)], axis=1)

    tk = MOBA_TK

    def keys(n):
        return kaug_ref[pl.ds(pl.multiple_of(n * tk, tk), tk), :]

    def values_t(n):
        return vt_ref[n]

    own = i * (t // tk)
    to = MOBA_OWN_TK
    causal = (lax.broadcasted_iota(jnp.int32, (to, t), 0) <= lax.broadcasted_iota(jnp.int32, (to, t), 1))
    carry = None
    for u in range(t // to):
        r0 = u * to
        k_u = kaug_ref[pl.ds(pl.multiple_of(i * t + r0, to), to), :]
        v_u = vt_ref[own + r0 // tk][:, r0 % tk:r0 % tk + to]
        s_t = jnp.where(causal[:, :t - r0], _dot_nt(k_u, q_aug[r0:]), NEG_INF)
        if carry is None:
            m0 = jnp.max(s_t, axis=0, keepdims=True)
            carry = (m0, _dot(v_u, jnp.exp2(s_t - m0).astype(BF16)))
        else:
            m_u, acc_u = _online_softmax_step_t((carry[0][:, r0:], carry[1][:, r0:]), s_t, v_u)
            carry = (jnp.concatenate([carry[0][:, :r0], m_u], axis=1),
                     jnp.concatenate([carry[1][:, :r0], acc_u], axis=1))

    _, acc = _flash_loop_t(own, q_aug, keys, values_t, carry)
    o_t = acc[:HEAD_DIM] * (1.0 / acc[HEAD_DIM:HEAD_DIM + 1])
    o_ref[0] = o_t.T.astype(o_ref.dtype)


def _moba_attention(qkv):
    b, s, _ = qkv.shape
    h = N_HEADS
    assert s % MOBA_TQ == 0 and MOBA_TQ % MOBA_BLOCK == 0
    n_blk = s // MOBA_BLOCK
    assert n_blk % 8 == 0 and n_blk <= LANES
    return pl.pallas_call(
        functools.partial(_moba_kernel, n_blk=n_blk),
        grid=(b, h, s // MOBA_TQ),
        in_specs=[
            pl.BlockSpec((1, MOBA_TQ, LANES), lambda bi, hi, i: (bi, i, hi)),
            pl.BlockSpec((1, s, LANES), lambda bi, hi, i: (bi, 0, h + hi)),
            pl.BlockSpec((1, s, LANES), lambda bi, hi, i: (bi, 0, 2 * h + hi)),
        ],
        out_specs=pl.BlockSpec((1, MOBA_TQ, LANES), lambda bi, hi, i: (bi, i, hi)),
        out_shape=jax.ShapeDtypeStruct((b, s, h * HEAD_DIM), BF16),
        scratch_shapes=[pltpu.VMEM((s, 2 * LANES), BF16), pltpu.VMEM((n_blk, LANES), F32),
                        pltpu.VMEM((s // MOBA_TK, HEAD_DIM + ONES_ROWS, MOBA_TK), BF16)],
        compiler_params=_cparams("parallel", "parallel", "arbitrary"),
        name="moba_attn",
    )(qkv, qkv, qkv)


NSA_TQ = 512
NSA_TK = 1024
def _cmp_kernel(*refs):
    st = NSA_CMP_STRIDE
    x_refs, (pos_ref, w_ref, ok_ref, ov_ref) = refs[:2 * st], refs[2 * st:]
    n16 = ok_ref.shape[2]
    for t, o_ref in enumerate((ok_ref, ov_ref)):
        first = jnp.zeros((n16, HEAD_DIM), F32)
        second = jnp.zeros((n16, HEAD_DIM), F32)
        for l in range(st):
            x = x_refs[t * st + l][0].astype(F32)
            first = first + _dot((x + pos_ref[t, l:l + 1, :]).astype(BF16), w_ref[t, l])
            second = second + _dot((x + pos_ref[t, st + l:st + l + 1, :]).astype(BF16), w_ref[t, st + l])
        out = first + pltpu.roll(second, n16 - 1, 0)
        o_ref[0, 0] = (out if t == 0 else out.T).astype(o_ref.dtype)


def _nsa_compress(slabs, batch, cmp_pos, w_cmp):
    g, st = NSA_KV_GROUPS, NSA_CMP_STRIDE
    b = batch
    n16 = slabs.shape[1] // b

    def slab_spec(first_block, l):
        return pl.BlockSpec((1, n16, LANES), lambda bi, gi: (l, bi, first_block + gi))

    x_specs = [slab_spec(t * g, l) for t in range(2) for l in range(st)]
    pos = cmp_pos.astype(F32)
    w = w_cmp.astype(BF16)
    return pl.pallas_call(
        _cmp_kernel,
        grid=(b, g),
        in_specs=x_specs + [_const_spec(pos.shape), _const_spec(w.shape)],
        out_specs=[pl.BlockSpec((1, 1, n16, HEAD_DIM), lambda bi, gi: (bi, gi, 0, 0)),
                   pl.BlockSpec((1, 1, HEAD_DIM, n16), lambda bi, gi: (bi, gi, 0, 0))],
        out_shape=[jax.ShapeDtypeStruct((b, g, n16, HEAD_DIM), BF16),
                   jax.ShapeDtypeStruct((b, g, HEAD_DIM, n16), BF16)],
        compiler_params=_cparams("parallel", "parallel"),
        name="nsa_compress",
    )(*([slabs] * (2 * st)), pos, w)


def _top_k_mask_t(score_t, k):
    n, q = score_t.shape
    work = score_t
    taken = jnp.zeros((1, q), F32)
    level = jnp.full((1, q), jnp.inf, F32)
    above = jnp.zeros((1, q), F32)
    for _ in range(k):
        best = jnp.max(work, axis=0, keepdims=True)
        hit = work == best
        active = taken < k
        level = jnp.where(active, best, level)
        above = jnp.where(active, taken, above)
        taken = taken + jnp.sum(hit.astype(F32), axis=0, keepdims=True)
        work = jnp.where(hit, -jnp.inf, work)
    tie = score_t == level
    lower = (lax.broadcasted_iota(jnp.int32, (n, n), 1)
             < lax.broadcasted_iota(jnp.int32, (n, n), 0)).astype(BF16)
    ties_before = _dot(lower, tie.astype(BF16))
    return (score_t > level) | (tie & (ties_before < k - above))


def _nsa_kernel(q_ref, kc_ref, vc_ref, ks_ref, vs_ref, kw_ref, vw_ref, gate_ref, ov_ref, o_ref,
                ksaug_ref, vst_ref, vwt_ref, *, n_sel):
    i = pl.program_id(2)
    tq, tk, hg = NSA_TQ, NSA_TK, NSA_HEADS_PER_GROUP
    s_len = ks_ref.shape[1]
    n_cmp_pad = kc_ref.shape[2]
    c0 = i * tq

    @pl.when(i == 0)
    def _():
        ksaug_ref[:, :LANES] = ks_ref[0]
        blk = lax.broadcasted_iota(jnp.int32, (s_len, LANES), 0) // NSA_SEL_BLOCK
        lane = lax.broadcasted_iota(jnp.int32, (s_len, LANES), 1)
        ksaug_ref[:, LANES:] = (blk == lane).astype(BF16)
        for c in range(s_len // tk):
            vst_ref[c] = _transposed_values_with_ones(vs_ref[0, c * tk:(c + 1) * tk, :])
        for c in range(s_len // tq):
            vwt_ref[c] = _transposed_values_with_ones(vw_ref[0, c * tq:(c + 1) * tq, :])

    q_all = q_ref[0]
    q_heads = [q_all[:, h * LANES:(h + 1) * LANES] for h in range(hg)]

    def key_iota(n):
        return lax.broadcasted_iota(jnp.int32, (n, tq), 0)

    def q_pos(n):
        return c0 + lax.broadcasted_iota(jnp.int32, (n, tq), 1)

    kc = kc_ref[0, 0]
    cmp_mask = key_iota(n_cmp_pad) * NSA_CMP_STRIDE + (NSA_CMP_BLOCK - 1) <= q_pos(n_cmp_pad)
    o_c, p_sum = [], jnp.zeros((n_cmp_pad, tq), F32)
    for qh in q_heads:
        z = jnp.where(cmp_mask, _dot_nt(kc, qh), NEG_INF)
        e = jnp.where(cmp_mask, jnp.exp2(z - jnp.max(z, axis=0, keepdims=True)), 0.0)
        l = jnp.sum(e, axis=0, keepdims=True)
        inv = 1.0 / jnp.where(l > 0.0, l, 1.0)
        o_c.append(_dot(vc_ref[0, 0], e.astype(BF16)) * inv)
        p_sum = p_sum + e * inv

    ps_hi, ps_lo = _split_bf16(p_sum)
    imp = _dot(ov_ref[...], ps_hi) + _dot(ov_ref[...], ps_lo)
    blk = key_iota(LANES)
    cur = q_pos(LANES) // NSA_SEL_BLOCK
    forced = (blk == 0) | (blk == cur) | (blk == cur - 1)
    score = jnp.where(blk <= cur, imp + NSA_FORCE_BONUS * forced.astype(F32), -1.0)
    score = jnp.where(blk < n_sel, score, -jnp.inf)
    chosen = _top_k_mask_t(score, min(NSA_SEL_TOPK, n_sel))
    sel_bias = jnp.where(chosen, 0.0, NEG_INF).T.astype(BF16)
    q_aug = jnp.concatenate([jnp.concatenate(q_heads, axis=0),
                             jnp.concatenate([sel_bias] * hg, axis=0)], axis=1)

    def keys(j):
        return ksaug_ref[pl.ds(pl.multiple_of(j * tk, tk), tk), :]

    def values_t(j):
        return vst_ref[j]

    jd = c0 // tk
    causal = jd * tk + key_iota(tk) <= q_pos(tk)
    s_t = jnp.where(jnp.concatenate([causal] * hg, axis=1), _dot_nt(keys(jd), q_aug), NEG_INF)
    m0 = jnp.max(s_t, axis=0, keepdims=True)
    carry = (m0, _dot(values_t(jd), jnp.exp2(s_t - m0).astype(BF16)))
    _, acc = _flash_loop_t(jd, q_aug, keys, values_t, carry)
    o_s = acc[:HEAD_DIM] * (1.0 / acc[HEAD_DIM:HEAD_DIM + 1])

    span = tq + NSA_WINDOW
    wstart = pl.multiple_of(jnp.maximum(c0 - NSA_WINDOW, 0), tq)
    w_chunk = wstart // tq
    gap = q_pos(span) - (wstart + key_iota(span))
    win_mask = (gap >= 0) & (gap < NSA_WINDOW)
    kw = kw_ref[0, pl.ds(wstart, span), :]
    o_w = []
    for qh in q_heads:
        z = jnp.where(win_mask, _dot_nt(kw, qh), NEG_INF)
        e = jnp.exp2(z - jnp.max(z, axis=0, keepdims=True)).astype(BF16)
        acc_w = _dot(vwt_ref[w_chunk], e[0:tq])
        for c in range(1, span // tq):
            acc_w = acc_w + _dot(vwt_ref[w_chunk + c], e[c * tq:(c + 1) * tq])
        o_w.append(acc_w[:HEAD_DIM] * (1.0 / acc_w[HEAD_DIM:HEAD_DIM + 1]))

    gates_t = (1.0 / (1.0 + jnp.exp(-gate_ref[0]))).T
    for h in range(hg):
        branches = (o_c[h], o_s[:, h * tq:(h + 1) * tq], o_w[h])
        out = jnp.zeros((HEAD_DIM, tq), F32)
        for branch, o_b in enumerate(branches):
            r = branch * hg + h
            out = out + gates_t[r:r + 1, :] * o_b
        o_ref[0, :, h * LANES:(h + 1) * LANES] = out.T.astype(o_ref.dtype)


def _nsa_attention(main, gate_logits, kc_cmp, vc_cmp_t, overlap_t):
    b, s, _ = main.shape
    g, hg = NSA_KV_GROUPS, NSA_HEADS_PER_GROUP
    n_sel = s // NSA_SEL_BLOCK
    assert s % NSA_TK == 0 and n_sel <= LANES
    n16 = kc_cmp.shape[2]
    q_blocks = N_HEADS

    def kv_spec(which):
        return pl.BlockSpec((1, s, LANES), lambda bi, gi, i: (bi, 0, q_blocks + which * g + gi),
                            pipeline_mode=pl.Buffered(1))

    return pl.pallas_call(
        functools.partial(_nsa_kernel, n_sel=n_sel),
        grid=(b, g, s // NSA_TQ),
        in_specs=[
            pl.BlockSpec((1, NSA_TQ, hg * LANES), lambda bi, gi, i: (bi, i, gi)),
            pl.BlockSpec((1, 1, n16, HEAD_DIM), lambda bi, gi, i: (bi, gi, 0, 0)),
            pl.BlockSpec((1, 1, HEAD_DIM, n16), lambda bi, gi, i: (bi, gi, 0, 0)),
            kv_spec(2), kv_spec(3), kv_spec(4), kv_spec(5),
            pl.BlockSpec((1, NSA_TQ, LANES), lambda bi, gi, i: (bi, i, gi)),
            _const_spec(overlap_t.shape),
        ],
        out_specs=pl.BlockSpec((1, NSA_TQ, hg * LANES), lambda bi, gi, i: (bi, i, gi)),
        out_shape=jax.ShapeDtypeStruct((b, s, N_HEADS * HEAD_DIM), BF16),
        scratch_shapes=[pltpu.VMEM((s, 2 * LANES), BF16),
                        pltpu.VMEM((s // NSA_TK, HEAD_DIM + ONES_ROWS, NSA_TK), BF16),
                        pltpu.VMEM((s // NSA_TQ, HEAD_DIM + ONES_ROWS, NSA_TQ), BF16)],
        compiler_params=_cparams("parallel", "parallel", "arbitrary"),
        name="nsa_attn",
    )(main, kc_cmp, vc_cmp_t, main, main, main, main, gate_logits, overlap_t)


def _nsa_overlap_t(seq):
    n_cmp = (seq - NSA_CMP_BLOCK) // NSA_CMP_STRIDE + 1
    n_sel = seq // NSA_SEL_BLOCK
    cmp_start = np.arange(seq // NSA_CMP_STRIDE) * NSA_CMP_STRIDE
    sel_start = np.arange(LANES) * NSA_SEL_BLOCK
    ov = ((cmp_start[:, None] < sel_start[None, :] + NSA_SEL_BLOCK)
          & (cmp_start[:, None] + NSA_CMP_BLOCK > sel_start[None, :]))
    ov &= (np.arange(seq // NSA_CMP_STRIDE)[:, None] < n_cmp) & (np.arange(LANES)[None, :] < n_sel)
    return jnp.asarray(ov.T, dtype=BF16)


def _glu_proj_kernel(x_ref, g_ref, w_ref, o_ref):
    h = _rms(x_ref[...], g_ref[...]).astype(BF16)
    d = o_ref.shape[1]
    for c in range(d // PROJ_CHUNK):
        a = _dot(h, w_ref[:, c * PROJ_CHUNK:(c + 1) * PROJ_CHUNK])
        gate = _dot(h, w_ref[:, d + c * PROJ_CHUNK:d + (c + 1) * PROJ_CHUNK])
        o_ref[:, c * PROJ_CHUNK:(c + 1) * PROJ_CHUNK] = (a / (1.0 + jnp.exp(-gate))).astype(o_ref.dtype)


def _glu_project(x2d, gain, w):
    t_rows, d = x2d.shape
    return pl.pallas_call(
        _glu_proj_kernel,
        grid=(t_rows // ROW_TILE,),
        in_specs=[pl.BlockSpec((ROW_TILE, d), lambda i: (i, 0)), _const_spec((1, d)), _const_spec(w.shape)],
        out_specs=pl.BlockSpec((ROW_TILE, d), lambda i: (i, 0)),
        out_shape=jax.ShapeDtypeStruct((t_rows, d), BF16),
        compiler_params=_cparams("parallel"),
        name="norm_glu_proj",
    )(x2d, gain.reshape(1, d), w)


CONV_TILE = 256
SUBLANES = 8
CONV_PHASE_ROWS = CONV_TILE + CONV_HALO - SUBLANES


def _conv_kernel(u_ref, halo_ref, dw_ref, db_ref, lg_ref, lb_ref, o_ref, ext_ref, phase_ref):
    i = pl.program_id(1)
    halo = halo_ref[0].astype(F32)
    ext_ref[0:CONV_HALO, :] = jnp.where(i == 0, 0.0, halo)
    ext_ref[CONV_HALO:, :] = u_ref[0].astype(F32)
    for b in range(1, SUBLANES):
        phase_ref[b - 1] = ext_ref[pl.ds(b, CONV_PHASE_ROWS), :]
    lead = CONV_HALO - (CONV_WIDTH - 1)
    for r in range(CONV_TILE // CONV_ROWS):
        acc = jnp.zeros((CONV_ROWS, D_MODEL), F32) + db_ref[...]
        for w in range(CONV_WIDTH):
            shift = (lead + w) % SUBLANES
            start = r * CONV_ROWS + lead + w - shift
            src = ext_ref if shift == 0 else phase_ref.at[shift - 1]
            acc = acc + dw_ref[w:w + 1, :] * src[pl.ds(start, CONV_ROWS), :]
        mu = jnp.mean(acc, axis=-1, keepdims=True)
        cen = acc - mu
        var = jnp.mean(cen * cen, axis=-1, keepdims=True)
        un = cen * lax.rsqrt(var + NORM_EPS) * lg_ref[...] + lb_ref[...]
        o_ref[0, r * CONV_ROWS:(r + 1) * CONV_ROWS, :] = (un / (1.0 + jnp.exp(-un))).astype(o_ref.dtype)


def _conv_ln_swish(u, dw_w, dw_b, ln_g, ln_b):
    b, s, d = u.shape
    halo_per_tile = CONV_TILE // CONV_HALO
    dw = jnp.concatenate([dw_w, jnp.zeros((1, d), F32)], axis=0)
    return pl.pallas_call(
        _conv_kernel,
        grid=(b, s // CONV_TILE),
        in_specs=[
            pl.BlockSpec((1, CONV_TILE, d), lambda bi, i: (bi, i, 0)),
            pl.BlockSpec((1, CONV_HALO, d), lambda bi, i: (bi, jnp.maximum(i * halo_per_tile - 1, 0), 0)),
            _const_spec(dw.shape), _const_spec((1, d)), _const_spec((1, d)), _const_spec((1, d)),
        ],
        out_specs=pl.BlockSpec((1, CONV_TILE, d), lambda bi, i: (bi, i, 0)),
        out_shape=jax.ShapeDtypeStruct((b, s, d), BF16),
        scratch_shapes=[pltpu.VMEM((CONV_TILE + CONV_HALO, d), F32),
                        pltpu.VMEM((SUBLANES - 1, CONV_PHASE_ROWS, d), F32)],
        compiler_params=_cparams("parallel", "parallel"),
        name="conv_ln_swish",
    )(u, u, dw, dw_b.reshape(1, d), ln_g.reshape(1, d), ln_b.reshape(1, d))


def _rope_tables(seq):
    half = HEAD_DIM // 2
    inv_freq = ROPE_THETA ** (-jnp.arange(half, dtype=F32) / half)
    ang = jnp.arange(seq, dtype=F32)[:, None] * inv_freq[None, :]
    cos, sin = jnp.cos(ang), jnp.sin(ang)
    return jnp.concatenate([cos, cos], axis=1), jnp.concatenate([-sin, sin], axis=1)


def _nsa_mixer(x2d, norm_gain, w_in, q_gain, k_gain, cmp_pos, w_cmp, tables, batch, seq):
    g, dh = NSA_KV_GROUPS, HEAD_DIM
    n_main = (N_HEADS + 6 * g) * dh
    scale = dh ** -0.5 * LOG2_E
    w_main = w_in[:, :n_main].astype(BF16)
    hg = NSA_HEADS_PER_GROUP
    w_g = w_in[:, n_main:].reshape(-1, 3, g, hg).transpose(0, 2, 1, 3).reshape(-1, g, 3 * hg)
    w_gate = jnp.pad(w_g, ((0, 0), (0, 0), (0, LANES - 3 * hg))).reshape(-1, g * LANES).astype(BF16)
    ones = jnp.ones((dh,), F32)
    head_gains = jnp.stack([q_gain * scale] * N_HEADS + [k_gain[0]] * g + [ones] * g
                           + [k_gain[1]] * g + [ones] * g + [k_gain[2]] * g + [ones] * g)
    rope_blocks = [True] * N_HEADS + [True] * g + [False] * g + [True] * g + [False] * g + [True] * g + [False] * g
    kc_vc_chunk = N_HEADS * dh // PROJ_CHUNK
    assert 2 * g * dh == PROJ_CHUNK
    main, gate_logits, slabs = _project(x2d, norm_gain, w_main, *tables, head_gains, rope_blocks, seq,
                                        w_gate=w_gate, slab_chunk=kc_vc_chunk)
    main = main.reshape(batch, seq, n_main)
    gate_logits = gate_logits.reshape(batch, seq, g * LANES)

    kc_cmp, vc_cmp_t = _nsa_compress(slabs, batch, cmp_pos, w_cmp)
    out = _nsa_attention(main, gate_logits, kc_cmp, vc_cmp_t, _nsa_overlap_t(seq))
    return out.reshape(batch * seq, N_HEADS * dh)


def _qkv_weight_with_scaled_q(w_in):
    n_q = N_HEADS * HEAD_DIM
    return jnp.concatenate([w_in[:, :n_q] * HEAD_DIM ** -0.5, w_in[:, n_q:]], axis=1).astype(BF16)


def _sb_mixer(x2d, norm_gain, w_in, tables, batch, seq):
    n = 3 * N_HEADS * HEAD_DIM
    head_gains = jnp.ones((n // LANES, HEAD_DIM), F32)
    qkv = _project(x2d, norm_gain, _qkv_weight_with_scaled_q(w_in), *tables, head_gains,
                   [False] * (n // LANES), seq)
    return _sb_attention(qkv.reshape(batch, seq, n)).reshape(batch * seq, N_HEADS * HEAD_DIM)


def _conv_mixer(x2d, norm_gain, w_in, dw_w, dw_b, ln_g, ln_b, batch, seq):
    u = _glu_project(x2d, norm_gain, w_in.astype(BF16))
    a = _conv_ln_swish(u.reshape(batch, seq, D_MODEL), dw_w, dw_b, ln_g, ln_b)
    return a.reshape(batch * seq, D_MODEL)


def _moba_mixer(x2d, norm_gain, w_in, q_gain, k_gain, tables, batch, seq):
    n = 3 * N_HEADS * HEAD_DIM
    ones = jnp.ones((HEAD_DIM,), F32)
    head_gains = jnp.stack([q_gain * (HEAD_DIM ** -0.5 * LOG2_E)] * N_HEADS + [k_gain] * N_HEADS + [ones] * N_HEADS)
    rope_blocks = [True] * (2 * N_HEADS) + [False] * N_HEADS
    qkv = _project(x2d, norm_gain, w_in.astype(BF16), *tables, head_gains, rope_blocks, seq)
    return _moba_attention(qkv.reshape(batch, seq, n)).reshape(batch * seq, N_HEADS * HEAD_DIM)


def kernel(x, attn_norm, mlp_norm, mlp_w_up, mlp_w_down, nsa_w_in, nsa_q_norm, nsa_k_norm, nsa_cmp_pos, nsa_w_cmp, nsa_w_out, sb_w_in, sb_w_out, conv_w_in, conv_dw_w, conv_dw_b, conv_ln_g, conv_ln_b, conv_w_out, moba_w_in, moba_q_norm, moba_k_norm, moba_w_out):
    batch, seq, d = x.shape
    depth = attn_norm.shape[0]
    tables = _rope_tables(seq)
    w_up_all, w_down_all = mlp_w_up.astype(BF16), mlp_w_down.astype(BF16)
    x2d = x.reshape(batch * seq, d)
    for i in range(depth):
        m, j = i % 4, i // 4
        if m == 0:
            a = _nsa_mixer(x2d, attn_norm[i], nsa_w_in[j], nsa_q_norm[j], nsa_k_norm[j], nsa_cmp_pos[j],
                           nsa_w_cmp[j], tables, batch, seq)
            w_out = nsa_w_out[j]
        elif m == 1:
            a = _sb_mixer(x2d, attn_norm[i], sb_w_in[j], tables, batch, seq)
            w_out = sb_w_out[j]
        elif m == 2:
            a = _conv_mixer(x2d, attn_norm[i], conv_w_in[j], conv_dw_w[j], conv_dw_b[j], conv_ln_g[j],
                            conv_ln_b[j], batch, seq)
            w_out = conv_w_out[j]
        else:
            a = _moba_mixer(x2d, attn_norm[i], moba_w_in[j], moba_q_norm[j], moba_k_norm[j], tables, batch, seq)
            w_out = moba_w_out[j]
        x2d = _mixer_out_and_mlp(x2d, a, w_out.astype(BF16), mlp_norm[i], w_up_all, w_down_all, i)
    return x2d.reshape(batch, seq, d)
```

```python
import functools

import numpy as np
import jax
import jax.numpy as jnp
from jax import lax
from jax.experimental import pallas as pl
from jax.experimental.pallas import tpu as pltpu

F32 = jnp.float32
BF16 = jnp.bfloat16

D_MODEL = 1024
N_HEADS = 8
HEAD_DIM = 128
D_FF = 4 * D_MODEL
ROPE_THETA = 10000.0
NORM_EPS = 1e-6
NEG_INF = -1e30
LOG2_E = 1.4426950408889634

NSA_KV_GROUPS = 2
NSA_HEADS_PER_GROUP = N_HEADS // NSA_KV_GROUPS
NSA_CMP_BLOCK = 32
NSA_CMP_STRIDE = 16
NSA_SEL_BLOCK = 64
NSA_SEL_TOPK = 16
NSA_WINDOW = 512
NSA_FORCE_BONUS = 1000.0

CONV_WIDTH = 31
MOBA_BLOCK = 256
MOBA_TOPK = 3

LANES = 128
VMEM_LIMIT_BYTES = 56 * 1024 * 1024
ROW_TILE = 512
PROJ_CHUNK = 512
FF_CHUNK = 1024
CONV_HALO = 32
CONV_ROWS = 32


def _cparams(*sem):
    return pltpu.CompilerParams(dimension_semantics=sem, vmem_limit_bytes=VMEM_LIMIT_BYTES)


def _const_spec(shape):
    zeros = (0,) * len(shape)
    return pl.BlockSpec(shape, lambda *_: zeros, pipeline_mode=pl.Buffered(1))


def _rms(x, gain):
    return x * lax.rsqrt(jnp.mean(x * x, axis=-1, keepdims=True) + NORM_EPS) * gain


def _dot(a, b):
    return jnp.dot(a, b, preferred_element_type=F32)


def _dot_nt(a, b):
    return lax.dot_general(a, b, (((1,), (1,)), ((), ())), preferred_element_type=F32)


def _split_bf16(x):
    hi = x.astype(BF16)
    lo = (x - hi.astype(F32)).astype(BF16)
    return hi, lo


def _proj_kernel(*refs, rope_blocks, nsa_extras, slab_chunk):
    if nsa_extras:
        x_ref, g_ref, w_ref, cos_ref, sin_ref, hg_ref, wg_ref, o_ref, og_ref, oslab_ref, slab_ref = refs
    else:
        x_ref, g_ref, w_ref, cos_ref, sin_ref, hg_ref, o_ref = refs
    h = _rms(x_ref[...], g_ref[...]).astype(BF16)
    blocks_per_chunk = PROJ_CHUNK // LANES
    for c in range(len(rope_blocks) // blocks_per_chunk):
        y = _dot(h, w_ref[:, c * PROJ_CHUNK:(c + 1) * PROJ_CHUNK])
        chunk_flags = rope_blocks[c * blocks_per_chunk:(c + 1) * blocks_per_chunk]
        keep_slabs = nsa_extras and c == slab_chunk
        if not any(chunk_flags) and not keep_slabs:
            o_ref[:, c * PROJ_CHUNK:(c + 1) * PROJ_CHUNK] = y.astype(o_ref.dtype)
            continue
        for k, flag in enumerate(chunk_flags):
            b = c * blocks_per_chunk + k
            yb = y[:, k * LANES:(k + 1) * LANES]
            if flag:
                yb = _rms(yb, hg_ref[b:b + 1, :])
                yb = yb * cos_ref[...] + pltpu.roll(yb, HEAD_DIM // 2, 1) * sin_ref[...]
            o_ref[:, b * LANES:(b + 1) * LANES] = yb.astype(o_ref.dtype)
            if keep_slabs:
                slab_ref[k] = yb
    if nsa_extras:
        og_ref[...] = _dot(h, wg_ref[...])
        st = NSA_CMP_STRIDE
        for l in range(st):
            for k in range(blocks_per_chunk):
                rows_l = slab_ref[k, pl.ds(l, ROW_TILE // st, stride=st), :]
                oslab_ref[l, :, k * LANES:(k + 1) * LANES] = rows_l.astype(oslab_ref.dtype)


def _project(x2d, gain, w, cos_t, sin_t, head_gains, rope_blocks, seq, w_gate=None, slab_chunk=None):
    t_rows, d = x2d.shape
    n = w.shape[1]
    assert n % PROJ_CHUNK == 0 and len(rope_blocks) == n // LANES and seq % ROW_TILE == 0
    seq_tiles = seq // ROW_TILE
    has_gate = w_gate is not None
    in_specs = [
        pl.BlockSpec((ROW_TILE, d), lambda i: (i, 0)),
        _const_spec((1, d)),
        _const_spec((d, n)),
        pl.BlockSpec((ROW_TILE, LANES), lambda i: (i % seq_tiles, 0)),
        pl.BlockSpec((ROW_TILE, LANES), lambda i: (i % seq_tiles, 0)),
        _const_spec(head_gains.shape),
    ]
    args = [x2d, gain.reshape(1, d), w, cos_t, sin_t, head_gains]
    out_shape = [jax.ShapeDtypeStruct((t_rows, n), BF16)]
    out_specs = [pl.BlockSpec((ROW_TILE, n), lambda i: (i, 0))]
    if has_gate:
        in_specs.append(_const_spec(w_gate.shape))
        args.append(w_gate)
        out_shape.append(jax.ShapeDtypeStruct((t_rows, w_gate.shape[1]), F32))
        out_specs.append(pl.BlockSpec((ROW_TILE, w_gate.shape[1]), lambda i: (i, 0)))
        st = NSA_CMP_STRIDE
        out_shape.append(jax.ShapeDtypeStruct((st, t_rows // st, PROJ_CHUNK), BF16))
        out_specs.append(pl.BlockSpec((st, ROW_TILE // st, PROJ_CHUNK), lambda i: (0, i, 0)))
    outs = pl.pallas_call(
        functools.partial(_proj_kernel, rope_blocks=tuple(rope_blocks), nsa_extras=has_gate,
                          slab_chunk=slab_chunk),
        grid=(t_rows // ROW_TILE,),
        in_specs=in_specs,
        out_specs=out_specs,
        out_shape=out_shape,
        scratch_shapes=[pltpu.VMEM((PROJ_CHUNK // LANES, ROW_TILE, LANES), F32)] if has_gate else [],
        compiler_params=_cparams("parallel"),
        name="norm_proj",
    )(*args)
    return outs if has_gate else outs[0]


def _mlp_kernel(x_ref, a_ref, wo_ref, g_ref, wup_ref, wdn_ref, o_ref):
    x1 = x_ref[...] + _dot(a_ref[...], wo_ref[...])
    h = _rms(x1, g_ref[...]).astype(BF16)
    acc = x1
    for c in range(D_FF // FF_CHUNK):
        u = _dot(h, wup_ref[0, :, c * FF_CHUNK:(c + 1) * FF_CHUNK])
        act = jnp.square(jnp.maximum(u, 0.0)).astype(BF16)
        acc = acc + _dot(act, wdn_ref[0, c * FF_CHUNK:(c + 1) * FF_CHUNK, :])
    o_ref[...] = acc


def _mixer_out_and_mlp(x2d, a2d, w_out, gain, w_up_all, w_down_all, layer):
    t_rows, d = x2d.shape

    def layer_spec(shape):
        return pl.BlockSpec((1,) + shape, lambda i: (layer, 0, 0), pipeline_mode=pl.Buffered(1))

    return pl.pallas_call(
        _mlp_kernel,
        grid=(t_rows // ROW_TILE,),
        in_specs=[
            pl.BlockSpec((ROW_TILE, d), lambda i: (i, 0)),
            pl.BlockSpec((ROW_TILE, d), lambda i: (i, 0)),
            _const_spec((d, d)),
            _const_spec((1, d)),
            layer_spec((d, D_FF)),
            layer_spec((D_FF, d)),
        ],
        out_specs=pl.BlockSpec((ROW_TILE, d), lambda i: (i, 0)),
        out_shape=jax.ShapeDtypeStruct((t_rows, d), F32),
        compiler_params=_cparams("parallel"),
        name="outproj_mlp",
    )(x2d, a2d, w_out, gain.reshape(1, d), w_up_all, w_down_all)


SB_TQ = 1024
SB_TK = 256
SB_UNDERFLOW_LOG = -104.0


def _sb_kernel(q_ref, k_ref, v_ref, o_ref):
    i = pl.program_id(2)
    tq, tk = SB_TQ, SB_TK
    own_blocks = tq // tk
    suffix_ones = (lax.broadcasted_iota(jnp.int32, (tk, tk), 0)
                   > lax.broadcasted_iota(jnp.int32, (tk, tk), 1)).astype(BF16)

    def block(j, q, carry, diagonal):
        o, later = carry
        start = pl.multiple_of(j * tk, tk)
        kj = k_ref[0, pl.ds(start, tk), :]
        vj = v_ref[0, pl.ds(start, tk), :]
        z = _dot_nt(q, kj)
        softplus = jnp.maximum(z, 0.0) + jnp.log(1.0 + jnp.exp(-jnp.abs(z)))
        log_om = -softplus
        if diagonal:
            below = (lax.broadcasted_iota(jnp.int32, z.shape, 0)
                     > lax.broadcasted_iota(jnp.int32, z.shape, 1))
            log_om = jnp.where(below, log_om, 0.0)
        hi, lo = _split_bf16(log_om)
        between = _dot(hi, suffix_ones) + _dot(lo, suffix_ones) + later
        a = jnp.exp(z - softplus + between)
        if diagonal:
            a = jnp.where(below, a, 0.0)
        o = o + _dot(a.astype(BF16), vj)
        later = later + jnp.sum(log_om, axis=1, keepdims=True)
        return o, later

    o = jnp.zeros((tq, HEAD_DIM), F32)
    later = jnp.zeros((tq, 1), F32)
    for b in reversed(range(own_blocks)):
        r0 = b * tk
        o_b, later_b = block(i * own_blocks + b, q_ref[0, r0:, :], (o[r0:], later[r0:]), True)
        if b > 0:
            o = jnp.concatenate([o[:r0], o_b], axis=0)
            later = jnp.concatenate([later[:r0], later_b], axis=0)
        else:
            o, later = o_b, later_b

    def more(state):
        j, _, later = state
        return (j >= 0) & (jnp.max(later) > SB_UNDERFLOW_LOG)

    def step(state):
        j, o, later = state
        o, later = block(j, q_ref[0], (o, later), False)
        return j - 1, o, later

    _, o, _ = lax.while_loop(more, step, (i * own_blocks - 1, o, later))
    o_ref[0] = o.astype(o_ref.dtype)


def _sb_attention(qkv):
    b, s, _ = qkv.shape
    h = N_HEADS
    return pl.pallas_call(
        _sb_kernel,
        grid=(b, h, s // SB_TQ),
        in_specs=[
            pl.BlockSpec((1, SB_TQ, LANES), lambda bi, hi, i: (bi, i, hi)),
            pl.BlockSpec((1, s, LANES), lambda bi, hi, i: (bi, 0, h + hi)),
            pl.BlockSpec((1, s, LANES), lambda bi, hi, i: (bi, 0, 2 * h + hi)),
        ],
        out_specs=pl.BlockSpec((1, SB_TQ, LANES), lambda bi, hi, i: (bi, i, hi)),
        out_shape=jax.ShapeDtypeStruct((b, s, h * HEAD_DIM), BF16),
        compiler_params=_cparams("parallel", "parallel", "arbitrary"),
        name="stick_breaking_attn",
    )(qkv, qkv, qkv)


MOBA_TQ = 2048
MOBA_TK = 1024
MOBA_OWN_TK = 512
ONES_ROWS = 16


def _online_softmax_step_t(carry, s_t, v_t):
    m, acc = carry
    m_new = jnp.maximum(m, jnp.max(s_t, axis=0, keepdims=True))
    p_t = jnp.exp2(s_t - m_new)
    acc = jnp.exp2(m - m_new) * acc + _dot(v_t, p_t.astype(BF16))
    return m_new, acc


def _flash_loop_t(n_steps, q_aug, keys_fn, values_t_fn, carry):
    def body(n, carry):
        return _online_softmax_step_t(carry, _dot_nt(keys_fn(n), q_aug), values_t_fn(n))

    return lax.fori_loop(0, n_steps, body, carry)


def _transposed_values_with_ones(v):
    v_t = v.astype(F32).T.astype(BF16)
    return jnp.concatenate([v_t, jnp.ones((ONES_ROWS, v.shape[0]), BF16)], axis=0)


def _moba_kernel(q_ref, k_ref, v_ref, o_ref, kaug_ref, kmean_ref, vt_ref, *, n_blk):
    i = pl.program_id(2)
    t, bs = MOBA_TQ, MOBA_BLOCK
    s_len = k_ref.shape[1]

    @pl.when(i == 0)
    def _():
        k = k_ref[0]
        kaug_ref[:, :LANES] = k
        blk = lax.broadcasted_iota(jnp.int32, (s_len, LANES), 0) // bs
        lane = lax.broadcasted_iota(jnp.int32, (s_len, LANES), 1)
        kaug_ref[:, LANES:] = (blk == lane).astype(BF16)
        kmean_ref[...] = jnp.mean(k.astype(F32).reshape(n_blk, bs, LANES), axis=1)
        for c in range(s_len // MOBA_TK):
            vt_ref[c] = _transposed_values_with_ones(v_ref[0, c * MOBA_TK:(c + 1) * MOBA_TK, :])

    q = q_ref[0]
    km_hi, km_lo = _split_bf16(kmean_ref[...])
    gate = _dot_nt(km_hi, q) + _dot_nt(km_lo, q)
    blk_id = lax.broadcasted_iota(jnp.int32, (n_blk, t), 0)
    cur = i * (t // bs) + lax.broadcasted_iota(jnp.int32, (n_blk, t), 1) // bs
    past = blk_id < cur
    gate = jnp.where(past, gate, -jnp.inf)
    visible = (_top_k_mask_t(gate, MOBA_TOPK) & past) | (blk_id == cur)
    sel_bias = jnp.where(visible, 0.0, NEG_INF)
    sel_bias = jnp.concatenate([sel_bias, jnp.zeros((LANES - n_blk, t), F32)], axis=0)
    q_aug = jnp.concatenate([q, sel_bias.T.astype(BF16)], axis=1)

    tk = MOBA_TK

    def keys(n):
        return kaug_ref[pl.ds(pl.multiple_of(n * tk, tk), tk), :]

    def values_t(n):
        return vt_ref[n]

    own = i * (t // tk)
    to = MOBA_OWN_TK
    causal = (lax.broadcasted_iota(jnp.int32, (to, t), 0) <= lax.broadcasted_iota(jnp.int32, (to, t), 1))
    carry = None
    for u in range(t // to):
        r0 = u * to
        k_u = kaug_ref[pl.ds(pl.multiple_of(i * t + r0, to), to), :]
        v_u = vt_ref[own + r0 // tk][:, r0 % tk:r0 % tk + to]
        s_t = jnp.where(causal[:, :t - r0], _dot_nt(k_u, q_aug[r0:]), NEG_INF)
        if carry is None:
            m0 = jnp.max(s_t, axis=0, keepdims=True)
            carry = (m0, _dot(v_u, jnp.exp2(s_t - m0).astype(BF16)))
        else:
            m_u, acc_u = _online_softmax_step_t((carry[0][:, r0:], carry[1][:, r0:]), s_t, v_u)
            carry = (jnp.concatenate([carry[0][:, :r0], m_u], axis=1),
                     jnp.concatenate([carry[1][:, :r0], acc_u], axis=1))

    _, acc = _flash_loop_t(own, q_aug, keys, values_t, carry)
    o_t = acc[:HEAD_DIM] * (1.0 / acc[HEAD_DIM:HEAD_DIM + 1])
    o_ref[0] = o_t.T.astype(o_ref.dtype)


def _moba_attention(qkv):
    b, s, _ = qkv.shape
    h = N_HEADS
    assert s % MOBA_TQ == 0 and MOBA_TQ % MOBA_BLOCK == 0
    n_blk = s // MOBA_BLOCK
    assert n_blk % 8 == 0 and n_blk <= LANES
    return pl.pallas_call(
        functools.partial(_moba_kernel, n_blk=n_blk),
        grid=(b, h, s // MOBA_TQ),
        in_specs=[
            pl.BlockSpec((1, MOBA_TQ, LANES), lambda bi, hi, i: (bi, i, hi)),
            pl.BlockSpec((1, s, LANES), lambda bi, hi, i: (bi, 0, h + hi)),
            pl.BlockSpec((1, s, LANES), lambda bi, hi, i: (bi, 0, 2 * h + hi)),
        ],
        out_specs=pl.BlockSpec((1, MOBA_TQ, LANES), lambda bi, hi, i: (bi, i, hi)),
        out_shape=jax.ShapeDtypeStruct((b, s, h * HEAD_DIM), BF16),
        scratch_shapes=[pltpu.VMEM((s, 2 * LANES), BF16), pltpu.VMEM((n_blk, LANES), F32),
                        pltpu.VMEM((s // MOBA_TK, HEAD_DIM + ONES_ROWS, MOBA_TK), BF16)],
        compiler_params=_cparams("parallel", "parallel", "arbitrary"),
        name="moba_attn",
    )(qkv, qkv, qkv)


NSA_TQ = 512
NSA_TK = 1024
def _cmp_kernel(*refs):
    st = NSA_CMP_STRIDE
    x_refs, (pos_ref, w_ref, ok_ref, ov_ref) = refs[:2 * st], refs[2 * st:]
    n16 = ok_ref.shape[2]
    for t, o_ref in enumerate((ok_ref, ov_ref)):
        first = jnp.zeros((n16, HEAD_DIM), F32)
        second = jnp.zeros((n16, HEAD_DIM), F32)
        for l in range(st):
            x = x_refs[t * st + l][0].astype(F32)
            first = first + _dot((x + pos_ref[t, l:l + 1, :]).astype(BF16), w_ref[t, l])
            second = second + _dot((x + pos_ref[t, st + l:st + l + 1, :]).astype(BF16), w_ref[t, st + l])
        out = first + pltpu.roll(second, n16 - 1, 0)
        o_ref[0, 0] = (out if t == 0 else out.T).astype(o_ref.dtype)


def _nsa_compress(slabs, batch, cmp_pos, w_cmp):
    g, st = NSA_KV_GROUPS, NSA_CMP_STRIDE
    b = batch
    n16 = slabs.shape[1] // b

    def slab_spec(first_block, l):
        return pl.BlockSpec((1, n16, LANES), lambda bi, gi: (l, bi, first_block + gi))

    x_specs = [slab_spec(t * g, l) for t in range(2) for l in range(st)]
    pos = cmp_pos.astype(F32)
    w = w_cmp.astype(BF16)
    return pl.pallas_call(
        _cmp_kernel,
        grid=(b, g),
        in_specs=x_specs + [_const_spec(pos.shape), _const_spec(w.shape)],
        out_specs=[pl.BlockSpec((1, 1, n16, HEAD_DIM), lambda bi, gi: (bi, gi, 0, 0)),
                   pl.BlockSpec((1, 1, HEAD_DIM, n16), lambda bi, gi: (bi, gi, 0, 0))],
        out_shape=[jax.ShapeDtypeStruct((b, g, n16, HEAD_DIM), BF16),
                   jax.ShapeDtypeStruct((b, g, HEAD_DIM, n16), BF16)],
        compiler_params=_cparams("parallel", "parallel"),
        name="nsa_compress",
    )(*([slabs] * (2 * st)), pos, w)


def _top_k_mask_t(score_t, k):
    n, q = score_t.shape
    work = score_t
    taken = jnp.zeros((1, q), F32)
    level = jnp.full((1, q), jnp.inf, F32)
    above = jnp.zeros((1, q), F32)
    for _ in range(k):
        best = jnp.max(work, axis=0, keepdims=True)
        hit = work == best
        active = taken < k
        level = jnp.where(active, best, level)
        above = jnp.where(active, taken, above)
        taken = taken + jnp.sum(hit.astype(F32), axis=0, keepdims=True)
        work = jnp.where(hit, -jnp.inf, work)
    tie = score_t == level
    lower = (lax.broadcasted_iota(jnp.int32, (n, n), 1)
             < lax.broadcasted_iota(jnp.int32, (n, n), 0)).astype(BF16)
    ties_before = _dot(lower, tie.astype(BF16))
    return (score_t > level) | (tie & (ties_before < k - above))


def _nsa_kernel(q_ref, kc_ref, vc_ref, ks_ref, vs_ref, kw_ref, vw_ref, gate_ref, ov_ref, o_ref,
                ksaug_ref, vst_ref, vwt_ref, *, n_sel):
    i = pl.program_id(2)
    tq, tk, hg = NSA_TQ, NSA_TK, NSA_HEADS_PER_GROUP
    s_len = ks_ref.shape[1]
    n_cmp_pad = kc_ref.shape[2]
    c0 = i * tq

    @pl.when(i == 0)
    def _():
        ksaug_ref[:, :LANES] = ks_ref[0]
        blk = lax.broadcasted_iota(jnp.int32, (s_len, LANES), 0) // NSA_SEL_BLOCK
        lane = lax.broadcasted_iota(jnp.int32, (s_len, LANES), 1)
        ksaug_ref[:, LANES:] = (blk == lane).astype(BF16)
        for c in range(s_len // tk):
            vst_ref[c] = _transposed_values_with_ones(vs_ref[0, c * tk:(c + 1) * tk, :])
        for c in range(s_len // tq):
            vwt_ref[c] = _transposed_values_with_ones(vw_ref[0, c * tq:(c + 1) * tq, :])

    q_all = q_ref[0]
    q_heads = [q_all[:, h * LANES:(h + 1) * LANES] for h in range(hg)]

    def key_iota(n):
        return lax.broadcasted_iota(jnp.int32, (n, tq), 0)

    def q_pos(n):
        return c0 + lax.broadcasted_iota(jnp.int32, (n, tq), 1)

    q4 = jnp.concatenate(q_heads, axis=0)

    def all_heads(mask):
        return jnp.concatenate([mask] * hg, axis=1)

    cmp_mask = all_heads(key_iota(n_cmp_pad) * NSA_CMP_STRIDE + (NSA_CMP_BLOCK - 1) <= q_pos(n_cmp_pad))
    z = jnp.where(cmp_mask, _dot_nt(kc_ref[0, 0], q4), NEG_INF)
    e = jnp.where(cmp_mask, jnp.exp2(z - jnp.max(z, axis=0, keepdims=True)), 0.0)
    l = jnp.sum(e, axis=0, keepdims=True)
    inv = 1.0 / jnp.where(l > 0.0, l, 1.0)
    o_c = _dot(vc_ref[0, 0], e.astype(BF16)) * inv
    p_c = e * inv
    p_sum = p_c[:, 0:tq]
    for h in range(1, hg):
        p_sum = p_sum + p_c[:, h * tq:(h + 1) * tq]

    ps_hi, ps_lo = _split_bf16(p_sum)
    imp = _dot(ov_ref[...], ps_hi) + _dot(ov_ref[...], ps_lo)
    blk = key_iota(LANES)
    cur = q_pos(LANES) // NSA_SEL_BLOCK
    forced = (blk == 0) | (blk == cur) | (blk == cur - 1)
    score = jnp.where(blk <= cur, imp + NSA_FORCE_BONUS * forced.astype(F32), -1.0)
    score = jnp.where(blk < n_sel, score, -jnp.inf)
    chosen = _top_k_mask_t(score, min(NSA_SEL_TOPK, n_sel))
    sel_bias = jnp.where(chosen, 0.0, NEG_INF).T.astype(BF16)
    q_aug = jnp.concatenate([q4, jnp.concatenate([sel_bias] * hg, axis=0)], axis=1)

    def keys(j):
        return ksaug_ref[pl.ds(pl.multiple_of(j * tk, tk), tk), :]

    def values_t(j):
        return vst_ref[j]

    jd = c0 // tk
    causal = jd * tk + key_iota(tk) <= q_pos(tk)
    s_t = jnp.where(all_heads(causal), _dot_nt(keys(jd), q_aug), NEG_INF)
    m0 = jnp.max(s_t, axis=0, keepdims=True)
    carry = (m0, _dot(values_t(jd), jnp.exp2(s_t - m0).astype(BF16)))
    _, acc = _flash_loop_t(jd, q_aug, keys, values_t, carry)
    o_s = acc[:HEAD_DIM] * (1.0 / acc[HEAD_DIM:HEAD_DIM + 1])

    span = tq + NSA_WINDOW
    wstart = pl.multiple_of(jnp.maximum(c0 - NSA_WINDOW, 0), tq)
    w_chunk = wstart // tq
    gap = q_pos(span) - (wstart + key_iota(span))
    win_mask = all_heads((gap >= 0) & (gap < NSA_WINDOW))
    z = jnp.where(win_mask, _dot_nt(kw_ref[0, pl.ds(wstart, span), :], q4), NEG_INF)
    e = jnp.exp2(z - jnp.max(z, axis=0, keepdims=True)).astype(BF16)
    acc_w = _dot(vwt_ref[w_chunk], e[0:tq])
    for c in range(1, span // tq):
        acc_w = acc_w + _dot(vwt_ref[w_chunk + c], e[c * tq:(c + 1) * tq])
    o_w = acc_w[:HEAD_DIM] * (1.0 / acc_w[HEAD_DIM:HEAD_DIM + 1])

    gates_t = (1.0 / (1.0 + jnp.exp(-gate_ref[0]))).T
    for h in range(hg):
        out = jnp.zeros((HEAD_DIM, tq), F32)
        for branch, o_b in enumerate((o_c, o_s, o_w)):
            r = branch * hg + h
            out = out + gates_t[r:r + 1, :] * o_b[:, h * tq:(h + 1) * tq]
        o_ref[0, :, h * LANES:(h + 1) * LANES] = out.T.astype(o_ref.dtype)


def _nsa_attention(main, gate_logits, kc_cmp, vc_cmp_t, overlap_t):
    b, s, _ = main.shape
    g, hg = NSA_KV_GROUPS, NSA_HEADS_PER_GROUP
    n_sel = s // NSA_SEL_BLOCK
    assert s % NSA_TK == 0 and n_sel <= LANES
    n16 = kc_cmp.shape[2]
    q_blocks = N_HEADS

    def kv_spec(which):
        return pl.BlockSpec((1, s, LANES), lambda bi, gi, i: (bi, 0, q_blocks + which * g + gi),
                            pipeline_mode=pl.Buffered(1))

    return pl.pallas_call(
        functools.partial(_nsa_kernel, n_sel=n_sel),
        grid=(b, g, s // NSA_TQ),
        in_specs=[
            pl.BlockSpec((1, NSA_TQ, hg * LANES), lambda bi, gi, i: (bi, i, gi)),
            pl.BlockSpec((1, 1, n16, HEAD_DIM), lambda bi, gi, i: (bi, gi, 0, 0)),
            pl.BlockSpec((1, 1, HEAD_DIM, n16), lambda bi, gi, i: (bi, gi, 0, 0)),
            kv_spec(2), kv_spec(3), kv_spec(4), kv_spec(5),
            pl.BlockSpec((1, NSA_TQ, LANES), lambda bi, gi, i: (bi, i, gi)),
            _const_spec(overlap_t.shape),
        ],
        out_specs=pl.BlockSpec((1, NSA_TQ, hg * LANES), lambda bi, gi, i: (bi, i, gi)),
        out_shape=jax.ShapeDtypeStruct((b, s, N_HEADS * HEAD_DIM), BF16),
        scratch_shapes=[pltpu.VMEM((s, 2 * LANES), BF16),
                        pltpu.VMEM((s // NSA_TK, HEAD_DIM + ONES_ROWS, NSA_TK), BF16),
                        pltpu.VMEM((s // NSA_TQ, HEAD_DIM + ONES_ROWS, NSA_TQ), BF16)],
        compiler_params=_cparams("parallel", "parallel", "arbitrary"),
        name="nsa_attn",
    )(main, kc_cmp, vc_cmp_t, main, main, main, main, gate_logits, overlap_t)


def _nsa_overlap_t(seq):
    n_cmp = (seq - NSA_CMP_BLOCK) // NSA_CMP_STRIDE + 1
    n_sel = seq // NSA_SEL_BLOCK
    cmp_start = np.arange(seq // NSA_CMP_STRIDE) * NSA_CMP_STRIDE
    sel_start = np.arange(LANES) * NSA_SEL_BLOCK
    ov = ((cmp_start[:, None] < sel_start[None, :] + NSA_SEL_BLOCK)
          & (cmp_start[:, None] + NSA_CMP_BLOCK > sel_start[None, :]))
    ov &= (np.arange(seq // NSA_CMP_STRIDE)[:, None] < n_cmp) & (np.arange(LANES)[None, :] < n_sel)
    return jnp.asarray(ov.T, dtype=BF16)


def _glu_proj_kernel(x_ref, g_ref, w_ref, o_ref):
    h = _rms(x_ref[...], g_ref[...]).astype(BF16)
    d = o_ref.shape[1]
    for c in range(d // PROJ_CHUNK):
        a = _dot(h, w_ref[:, c * PROJ_CHUNK:(c + 1) * PROJ_CHUNK])
        gate = _dot(h, w_ref[:, d + c * PROJ_CHUNK:d + (c + 1) * PROJ_CHUNK])
        o_ref[:, c * PROJ_CHUNK:(c + 1) * PROJ_CHUNK] = (a / (1.0 + jnp.exp(-gate))).astype(o_ref.dtype)


def _glu_project(x2d, gain, w):
    t_rows, d = x2d.shape
    return pl.pallas_call(
        _glu_proj_kernel,
        grid=(t_rows // ROW_TILE,),
        in_specs=[pl.BlockSpec((ROW_TILE, d), lambda i: (i, 0)), _const_spec((1, d)), _const_spec(w.shape)],
        out_specs=pl.BlockSpec((ROW_TILE, d), lambda i: (i, 0)),
        out_shape=jax.ShapeDtypeStruct((t_rows, d), BF16),
        compiler_params=_cparams("parallel"),
        name="norm_glu_proj",
    )(x2d, gain.reshape(1, d), w)


CONV_TILE = 256
SUBLANES = 8
CONV_PHASE_ROWS = CONV_TILE + CONV_HALO - SUBLANES


def _conv_kernel(u_ref, halo_ref, dw_ref, db_ref, lg_ref, lb_ref, o_ref, ext_ref, phase_ref):
    i = pl.program_id(1)
    halo = halo_ref[0].astype(F32)
    ext_ref[0:CONV_HALO, :] = jnp.where(i == 0, 0.0, halo)
    ext_ref[CONV_HALO:, :] = u_ref[0].astype(F32)
    for b in range(1, SUBLANES):
        phase_ref[b - 1] = ext_ref[pl.ds(b, CONV_PHASE_ROWS), :]
    lead = CONV_HALO - (CONV_WIDTH - 1)
    for r in range(CONV_TILE // CONV_ROWS):
        acc = jnp.zeros((CONV_ROWS, D_MODEL), F32) + db_ref[...]
        for w in range(CONV_WIDTH):
            shift = (lead + w) % SUBLANES
            start = r * CONV_ROWS + lead + w - shift
            src = ext_ref if shift == 0 else phase_ref.at[shift - 1]
            acc = acc + dw_ref[w:w + 1, :] * src[pl.ds(start, CONV_ROWS), :]
        mu = jnp.mean(acc, axis=-1, keepdims=True)
        cen = acc - mu
        var = jnp.mean(cen * cen, axis=-1, keepdims=True)
        un = cen * lax.rsqrt(var + NORM_EPS) * lg_ref[...] + lb_ref[...]
        o_ref[0, r * CONV_ROWS:(r + 1) * CONV_ROWS, :] = (un / (1.0 + jnp.exp(-un))).astype(o_ref.dtype)


def _conv_ln_swish(u, dw_w, dw_b, ln_g, ln_b):
    b, s, d = u.shape
    halo_per_tile = CONV_TILE // CONV_HALO
    dw = jnp.concatenate([dw_w, jnp.zeros((1, d), F32)], axis=0)
    return pl.pallas_call(
        _conv_kernel,
        grid=(b, s // CONV_TILE),
        in_specs=[
            pl.BlockSpec((1, CONV_TILE, d), lambda bi, i: (bi, i, 0)),
            pl.BlockSpec((1, CONV_HALO, d), lambda bi, i: (bi, jnp.maximum(i * halo_per_tile - 1, 0), 0)),
            _const_spec(dw.shape), _const_spec((1, d)), _const_spec((1, d)), _const_spec((1, d)),
        ],
        out_specs=pl.BlockSpec((1, CONV_TILE, d), lambda bi, i: (bi, i, 0)),
        out_shape=jax.ShapeDtypeStruct((b, s, d), BF16),
        scratch_shapes=[pltpu.VMEM((CONV_TILE + CONV_HALO, d), F32),
                        pltpu.VMEM((SUBLANES - 1, CONV_PHASE_ROWS, d), F32)],
        compiler_params=_cparams("parallel", "parallel"),
        name="conv_ln_swish",
    )(u, u, dw, dw_b.reshape(1, d), ln_g.reshape(1, d), ln_b.reshape(1, d))


def _rope_tables(seq):
    half = HEAD_DIM // 2
    inv_freq = ROPE_THETA ** (-jnp.arange(half, dtype=F32) / half)
    ang = jnp.arange(seq, dtype=F32)[:, None] * inv_freq[None, :]
    cos, sin = jnp.cos(ang), jnp.sin(ang)
    return jnp.concatenate([cos, cos], axis=1), jnp.concatenate([-sin, sin], axis=1)


def _nsa_mixer(x2d, norm_gain, w_in, q_gain, k_gain, cmp_pos, w_cmp, tables, batch, seq):
    g, dh = NSA_KV_GROUPS, HEAD_DIM
    n_main = (N_HEADS + 6 * g) * dh
    scale = dh ** -0.5 * LOG2_E
    w_main = w_in[:, :n_main].astype(BF16)
    hg = NSA_HEADS_PER_GROUP
    w_g = w_in[:, n_main:].reshape(-1, 3, g, hg).transpose(0, 2, 1, 3).reshape(-1, g, 3 * hg)
    w_gate = jnp.pad(w_g, ((0, 0), (0, 0), (0, LANES - 3 * hg))).reshape(-1, g * LANES).astype(BF16)
    ones = jnp.ones((dh,), F32)
    head_gains = jnp.stack([q_gain * scale] * N_HEADS + [k_gain[0]] * g + [ones] * g
                           + [k_gain[1]] * g + [ones] * g + [k_gain[2]] * g + [ones] * g)
    rope_blocks = [True] * N_HEADS + [True] * g + [False] * g + [True] * g + [False] * g + [True] * g + [False] * g
    kc_vc_chunk = N_HEADS * dh // PROJ_CHUNK
    assert 2 * g * dh == PROJ_CHUNK
    main, gate_logits, slabs = _project(x2d, norm_gain, w_main, *tables, head_gains, rope_blocks, seq,
                                        w_gate=w_gate, slab_chunk=kc_vc_chunk)
    main = main.reshape(batch, seq, n_main)
    gate_logits = gate_logits.reshape(batch, seq, g * LANES)

    kc_cmp, vc_cmp_t = _nsa_compress(slabs, batch, cmp_pos, w_cmp)
    out = _nsa_attention(main, gate_logits, kc_cmp, vc_cmp_t, _nsa_overlap_t(seq))
    return out.reshape(batch * seq, N_HEADS * dh)


def _qkv_weight_with_scaled_q(w_in):
    n_q = N_HEADS * HEAD_DIM
    return jnp.concatenate([w_in[:, :n_q] * HEAD_DIM ** -0.5, w_in[:, n_q:]], axis=1).astype(BF16)


def _sb_mixer(x2d, norm_gain, w_in, tables, batch, seq):
    n = 3 * N_HEADS * HEAD_DIM
    head_gains = jnp.ones((n // LANES, HEAD_DIM), F32)
    qkv = _project(x2d, norm_gain, _qkv_weight_with_scaled_q(w_in), *tables, head_gains,
                   [False] * (n // LANES), seq)
    return _sb_attention(qkv.reshape(batch, seq, n)).reshape(batch * seq, N_HEADS * HEAD_DIM)


def _conv_mixer(x2d, norm_gain, w_in, dw_w, dw_b, ln_g, ln_b, batch, seq):
    u = _glu_project(x2d, norm_gain, w_in.astype(BF16))
    a = _conv_ln_swish(u.reshape(batch, seq, D_MODEL), dw_w, dw_b, ln_g, ln_b)
    return a.reshape(batch * seq, D_MODEL)


def _moba_mixer(x2d, norm_gain, w_in, q_gain, k_gain, tables, batch, seq):
    n = 3 * N_HEADS * HEAD_DIM
    ones = jnp.ones((HEAD_DIM,), F32)
    head_gains = jnp.stack([q_gain * (HEAD_DIM ** -0.5 * LOG2_E)] * N_HEADS + [k_gain] * N_HEADS + [ones] * N_HEADS)
    rope_blocks = [True] * (2 * N_HEADS) + [False] * N_HEADS
    qkv = _project(x2d, norm_gain, w_in.astype(BF16), *tables, head_gains, rope_blocks, seq)
    return _moba_attention(qkv.reshape(batch, seq, n)).reshape(batch * seq, N_HEADS * HEAD_DIM)


def kernel(x, attn_norm, mlp_norm, mlp_w_up, mlp_w_down, nsa_w_in, nsa_q_norm, nsa_k_norm, nsa_cmp_pos, nsa_w_cmp, nsa_w_out, sb_w_in, sb_w_out, conv_w_in, conv_dw_w, conv_dw_b, conv_ln_g, conv_ln_b, conv_w_out, moba_w_in, moba_q_norm, moba_k_norm, moba_w_out):
    batch, seq, d = x.shape
    depth = attn_norm.shape[0]
    tables = _rope_tables(seq)
    w_up_all, w_down_all = mlp_w_up.astype(BF16), mlp_w_down.astype(BF16)
    x2d = x.reshape(batch * seq, d)
    for i in range(depth):
        m, j = i % 4, i // 4
        if m == 0:
            a = _nsa_mixer(x2d, attn_norm[i], nsa_w_in[j], nsa_q_norm[j], nsa_k_norm[j], nsa_cmp_pos[j],
                           nsa_w_cmp[j], tables, batch, seq)
            w_out = nsa_w_out[j]
        elif m == 1:
            a = _sb_mixer(x2d, attn_norm[i], sb_w_in[j], tables, batch, seq)
            w_out = sb_w_out[j]
        elif m == 2:
            a = _conv_mixer(x2d, attn_norm[i], conv_w_in[j], conv_dw_w[j], conv_dw_b[j], conv_ln_g[j],
                            conv_ln_b[j], batch, seq)
            w_out = conv_w_out[j]
        else:
            a = _moba_mixer(x2d, attn_norm[i], moba_w_in[j], moba_q_norm[j], moba_k_norm[j], tables, batch, seq)
            w_out = moba_w_out[j]
        x2d = _mixer_out_and_mlp(x2d, a, w_out.astype(BF16), mlp_norm[i], w_up_all, w_down_all, i)
    return x2d.reshape(batch, seq, d)
```

```python
import functools

import numpy as np
import jax
import jax.numpy as jnp
from jax import lax
from jax.experimental import pallas as pl
from jax.experimental.pallas import tpu as pltpu

F32 = jnp.float32
BF16 = jnp.bfloat16

D_MODEL = 1024
N_HEADS = 8
HEAD_DIM = 128
D_FF = 4 * D_MODEL
ROPE_THETA = 10000.0
NORM_EPS = 1e-6
NEG_INF = -1e30
LOG2_E = 1.4426950408889634

NSA_KV_GROUPS = 2
NSA_HEADS_PER_GROUP = N_HEADS // NSA_KV_GROUPS
NSA_CMP_BLOCK = 32
NSA_CMP_STRIDE = 16
NSA_SEL_BLOCK = 64
NSA_SEL_TOPK = 16
NSA_WINDOW = 512
NSA_FORCE_BONUS = 1000.0

CONV_WIDTH = 31
MOBA_BLOCK = 256
MOBA_TOPK = 3

LANES = 128
VMEM_LIMIT_BYTES = 56 * 1024 * 1024
ROW_TILE = 512
PROJ_CHUNK = 512
FF_CHUNK = 1024
CONV_HALO = 32
CONV_ROWS = 32


def _cparams(*sem):
    return pltpu.CompilerParams(dimension_semantics=sem, vmem_limit_bytes=VMEM_LIMIT_BYTES)


def _const_spec(shape):
    zeros = (0,) * len(shape)
    return pl.BlockSpec(shape, lambda *_: zeros, pipeline_mode=pl.Buffered(1))


def _rms(x, gain):
    return x * lax.rsqrt(jnp.mean(x * x, axis=-1, keepdims=True) + NORM_EPS) * gain


def _dot(a, b):
    return jnp.dot(a, b, preferred_element_type=F32)


def _dot_nt(a, b):
    return lax.dot_general(a, b, (((1,), (1,)), ((), ())), preferred_element_type=F32)


def _split_bf16(x):
    hi = x.astype(BF16)
    lo = (x - hi.astype(F32)).astype(BF16)
    return hi, lo


def _proj_kernel(*refs, rope_blocks, nsa_extras, slab_chunk):
    if nsa_extras:
        x_ref, g_ref, w_ref, cos_ref, sin_ref, hg_ref, wg_ref, o_ref, og_ref, oslab_ref, slab_ref = refs
    else:
        x_ref, g_ref, w_ref, cos_ref, sin_ref, hg_ref, o_ref = refs
    h = _rms(x_ref[...], g_ref[...]).astype(BF16)
    blocks_per_chunk = PROJ_CHUNK // LANES
    for c in range(len(rope_blocks) // blocks_per_chunk):
        y = _dot(h, w_ref[:, c * PROJ_CHUNK:(c + 1) * PROJ_CHUNK])
        chunk_flags = rope_blocks[c * blocks_per_chunk:(c + 1) * blocks_per_chunk]
        keep_slabs = nsa_extras and c == slab_chunk
        if not any(chunk_flags) and not keep_slabs:
            o_ref[:, c * PROJ_CHUNK:(c + 1) * PROJ_CHUNK] = y.astype(o_ref.dtype)
            continue
        for k, flag in enumerate(chunk_flags):
            b = c * blocks_per_chunk + k
            yb = y[:, k * LANES:(k + 1) * LANES]
            if flag:
                yb = _rms(yb, hg_ref[b:b + 1, :])
                yb = yb * cos_ref[...] + pltpu.roll(yb, HEAD_DIM // 2, 1) * sin_ref[...]
            o_ref[:, b * LANES:(b + 1) * LANES] = yb.astype(o_ref.dtype)
            if keep_slabs:
                slab_ref[k] = yb
    if nsa_extras:
        og_ref[...] = _dot(h, wg_ref[...])
        st = NSA_CMP_STRIDE
        for l in range(st):
            for k in range(blocks_per_chunk):
                rows_l = slab_ref[k, pl.ds(l, ROW_TILE // st, stride=st), :]
                oslab_ref[l, :, k * LANES:(k + 1) * LANES] = rows_l.astype(oslab_ref.dtype)


def _project(x2d, gain, w, cos_t, sin_t, head_gains, rope_blocks, seq, w_gate=None, slab_chunk=None):
    t_rows, d = x2d.shape
    n = w.shape[1]
    assert n % PROJ_CHUNK == 0 and len(rope_blocks) == n // LANES and seq % ROW_TILE == 0
    seq_tiles = seq // ROW_TILE
    has_gate = w_gate is not None
    in_specs = [
        pl.BlockSpec((ROW_TILE, d), lambda i: (i, 0)),
        _const_spec((1, d)),
        _const_spec((d, n)),
        pl.BlockSpec((ROW_TILE, LANES), lambda i: (i % seq_tiles, 0)),
        pl.BlockSpec((ROW_TILE, LANES), lambda i: (i % seq_tiles, 0)),
        _const_spec(head_gains.shape),
    ]
    args = [x2d, gain.reshape(1, d), w, cos_t, sin_t, head_gains]
    out_shape = [jax.ShapeDtypeStruct((t_rows, n), BF16)]
    out_specs = [pl.BlockSpec((ROW_TILE, n), lambda i: (i, 0))]
    if has_gate:
        in_specs.append(_const_spec(w_gate.shape))
        args.append(w_gate)
        out_shape.append(jax.ShapeDtypeStruct((t_rows, w_gate.shape[1]), F32))
        out_specs.append(pl.BlockSpec((ROW_TILE, w_gate.shape[1]), lambda i: (i, 0)))
        st = NSA_CMP_STRIDE
        out_shape.append(jax.ShapeDtypeStruct((st, t_rows // st, PROJ_CHUNK), BF16))
        out_specs.append(pl.BlockSpec((st, ROW_TILE // st, PROJ_CHUNK), lambda i: (0, i, 0)))
    outs = pl.pallas_call(
        functools.partial(_proj_kernel, rope_blocks=tuple(rope_blocks), nsa_extras=has_gate,
                          slab_chunk=slab_chunk),
        grid=(t_rows // ROW_TILE,),
        in_specs=in_specs,
        out_specs=out_specs,
        out_shape=out_shape,
        scratch_shapes=[pltpu.VMEM((PROJ_CHUNK // LANES, ROW_TILE, LANES), F32)] if has_gate else [],
        compiler_params=_cparams("parallel"),
        name="norm_proj",
    )(*args)
    return outs if has_gate else outs[0]


def _mlp_kernel(x_ref, a_ref, wo_ref, g_ref, wup_ref, wdn_ref, o_ref):
    x1 = x_ref[...] + _dot(a_ref[...], wo_ref[...])
    h = _rms(x1, g_ref[...]).astype(BF16)
    acc = x1
    for c in range(D_FF // FF_CHUNK):
        u = _dot(h, wup_ref[0, :, c * FF_CHUNK:(c + 1) * FF_CHUNK])
        act = jnp.square(jnp.maximum(u, 0.0)).astype(BF16)
        acc = acc + _dot(act, wdn_ref[0, c * FF_CHUNK:(c + 1) * FF_CHUNK, :])
    o_ref[...] = acc


def _mixer_out_and_mlp(x2d, a2d, w_out, gain, w_up_all, w_down_all, layer):
    t_rows, d = x2d.shape

    def layer_spec(shape):
        return pl.BlockSpec((1,) + shape, lambda i: (layer, 0, 0), pipeline_mode=pl.Buffered(1))

    return pl.pallas_call(
        _mlp_kernel,
        grid=(t_rows // ROW_TILE,),
        in_specs=[
            pl.BlockSpec((ROW_TILE, d), lambda i: (i, 0)),
            pl.BlockSpec((ROW_TILE, d), lambda i: (i, 0)),
            _const_spec((d, d)),
            _const_spec((1, d)),
            layer_spec((d, D_FF)),
            layer_spec((D_FF, d)),
        ],
        out_specs=pl.BlockSpec((ROW_TILE, d), lambda i: (i, 0)),
        out_shape=jax.ShapeDtypeStruct((t_rows, d), F32),
        compiler_params=_cparams("parallel"),
        name="outproj_mlp",
    )(x2d, a2d, w_out, gain.reshape(1, d), w_up_all, w_down_all)


SB_TQ = 1024
SB_TK = 256
SB_NEAR_ROWS = 256
SB_UNDERFLOW_LOG = -104.0


def _sb_kernel(q_ref, k_ref, v_ref, o_ref):
    i = pl.program_id(2)
    tq, tk = SB_TQ, SB_TK
    own_blocks = tq // tk
    suffix_ones = (lax.broadcasted_iota(jnp.int32, (tk, tk), 0)
                   > lax.broadcasted_iota(jnp.int32, (tk, tk), 1)).astype(BF16)

    def block(j, q, carry, diagonal):
        o, later = carry
        start = pl.multiple_of(j * tk, tk)
        kj = k_ref[0, pl.ds(start, tk), :]
        vj = v_ref[0, pl.ds(start, tk), :]
        z = _dot_nt(q, kj)
        softplus = jnp.maximum(z, 0.0) + jnp.log(1.0 + jnp.exp(-jnp.abs(z)))
        log_om = -softplus
        if diagonal:
            below = (lax.broadcasted_iota(jnp.int32, z.shape, 0)
                     > lax.broadcasted_iota(jnp.int32, z.shape, 1))
            log_om = jnp.where(below, log_om, 0.0)
        hi, lo = _split_bf16(log_om)
        between = _dot(hi, suffix_ones) + _dot(lo, suffix_ones) + later
        a = jnp.exp(z - softplus + between)
        if diagonal:
            a = jnp.where(below, a, 0.0)
        o = o + _dot(a.astype(BF16), vj)
        later = later + jnp.sum(log_om, axis=1, keepdims=True)
        return o, later

    def block_if_live(j, r0, o, later):
        def run(state):
            return block(j, q_ref[0, r0:, :], state, False)

        return lax.cond(jnp.max(later[r0:]) > SB_UNDERFLOW_LOG, run, lambda state: state, (o[r0:], later[r0:]))

    def splice(full, part, r0, r1):
        pieces = [full[:r0]] * (r0 > 0) + [part] + [full[r1:]] * (r1 < tq)
        return jnp.concatenate(pieces, axis=0) if len(pieces) > 1 else part

    o = jnp.zeros((tq, HEAD_DIM), F32)
    later = jnp.zeros((tq, 1), F32)
    for b in reversed(range(own_blocks)):
        j = i * own_blocks + b
        r0, r1 = b * tk, min(b * tk + tk + SB_NEAR_ROWS, tq)
        if r1 < tq:
            o_far, later_far = block_if_live(j, r1, o, later)
            o, later = splice(o, o_far, r1, tq), splice(later, later_far, r1, tq)
        o_b, later_b = block(j, q_ref[0, r0:r1, :], (o[r0:r1], later[r0:r1]), True)
        o, later = splice(o, o_b, r0, r1), splice(later, later_b, r0, r1)

    def more(state):
        j, _, later = state
        return (j >= 0) & (jnp.max(later) > SB_UNDERFLOW_LOG)

    def step(state):
        j, o, later = state
        o_far, later_far = block_if_live(j, SB_NEAR_ROWS, o, later)
        o_top, later_top = block(j, q_ref[0, :SB_NEAR_ROWS, :], (o[:SB_NEAR_ROWS], later[:SB_NEAR_ROWS]), False)
        o = jnp.concatenate([o_top, o_far], axis=0)
        later = jnp.concatenate([later_top, later_far], axis=0)
        return j - 1, o, later

    _, o, _ = lax.while_loop(more, step, (i * own_blocks - 1, o, later))
    o_ref[0] = o.astype(o_ref.dtype)


def _sb_attention(qkv):
    b, s, _ = qkv.shape
    h = N_HEADS
    return pl.pallas_call(
        _sb_kernel,
        grid=(b, h, s // SB_TQ),
        in_specs=[
            pl.BlockSpec((1, SB_TQ, LANES), lambda bi, hi, i: (bi, i, hi)),
            pl.BlockSpec((1, s, LANES), lambda bi, hi, i: (bi, 0, h + hi)),
            pl.BlockSpec((1, s, LANES), lambda bi, hi, i: (bi, 0, 2 * h + hi)),
        ],
        out_specs=pl.BlockSpec((1, SB_TQ, LANES), lambda bi, hi, i: (bi, i, hi)),
        out_shape=jax.ShapeDtypeStruct((b, s, h * HEAD_DIM), BF16),
        compiler_params=_cparams("parallel", "parallel", "arbitrary"),
        name="stick_breaking_attn",
    )(qkv, qkv, qkv)


MOBA_TQ = 2048
MOBA_TK = 1024
MOBA_OWN_TK = 512
ONES_ROWS = 16


def _online_softmax_step_t(carry, s_t, v_t):
    m, acc = carry
    m_new = jnp.maximum(m, jnp.max(s_t, axis=0, keepdims=True))
    p_t = jnp.exp2(s_t - m_new)
    acc = jnp.exp2(m - m_new) * acc + _dot(v_t, p_t.astype(BF16))
    return m_new, acc


def _flash_loop_t(n_steps, q_aug, keys_fn, values_t_fn, carry):
    def body(n, carry):
        return _online_softmax_step_t(carry, _dot_nt(keys_fn(n), q_aug), values_t_fn(n))

    return lax.fori_loop(0, n_steps, body, carry)


def _transposed_values_with_ones(v):
    v_t = v.astype(F32).T.astype(BF16)
    return jnp.concatenate([v_t, jnp.ones((ONES_ROWS, v.shape[0]), BF16)], axis=0)


def _moba_kernel(q_ref, k_ref, v_ref, o_ref, kaug_ref, kmean_ref, vt_ref, *, n_blk):
    i = pl.program_id(2)
    t, bs = MOBA_TQ, MOBA_BLOCK
    s_len = k_ref.shape[1]

    @pl.when(i == 0)
    def _():
        k = k_ref[0]
        kaug_ref[:, :LANES] = k
        blk = lax.broadcasted_iota(jnp.int32, (s_len, LANES), 0) // bs
        lane = lax.broadcasted_iota(jnp.int32, (s_len, LANES), 1)
        kaug_ref[:, LANES:] = (blk == lane).astype(BF16)
        kmean_ref[...] = jnp.mean(k.astype(F32).reshape(n_blk, bs, LANES), axis=1)
        for c in range(s_len // MOBA_TK):
            vt_ref[c] = _transposed_values_with_ones(v_ref[0, c * MOBA_TK:(c + 1) * MOBA_TK, :])

    q = q_ref[0]
    km_hi, km_lo = _split_bf16(kmean_ref[...])
    gate = _dot_nt(km_hi, q) + _dot_nt(km_lo, q)
    blk_id = lax.broadcasted_iota(jnp.int32, (n_blk, t), 0)
    cur = i * (t // bs) + lax.broadcasted_iota(jnp.int32, (n_blk, t), 1) // bs
    past = blk_id < cur
    gate = jnp.where(past, gate, -jnp.inf)
    visible = (_top_k_mask_t(gate, MOBA_TOPK) & past) | (blk_id == cur)
    sel_bias = jnp.where(visible, 0.0, NEG_INF)
    sel_bias = jnp.concatenate([sel_bias, jnp.zeros((LANES - n_blk, t), F32)], axis=0)
    q_aug = jnp.concatenate([q, sel_bias.T.astype(BF16)], axis=1)

    tk = MOBA_TK

    def keys(n):
        return kaug_ref[pl.ds(pl.multiple_of(n * tk, tk), tk), :]

    def values_t(n):
        return vt_ref[n]

    own = i * (t // tk)
    to = MOBA_OWN_TK
    causal = (lax.broadcasted_iota(jnp.int32, (to, t), 0) <= lax.broadcasted_iota(jnp.int32, (to, t), 1))
    carry = None
    for u in range(t // to):
        r0 = u * to
        k_u = kaug_ref[pl.ds(pl.multiple_of(i * t + r0, to), to), :]
        v_u = vt_ref[own + r0 // tk][:, r0 % tk:r0 % tk + to]
        s_t = jnp.where(causal[:, :t - r0], _dot_nt(k_u, q_aug[r0:]), NEG_INF)
        if carry is None:
            m0 = jnp.max(s_t, axis=0, keepdims=True)
            carry = (m0, _dot(v_u, jnp.exp2(s_t - m0).astype(BF16)))
        else:
            m_u, acc_u = _online_softmax_step_t((carry[0][:, r0:], carry[1][:, r0:]), s_t, v_u)
            carry = (jnp.concatenate([carry[0][:, :r0], m_u], axis=1),
                     jnp.concatenate([carry[1][:, :r0], acc_u], axis=1))

    _, acc = _flash_loop_t(own, q_aug, keys, values_t, carry)
    o_t = acc[:HEAD_DIM] * (1.0 / acc[HEAD_DIM:HEAD_DIM + 1])
    o_ref[0] = o_t.T.astype(o_ref.dtype)


def _moba_attention(qkv):
    b, s, _ = qkv.shape
    h = N_HEADS
    assert s % MOBA_TQ == 0 and MOBA_TQ % MOBA_BLOCK == 0
    n_blk = s // MOBA_BLOCK
    assert n_blk % 8 == 0 and n_blk <= LANES
    return pl.pallas_call(
        functools.partial(_moba_kernel, n_blk=n_blk),
        grid=(b, h, s // MOBA_TQ),
        in_specs=[
            pl.BlockSpec((1, MOBA_TQ, LANES), lambda bi, hi, i: (bi, i, hi)),
            pl.BlockSpec((1, s, LANES), lambda bi, hi, i: (bi, 0, h + hi)),
            pl.BlockSpec((1, s, LANES), lambda bi, hi, i: (bi, 0, 2 * h + hi)),
        ],
        out_specs=pl.BlockSpec((1, MOBA_TQ, LANES), lambda bi, hi, i: (bi, i, hi)),
        out_shape=jax.ShapeDtypeStruct((b, s, h * HEAD_DIM), BF16),
        scratch_shapes=[pltpu.VMEM((s, 2 * LANES), BF16), pltpu.VMEM((n_blk, LANES), F32),
                        pltpu.VMEM((s // MOBA_TK, HEAD_DIM + ONES_ROWS, MOBA_TK), BF16)],
        compiler_params=_cparams("parallel", "parallel", "arbitrary"),
        name="moba_attn",
    )(qkv, qkv, qkv)


NSA_TQ = 512
NSA_TK = 1024
def _cmp_kernel(*refs):
    st = NSA_CMP_STRIDE
    x_refs, (pos_ref, w_ref, ok_ref, ov_ref) = refs[:2 * st], refs[2 * st:]
    n16 = ok_ref.shape[2]
    for t, o_ref in enumerate((ok_ref, ov_ref)):
        first = jnp.zeros((n16, HEAD_DIM), F32)
        second = jnp.zeros((n16, HEAD_DIM), F32)
        for l in range(st):
            x = x_refs[t * st + l][0].astype(F32)
            first = first + _dot((x + pos_ref[t, l:l + 1, :]).astype(BF16), w_ref[t, l])
            second = second + _dot((x + pos_ref[t, st + l:st + l + 1, :]).astype(BF16), w_ref[t, st + l])
        out = first + pltpu.roll(second, n16 - 1, 0)
        o_ref[0, 0] = (out if t == 0 else out.T).astype(o_ref.dtype)


def _nsa_compress(slabs, batch, cmp_pos, w_cmp):
    g, st = NSA_KV_GROUPS, NSA_CMP_STRIDE
    b = batch
    n16 = slabs.shape[1] // b

    def slab_spec(first_block, l):
        return pl.BlockSpec((1, n16, LANES), lambda bi, gi: (l, bi, first_block + gi))

    x_specs = [slab_spec(t * g, l) for t in range(2) for l in range(st)]
    pos = cmp_pos.astype(F32)
    w = w_cmp.astype(BF16)
    return pl.pallas_call(
        _cmp_kernel,
        grid=(b, g),
        in_specs=x_specs + [_const_spec(pos.shape), _const_spec(w.shape)],
        out_specs=[pl.BlockSpec((1, 1, n16, HEAD_DIM), lambda bi, gi: (bi, gi, 0, 0)),
                   pl.BlockSpec((1, 1, HEAD_DIM, n16), lambda bi, gi: (bi, gi, 0, 0))],
        out_shape=[jax.ShapeDtypeStruct((b, g, n16, HEAD_DIM), BF16),
                   jax.ShapeDtypeStruct((b, g, HEAD_DIM, n16), BF16)],
        compiler_params=_cparams("parallel", "parallel"),
        name="nsa_compress",
    )(*([slabs] * (2 * st)), pos, w)


def _top_k_mask_t(score_t, k):
    n, q = score_t.shape
    work = score_t
    taken = jnp.zeros((1, q), F32)
    level = jnp.full((1, q), jnp.inf, F32)
    above = jnp.zeros((1, q), F32)
    for _ in range(k):
        best = jnp.max(work, axis=0, keepdims=True)
        hit = work == best
        active = taken < k
        level = jnp.where(active, best, level)
        above = jnp.where(active, taken, above)
        taken = taken + jnp.sum(hit.astype(F32), axis=0, keepdims=True)
        work = jnp.where(hit, -jnp.inf, work)
    tie = score_t == level
    lower = (lax.broadcasted_iota(jnp.int32, (n, n), 1)
             < lax.broadcasted_iota(jnp.int32, (n, n), 0)).astype(BF16)
    ties_before = _dot(lower, tie.astype(BF16))
    return (score_t > level) | (tie & (ties_before < k - above))


def _nsa_kernel(q_ref, kc_ref, vc_ref, ks_ref, vs_ref, kw_ref, vw_ref, gate_ref, ov_ref, o_ref,
                ksaug_ref, vst_ref, vwt_ref, *, n_sel):
    i = pl.program_id(2)
    tq, tk, hg = NSA_TQ, NSA_TK, NSA_HEADS_PER_GROUP
    s_len = ks_ref.shape[1]
    n_cmp_pad = kc_ref.shape[2]
    c0 = i * tq

    @pl.when(i == 0)
    def _():
        ksaug_ref[:, :LANES] = ks_ref[0]
        blk = lax.broadcasted_iota(jnp.int32, (s_len, LANES), 0) // NSA_SEL_BLOCK
        lane = lax.broadcasted_iota(jnp.int32, (s_len, LANES), 1)
        ksaug_ref[:, LANES:] = (blk == lane).astype(BF16)
        for c in range(s_len // tk):
            vst_ref[c] = _transposed_values_with_ones(vs_ref[0, c * tk:(c + 1) * tk, :])
        for c in range(s_len // tq):
            vwt_ref[c] = _transposed_values_with_ones(vw_ref[0, c * tq:(c + 1) * tq, :])

    q_all = q_ref[0]
    q_heads = [q_all[:, h * LANES:(h + 1) * LANES] for h in range(hg)]

    def key_iota(n):
        return lax.broadcasted_iota(jnp.int32, (n, tq), 0)

    def q_pos(n):
        return c0 + lax.broadcasted_iota(jnp.int32, (n, tq), 1)

    q4 = jnp.concatenate(q_heads, axis=0)

    def all_heads(mask):
        return jnp.concatenate([mask] * hg, axis=1)

    cmp_mask = all_heads(key_iota(n_cmp_pad) * NSA_CMP_STRIDE + (NSA_CMP_BLOCK - 1) <= q_pos(n_cmp_pad))
    z = jnp.where(cmp_mask, _dot_nt(kc_ref[0, 0], q4), NEG_INF)
    e = jnp.where(cmp_mask, jnp.exp2(z - jnp.max(z, axis=0, keepdims=True)), 0.0)
    l = jnp.sum(e, axis=0, keepdims=True)
    inv = 1.0 / jnp.where(l > 0.0, l, 1.0)
    o_c = _dot(vc_ref[0, 0], e.astype(BF16)) * inv
    p_c = e * inv
    p_sum = p_c[:, 0:tq]
    for h in range(1, hg):
        p_sum = p_sum + p_c[:, h * tq:(h + 1) * tq]

    ps_hi, ps_lo = _split_bf16(p_sum)
    imp = _dot(ov_ref[...], ps_hi) + _dot(ov_ref[...], ps_lo)
    blk = key_iota(LANES)
    cur = q_pos(LANES) // NSA_SEL_BLOCK
    forced = (blk == 0) | (blk == cur) | (blk == cur - 1)
    score = jnp.where(blk <= cur, imp + NSA_FORCE_BONUS * forced.astype(F32), -1.0)
    score = jnp.where(blk < n_sel, score, -jnp.inf)
    chosen = _top_k_mask_t(score, min(NSA_SEL_TOPK, n_sel))
    sel_bias = jnp.where(chosen, 0.0, NEG_INF).T.astype(BF16)
    q_aug = jnp.concatenate([q4, jnp.concatenate([sel_bias] * hg, axis=0)], axis=1)

    def keys(j):
        return ksaug_ref[pl.ds(pl.multiple_of(j * tk, tk), tk), :]

    def values_t(j):
        return vst_ref[j]

    jd = c0 // tk
    causal = jd * tk + key_iota(tk) <= q_pos(tk)
    s_t = jnp.where(all_heads(causal), _dot_nt(keys(jd), q_aug), NEG_INF)
    m0 = jnp.max(s_t, axis=0, keepdims=True)
    carry = (m0, _dot(values_t(jd), jnp.exp2(s_t - m0).astype(BF16)))
    _, acc = _flash_loop_t(jd, q_aug, keys, values_t, carry)
    o_s = acc[:HEAD_DIM] * (1.0 / acc[HEAD_DIM:HEAD_DIM + 1])

    span = tq + NSA_WINDOW
    wstart = pl.multiple_of(jnp.maximum(c0 - NSA_WINDOW, 0), tq)
    w_chunk = wstart // tq
    gap = q_pos(span) - (wstart + key_iota(span))
    win_mask = all_heads((gap >= 0) & (gap < NSA_WINDOW))
    z = jnp.where(win_mask, _dot_nt(kw_ref[0, pl.ds(wstart, span), :], q4), NEG_INF)
    e = jnp.exp2(z - jnp.max(z, axis=0, keepdims=True)).astype(BF16)
    acc_w = _dot(vwt_ref[w_chunk], e[0:tq])
    for c in range(1, span // tq):
        acc_w = acc_w + _dot(vwt_ref[w_chunk + c], e[c * tq:(c + 1) * tq])
    o_w = acc_w[:HEAD_DIM] * (1.0 / acc_w[HEAD_DIM:HEAD_DIM + 1])

    gates_t = (1.0 / (1.0 + jnp.exp(-gate_ref[0]))).T
    for h in range(hg):
        out = jnp.zeros((HEAD_DIM, tq), F32)
        for branch, o_b in enumerate((o_c, o_s, o_w)):
            r = branch * hg + h
            out = out + gates_t[r:r + 1, :] * o_b[:, h * tq:(h + 1) * tq]
        o_ref[0, :, h * LANES:(h + 1) * LANES] = out.T.astype(o_ref.dtype)


def _nsa_attention(main, gate_logits, kc_cmp, vc_cmp_t, overlap_t):
    b, s, _ = main.shape
    g, hg = NSA_KV_GROUPS, NSA_HEADS_PER_GROUP
    n_sel = s // NSA_SEL_BLOCK
    assert s % NSA_TK == 0 and n_sel <= LANES
    n16 = kc_cmp.shape[2]
    q_blocks = N_HEADS

    def kv_spec(which):
        return pl.BlockSpec((1, s, LANES), lambda bi, gi, i: (bi, 0, q_blocks + which * g + gi),
                            pipeline_mode=pl.Buffered(1))

    return pl.pallas_call(
        functools.partial(_nsa_kernel, n_sel=n_sel),
        grid=(b, g, s // NSA_TQ),
        in_specs=[
            pl.BlockSpec((1, NSA_TQ, hg * LANES), lambda bi, gi, i: (bi, i, gi)),
            pl.BlockSpec((1, 1, n16, HEAD_DIM), lambda bi, gi, i: (bi, gi, 0, 0)),
            pl.BlockSpec((1, 1, HEAD_DIM, n16), lambda bi, gi, i: (bi, gi, 0, 0)),
            kv_spec(2), kv_spec(3), kv_spec(4), kv_spec(5),
            pl.BlockSpec((1, NSA_TQ, LANES), lambda bi, gi, i: (bi, i, gi)),
            _const_spec(overlap_t.shape),
        ],
        out_specs=pl.BlockSpec((1, NSA_TQ, hg * LANES), lambda bi, gi, i: (bi, i, gi)),
        out_shape=jax.ShapeDtypeStruct((b, s, N_HEADS * HEAD_DIM), BF16),
        scratch_shapes=[pltpu.VMEM((s, 2 * LANES), BF16),
                        pltpu.VMEM((s // NSA_TK, HEAD_DIM + ONES_ROWS, NSA_TK), BF16),
                        pltpu.VMEM((s // NSA_TQ, HEAD_DIM + ONES_ROWS, NSA_TQ), BF16)],
        compiler_params=_cparams("parallel", "parallel", "arbitrary"),
        name="nsa_attn",
    )(main, kc_cmp, vc_cmp_t, main, main, main, main, gate_logits, overlap_t)


def _nsa_overlap_t(seq):
    n_cmp = (seq - NSA_CMP_BLOCK) // NSA_CMP_STRIDE + 1
    n_sel = seq // NSA_SEL_BLOCK
    cmp_start = np.arange(seq // NSA_CMP_STRIDE) * NSA_CMP_STRIDE
    sel_start = np.arange(LANES) * NSA_SEL_BLOCK
    ov = ((cmp_start[:, None] < sel_start[None, :] + NSA_SEL_BLOCK)
          & (cmp_start[:, None] + NSA_CMP_BLOCK > sel_start[None, :]))
    ov &= (np.arange(seq // NSA_CMP_STRIDE)[:, None] < n_cmp) & (np.arange(LANES)[None, :] < n_sel)
    return jnp.asarray(ov.T, dtype=BF16)


def _glu_proj_kernel(x_ref, g_ref, w_ref, o_ref):
    h = _rms(x_ref[...], g_ref[...]).astype(BF16)
    d = o_ref.shape[1]
    for c in range(d // PROJ_CHUNK):
        a = _dot(h, w_ref[:, c * PROJ_CHUNK:(c + 1) * PROJ_CHUNK])
        gate = _dot(h, w_ref[:, d + c * PROJ_CHUNK:d + (c + 1) * PROJ_CHUNK])
        o_ref[:, c * PROJ_CHUNK:(c + 1) * PROJ_CHUNK] = (a / (1.0 + jnp.exp(-gate))).astype(o_ref.dtype)


def _glu_project(x2d, gain, w):
    t_rows, d = x2d.shape
    return pl.pallas_call(
        _glu_proj_kernel,
        grid=(t_rows // ROW_TILE,),
        in_specs=[pl.BlockSpec((ROW_TILE, d), lambda i: (i, 0)), _const_spec((1, d)), _const_spec(w.shape)],
        out_specs=pl.BlockSpec((ROW_TILE, d), lambda i: (i, 0)),
        out_shape=jax.ShapeDtypeStruct((t_rows, d), BF16),
        compiler_params=_cparams("parallel"),
        name="norm_glu_proj",
    )(x2d, gain.reshape(1, d), w)


CONV_TILE = 256
SUBLANES = 8
CONV_PHASE_ROWS = CONV_TILE + CONV_HALO - SUBLANES


def _conv_kernel(u_ref, halo_ref, dw_ref, db_ref, lg_ref, lb_ref, o_ref, ext_ref, phase_ref):
    i = pl.program_id(1)
    halo = halo_ref[0].astype(F32)
    ext_ref[0:CONV_HALO, :] = jnp.where(i == 0, 0.0, halo)
    ext_ref[CONV_HALO:, :] = u_ref[0].astype(F32)
    for b in range(1, SUBLANES):
        phase_ref[b - 1] = ext_ref[pl.ds(b, CONV_PHASE_ROWS), :]
    lead = CONV_HALO - (CONV_WIDTH - 1)
    for r in range(CONV_TILE // CONV_ROWS):
        acc = jnp.zeros((CONV_ROWS, D_MODEL), F32) + db_ref[...]
        for w in range(CONV_WIDTH):
            shift = (lead + w) % SUBLANES
            start = r * CONV_ROWS + lead + w - shift
            src = ext_ref if shift == 0 else phase_ref.at[shift - 1]
            acc = acc + dw_ref[w:w + 1, :] * src[pl.ds(start, CONV_ROWS), :]
        mu = jnp.mean(acc, axis=-1, keepdims=True)
        cen = acc - mu
        var = jnp.mean(cen * cen, axis=-1, keepdims=True)
        un = cen * lax.rsqrt(var + NORM_EPS) * lg_ref[...] + lb_ref[...]
        o_ref[0, r * CONV_ROWS:(r + 1) * CONV_ROWS, :] = (un / (1.0 + jnp.exp(-un))).astype(o_ref.dtype)


def _conv_ln_swish(u, dw_w, dw_b, ln_g, ln_b):
    b, s, d = u.shape
    halo_per_tile = CONV_TILE // CONV_HALO
    dw = jnp.concatenate([dw_w, jnp.zeros((1, d), F32)], axis=0)
    return pl.pallas_call(
        _conv_kernel,
        grid=(b, s // CONV_TILE),
        in_specs=[
            pl.BlockSpec((1, CONV_TILE, d), lambda bi, i: (bi, i, 0)),
            pl.BlockSpec((1, CONV_HALO, d), lambda bi, i: (bi, jnp.maximum(i * halo_per_tile - 1, 0), 0)),
            _const_spec(dw.shape), _const_spec((1, d)), _const_spec((1, d)), _const_spec((1, d)),
        ],
        out_specs=pl.BlockSpec((1, CONV_TILE, d), lambda bi, i: (bi, i, 0)),
        out_shape=jax.ShapeDtypeStruct((b, s, d), BF16),
        scratch_shapes=[pltpu.VMEM((CONV_TILE + CONV_HALO, d), F32),
                        pltpu.VMEM((SUBLANES - 1, CONV_PHASE_ROWS, d), F32)],
        compiler_params=_cparams("parallel", "parallel"),
        name="conv_ln_swish",
    )(u, u, dw, dw_b.reshape(1, d), ln_g.reshape(1, d), ln_b.reshape(1, d))


def _rope_tables(seq):
    half = HEAD_DIM // 2
    inv_freq = ROPE_THETA ** (-jnp.arange(half, dtype=F32) / half)
    ang = jnp.arange(seq, dtype=F32)[:, None] * inv_freq[None, :]
    cos, sin = jnp.cos(ang), jnp.sin(ang)
    return jnp.concatenate([cos, cos], axis=1), jnp.concatenate([-sin, sin], axis=1)


def _nsa_mixer(x2d, norm_gain, w_in, q_gain, k_gain, cmp_pos, w_cmp, tables, batch, seq):
    g, dh = NSA_KV_GROUPS, HEAD_DIM
    n_main = (N_HEADS + 6 * g) * dh
    scale = dh ** -0.5 * LOG2_E
    w_main = w_in[:, :n_main].astype(BF16)
    hg = NSA_HEADS_PER_GROUP
    w_g = w_in[:, n_main:].reshape(-1, 3, g, hg).transpose(0, 2, 1, 3).reshape(-1, g, 3 * hg)
    w_gate = jnp.pad(w_g, ((0, 0), (0, 0), (0, LANES - 3 * hg))).reshape(-1, g * LANES).astype(BF16)
    ones = jnp.ones((dh,), F32)
    head_gains = jnp.stack([q_gain * scale] * N_HEADS + [k_gain[0]] * g + [ones] * g
                           + [k_gain[1]] * g + [ones] * g + [k_gain[2]] * g + [ones] * g)
    rope_blocks = [True] * N_HEADS + [True] * g + [False] * g + [True] * g + [False] * g + [True] * g + [False] * g
    kc_vc_chunk = N_HEADS * dh // PROJ_CHUNK
    assert 2 * g * dh == PROJ_CHUNK
    main, gate_logits, slabs = _project(x2d, norm_gain, w_main, *tables, head_gains, rope_blocks, seq,
                                        w_gate=w_gate, slab_chunk=kc_vc_chunk)
    main = main.reshape(batch, seq, n_main)
    gate_logits = gate_logits.reshape(batch, seq, g * LANES)

    kc_cmp, vc_cmp_t = _nsa_compress(slabs, batch, cmp_pos, w_cmp)
    out = _nsa_attention(main, gate_logits, kc_cmp, vc_cmp_t, _nsa_overlap_t(seq))
    return out.reshape(batch * seq, N_HEADS * dh)


def _qkv_weight_with_scaled_q(w_in):
    n_q = N_HEADS * HEAD_DIM
    return jnp.concatenate([w_in[:, :n_q] * HEAD_DIM ** -0.5, w_in[:, n_q:]], axis=1).astype(BF16)


def _sb_mixer(x2d, norm_gain, w_in, tables, batch, seq):
    n = 3 * N_HEADS * HEAD_DIM
    head_gains = jnp.ones((n // LANES, HEAD_DIM), F32)
    qkv = _project(x2d, norm_gain, _qkv_weight_with_scaled_q(w_in), *tables, head_gains,
                   [False] * (n // LANES), seq)
    return _sb_attention(qkv.reshape(batch, seq, n)).reshape(batch * seq, N_HEADS * HEAD_DIM)


def _conv_mixer(x2d, norm_gain, w_in, dw_w, dw_b, ln_g, ln_b, batch, seq):
    u = _glu_project(x2d, norm_gain, w_in.astype(BF16))
    a = _conv_ln_swish(u.reshape(batch, seq, D_MODEL), dw_w, dw_b, ln_g, ln_b)
    return a.reshape(batch * seq, D_MODEL)


def _moba_mixer(x2d, norm_gain, w_in, q_gain, k_gain, tables, batch, seq):
    n = 3 * N_HEADS * HEAD_DIM
    ones = jnp.ones((HEAD_DIM,), F32)
    head_gains = jnp.stack([q_gain * (HEAD_DIM ** -0.5 * LOG2_E)] * N_HEADS + [k_gain] * N_HEADS + [ones] * N_HEADS)
    rope_blocks = [True] * (2 * N_HEADS) + [False] * N_HEADS
    qkv = _project(x2d, norm_gain, w_in.astype(BF16), *tables, head_gains, rope_blocks, seq)
    return _moba_attention(qkv.reshape(batch, seq, n)).reshape(batch * seq, N_HEADS * HEAD_DIM)


def kernel(x, attn_norm, mlp_norm, mlp_w_up, mlp_w_down, nsa_w_in, nsa_q_norm, nsa_k_norm, nsa_cmp_pos, nsa_w_cmp, nsa_w_out, sb_w_in, sb_w_out, conv_w_in, conv_dw_w, conv_dw_b, conv_ln_g, conv_ln_b, conv_w_out, moba_w_in, moba_q_norm, moba_k_norm, moba_w_out):
    batch, seq, d = x.shape
    depth = attn_norm.shape[0]
    tables = _rope_tables(seq)
    w_up_all, w_down_all = mlp_w_up.astype(BF16), mlp_w_down.astype(BF16)
    x2d = x.reshape(batch * seq, d)
    for i in range(depth):
        m, j = i % 4, i // 4
        if m == 0:
            a = _nsa_mixer(x2d, attn_norm[i], nsa_w_in[j], nsa_q_norm[j], nsa_k_norm[j], nsa_cmp_pos[j],
                           nsa_w_cmp[j], tables, batch, seq)
            w_out = nsa_w_out[j]
        elif m == 1:
            a = _sb_mixer(x2d, attn_norm[i], sb_w_in[j], tables, batch, seq)
            w_out = sb_w_out[j]
        elif m == 2:
            a = _conv_mixer(x2d, attn_norm[i], conv_w_in[j], conv_dw_w[j], conv_dw_b[j], conv_ln_g[j],
                            conv_ln_b[j], batch, seq)
            w_out = conv_w_out[j]
        else:
            a = _moba_mixer(x2d, attn_norm[i], moba_w_in[j], moba_q_norm[j], moba_k_norm[j], tables, batch, seq)
            w_out = moba_w_out[j]
        x2d = _mixer_out_and_mlp(x2d, a, w_out.astype(BF16), mlp_norm[i], w_up_all, w_down_all, i)
    return x2d.reshape(batch, seq, d)
```

```python
import functools

import numpy as np
import jax
import jax.numpy as jnp
from jax import lax
from jax.experimental import pallas as pl
from jax.experimental.pallas import tpu as pltpu

F32 = jnp.float32
BF16 = jnp.bfloat16

D_MODEL = 1024
N_HEADS = 8
HEAD_DIM = 128
D_FF = 4 * D_MODEL
ROPE_THETA = 10000.0
NORM_EPS = 1e-6
NEG_INF = -1e30
LOG2_E = 1.4426950408889634

NSA_KV_GROUPS = 2
NSA_HEADS_PER_GROUP = N_HEADS // NSA_KV_GROUPS
NSA_CMP_BLOCK = 32
NSA_CMP_STRIDE = 16
NSA_SEL_BLOCK = 64
NSA_SEL_TOPK = 16
NSA_WINDOW = 512
NSA_FORCE_BONUS = 1000.0

CONV_WIDTH = 31
MOBA_BLOCK = 256
MOBA_TOPK = 3

LANES = 128
VMEM_LIMIT_BYTES = 56 * 1024 * 1024
ROW_TILE = 512
PROJ_CHUNK = 512
FF_CHUNK = 1024
CONV_HALO = 32
CONV_ROWS = 32


def _cparams(*sem):
    return pltpu.CompilerParams(dimension_semantics=sem, vmem_limit_bytes=VMEM_LIMIT_BYTES)


def _const_spec(shape):
    zeros = (0,) * len(shape)
    return pl.BlockSpec(shape, lambda *_: zeros, pipeline_mode=pl.Buffered(1))


def _rms(x, gain):
    return x * lax.rsqrt(jnp.mean(x * x, axis=-1, keepdims=True) + NORM_EPS) * gain


def _dot(a, b):
    return jnp.dot(a, b, preferred_element_type=F32)


def _dot_nt(a, b):
    return lax.dot_general(a, b, (((1,), (1,)), ((), ())), preferred_element_type=F32)


def _split_bf16(x):
    hi = x.astype(BF16)
    lo = (x - hi.astype(F32)).astype(BF16)
    return hi, lo


def _proj_kernel(*refs, rope_blocks, nsa_extras, slab_chunk):
    if nsa_extras:
        x_ref, g_ref, w_ref, cos_ref, sin_ref, hg_ref, wg_ref, o_ref, og_ref, oslab_ref, slab_ref = refs
    else:
        x_ref, g_ref, w_ref, cos_ref, sin_ref, hg_ref, o_ref = refs
    h = _rms(x_ref[...], g_ref[...]).astype(BF16)
    blocks_per_chunk = PROJ_CHUNK // LANES
    for c in range(len(rope_blocks) // blocks_per_chunk):
        y = _dot(h, w_ref[:, c * PROJ_CHUNK:(c + 1) * PROJ_CHUNK])
        chunk_flags = rope_blocks[c * blocks_per_chunk:(c + 1) * blocks_per_chunk]
        keep_slabs = nsa_extras and c == slab_chunk
        if not any(chunk_flags) and not keep_slabs:
            o_ref[:, c * PROJ_CHUNK:(c + 1) * PROJ_CHUNK] = y.astype(o_ref.dtype)
            continue
        for k, flag in enumerate(chunk_flags):
            b = c * blocks_per_chunk + k
            yb = y[:, k * LANES:(k + 1) * LANES]
            if flag:
                yb = _rms(yb, hg_ref[b:b + 1, :])
                yb = yb * cos_ref[...] + pltpu.roll(yb, HEAD_DIM // 2, 1) * sin_ref[...]
            o_ref[:, b * LANES:(b + 1) * LANES] = yb.astype(o_ref.dtype)
            if keep_slabs:
                slab_ref[k] = yb
    if nsa_extras:
        og_ref[...] = _dot(h, wg_ref[...])
        st = NSA_CMP_STRIDE
        for l in range(st):
            for k in range(blocks_per_chunk):
                rows_l = slab_ref[k, pl.ds(l, ROW_TILE // st, stride=st), :]
                oslab_ref[l, :, k * LANES:(k + 1) * LANES] = rows_l.astype(oslab_ref.dtype)


def _project(x2d, gain, w, cos_t, sin_t, head_gains, rope_blocks, seq, w_gate=None, slab_chunk=None):
    t_rows, d = x2d.shape
    n = w.shape[1]
    assert n % PROJ_CHUNK == 0 and len(rope_blocks) == n // LANES and seq % ROW_TILE == 0
    seq_tiles = seq // ROW_TILE
    has_gate = w_gate is not None
    in_specs = [
        pl.BlockSpec((ROW_TILE, d), lambda i: (i, 0)),
        _const_spec((1, d)),
        _const_spec((d, n)),
        pl.BlockSpec((ROW_TILE, LANES), lambda i: (i % seq_tiles, 0)),
        pl.BlockSpec((ROW_TILE, LANES), lambda i: (i % seq_tiles, 0)),
        _const_spec(head_gains.shape),
    ]
    args = [x2d, gain.reshape(1, d), w, cos_t, sin_t, head_gains]
    out_shape = [jax.ShapeDtypeStruct((t_rows, n), BF16)]
    out_specs = [pl.BlockSpec((ROW_TILE, n), lambda i: (i, 0))]
    if has_gate:
        in_specs.append(_const_spec(w_gate.shape))
        args.append(w_gate)
        out_shape.append(jax.ShapeDtypeStruct((t_rows, w_gate.shape[1]), F32))
        out_specs.append(pl.BlockSpec((ROW_TILE, w_gate.shape[1]), lambda i: (i, 0)))
        st = NSA_CMP_STRIDE
        out_shape.append(jax.ShapeDtypeStruct((st, t_rows // st, PROJ_CHUNK), BF16))
        out_specs.append(pl.BlockSpec((st, ROW_TILE // st, PROJ_CHUNK), lambda i: (0, i, 0)))
    outs = pl.pallas_call(
        functools.partial(_proj_kernel, rope_blocks=tuple(rope_blocks), nsa_extras=has_gate,
                          slab_chunk=slab_chunk),
        grid=(t_rows // ROW_TILE,),
        in_specs=in_specs,
        out_specs=out_specs,
        out_shape=out_shape,
        scratch_shapes=[pltpu.VMEM((PROJ_CHUNK // LANES, ROW_TILE, LANES), F32)] if has_gate else [],
        compiler_params=_cparams("parallel"),
        name="norm_proj",
    )(*args)
    return outs if has_gate else outs[0]


def _mlp_kernel(x_ref, a_ref, wo_ref, g_ref, wup_ref, wdn_ref, o_ref):
    x1 = x_ref[...] + _dot(a_ref[...], wo_ref[...])
    h = _rms(x1, g_ref[...]).astype(BF16)
    acc = x1
    for c in range(D_FF // FF_CHUNK):
        u = _dot(h, wup_ref[0, :, c * FF_CHUNK:(c + 1) * FF_CHUNK])
        act = jnp.square(jnp.maximum(u, 0.0)).astype(BF16)
        acc = acc + _dot(act, wdn_ref[0, c * FF_CHUNK:(c + 1) * FF_CHUNK, :])
    o_ref[...] = acc


def _mixer_out_and_mlp(x2d, a2d, w_out, gain, w_up_all, w_down_all, layer):
    t_rows, d = x2d.shape

    def layer_spec(shape):
        return pl.BlockSpec((1,) + shape, lambda i: (layer, 0, 0), pipeline_mode=pl.Buffered(1))

    return pl.pallas_call(
        _mlp_kernel,
        grid=(t_rows // ROW_TILE,),
        in_specs=[
            pl.BlockSpec((ROW_TILE, d), lambda i: (i, 0)),
            pl.BlockSpec((ROW_TILE, d), lambda i: (i, 0)),
            _const_spec((d, d)),
            _const_spec((1, d)),
            layer_spec((d, D_FF)),
            layer_spec((D_FF, d)),
        ],
        out_specs=pl.BlockSpec((ROW_TILE, d), lambda i: (i, 0)),
        out_shape=jax.ShapeDtypeStruct((t_rows, d), F32),
        compiler_params=_cparams("parallel"),
        name="outproj_mlp",
    )(x2d, a2d, w_out, gain.reshape(1, d), w_up_all, w_down_all)


SB_TQ = 1024
SB_TK = 256
SB_NEAR_ROWS = 256
SB_UNDERFLOW_LOG = -104.0


def _sb_kernel(q_ref, k_ref, v_ref, o_ref):
    i = pl.program_id(2)
    tq, tk = SB_TQ, SB_TK
    own_blocks = tq // tk
    suffix_ones = (lax.broadcasted_iota(jnp.int32, (tk, tk), 0)
                   > lax.broadcasted_iota(jnp.int32, (tk, tk), 1)).astype(BF16)

    def block(j, q, carry, diagonal):
        o, later = carry
        start = pl.multiple_of(j * tk, tk)
        kj = k_ref[0, pl.ds(start, tk), :]
        vj = v_ref[0, pl.ds(start, tk), :]
        z = _dot_nt(q, kj)
        softplus = jnp.maximum(z, 0.0) + jnp.log(1.0 + jnp.exp(-jnp.abs(z)))
        log_om = -softplus
        if diagonal:
            below = (lax.broadcasted_iota(jnp.int32, z.shape, 0)
                     > lax.broadcasted_iota(jnp.int32, z.shape, 1))
            log_om = jnp.where(below, log_om, 0.0)
        hi, lo = _split_bf16(log_om)
        between = _dot(hi, suffix_ones) + _dot(lo, suffix_ones) + later
        a = jnp.exp(z - softplus + between)
        if diagonal:
            a = jnp.where(below, a, 0.0)
        o = o + _dot(a.astype(BF16), vj)
        later = later + jnp.sum(log_om, axis=1, keepdims=True)
        return o, later

    def block_if_live(j, r0, o, later):
        def run(state):
            return block(j, q_ref[0, r0:, :], state, False)

        return lax.cond(jnp.max(later[r0:]) > SB_UNDERFLOW_LOG, run, lambda state: state, (o[r0:], later[r0:]))

    def splice(full, part, r0, r1):
        pieces = [full[:r0]] * (r0 > 0) + [part] + [full[r1:]] * (r1 < tq)
        return jnp.concatenate(pieces, axis=0) if len(pieces) > 1 else part

    o = jnp.zeros((tq, HEAD_DIM), F32)
    later = jnp.zeros((tq, 1), F32)
    for b in reversed(range(own_blocks)):
        j = i * own_blocks + b
        r0, r1 = b * tk, min(b * tk + tk + SB_NEAR_ROWS, tq)
        if r1 < tq:
            o_far, later_far = block_if_live(j, r1, o, later)
            o, later = splice(o, o_far, r1, tq), splice(later, later_far, r1, tq)
        o_b, later_b = block(j, q_ref[0, r0:r1, :], (o[r0:r1], later[r0:r1]), True)
        o, later = splice(o, o_b, r0, r1), splice(later, later_b, r0, r1)

    def more(state):
        j, _, later = state
        return (j >= 0) & (jnp.max(later) > SB_UNDERFLOW_LOG)

    def step(state):
        j, o, later = state
        o_far, later_far = block_if_live(j, SB_NEAR_ROWS, o, later)
        o_top, later_top = block(j, q_ref[0, :SB_NEAR_ROWS, :], (o[:SB_NEAR_ROWS], later[:SB_NEAR_ROWS]), False)
        o = jnp.concatenate([o_top, o_far], axis=0)
        later = jnp.concatenate([later_top, later_far], axis=0)
        return j - 1, o, later

    _, o, _ = lax.while_loop(more, step, (i * own_blocks - 1, o, later))
    o_ref[0] = o.astype(o_ref.dtype)


def _sb_attention(qkv):
    b, s, _ = qkv.shape
    h = N_HEADS
    return pl.pallas_call(
        _sb_kernel,
        grid=(b, h, s // SB_TQ),
        in_specs=[
            pl.BlockSpec((1, SB_TQ, LANES), lambda bi, hi, i: (bi, i, hi)),
            pl.BlockSpec((1, s, LANES), lambda bi, hi, i: (bi, 0, h + hi)),
            pl.BlockSpec((1, s, LANES), lambda bi, hi, i: (bi, 0, 2 * h + hi)),
        ],
        out_specs=pl.BlockSpec((1, SB_TQ, LANES), lambda bi, hi, i: (bi, i, hi)),
        out_shape=jax.ShapeDtypeStruct((b, s, h * HEAD_DIM), BF16),
        compiler_params=_cparams("parallel", "parallel", "arbitrary"),
        name="stick_breaking_attn",
    )(qkv, qkv, qkv)


MOBA_TQ = 2048
MOBA_TK = 2048
MOBA_OWN_TK = 512
ONES_ROWS = 16


def _online_softmax_step_t(carry, s_t, v_t):
    m, acc = carry
    m_new = jnp.maximum(m, jnp.max(s_t, axis=0, keepdims=True))
    p_t = jnp.exp2(s_t - m_new)
    acc = jnp.exp2(m - m_new) * acc + _dot(v_t, p_t.astype(BF16))
    return m_new, acc


def _flash_loop_t(n_steps, q_aug, keys_fn, values_t_fn, carry):
    def body(n, carry):
        return _online_softmax_step_t(carry, _dot_nt(keys_fn(n), q_aug), values_t_fn(n))

    return lax.fori_loop(0, n_steps, body, carry)


def _transposed_values_with_ones(v):
    v_t = v.astype(F32).T.astype(BF16)
    return jnp.concatenate([v_t, jnp.ones((ONES_ROWS, v.shape[0]), BF16)], axis=0)


def _moba_kernel(q_ref, k_ref, v_ref, o_ref, kaug_ref, kmean_ref, vt_ref, *, n_blk):
    i = pl.program_id(2)
    t, bs = MOBA_TQ, MOBA_BLOCK
    s_len = k_ref.shape[1]

    @pl.when(i == 0)
    def _():
        k = k_ref[0]
        kaug_ref[:, :LANES] = k
        blk = lax.broadcasted_iota(jnp.int32, (s_len, LANES), 0) // bs
        lane = lax.broadcasted_iota(jnp.int32, (s_len, LANES), 1)
        kaug_ref[:, LANES:] = (blk == lane).astype(BF16)
        kmean_ref[...] = jnp.mean(k.astype(F32).reshape(n_blk, bs, LANES), axis=1)
        for c in range(s_len // MOBA_TK):
            vt_ref[c] = _transposed_values_with_ones(v_ref[0, c * MOBA_TK:(c + 1) * MOBA_TK, :])

    q = q_ref[0]
    km_hi, km_lo = _split_bf16(kmean_ref[...])
    gate = _dot_nt(km_hi, q) + _dot_nt(km_lo, q)
    blk_id = lax.broadcasted_iota(jnp.int32, (n_blk, t), 0)
    cur = i * (t // bs) + lax.broadcasted_iota(jnp.int32, (n_blk, t), 1) // bs
    past = blk_id < cur
    gate = jnp.where(past, gate, -jnp.inf)
    visible = (_top_k_mask_t(gate, MOBA_TOPK) & past) | (blk_id == cur)
    sel_bias = jnp.where(visible, 0.0, NEG_INF)
    sel_bias = jnp.concatenate([sel_bias, jnp.zeros((LANES - n_blk, t), F32)], axis=0)
    q_aug = jnp.concatenate([q, sel_bias.T.astype(BF16)], axis=1)

    tk = MOBA_TK

    def keys(n):
        return kaug_ref[pl.ds(pl.multiple_of(n * tk, tk), tk), :]

    def values_t(n):
        return vt_ref[n]

    own = i * (t // tk)
    to = MOBA_OWN_TK
    causal = (lax.broadcasted_iota(jnp.int32, (to, t), 0) <= lax.broadcasted_iota(jnp.int32, (to, t), 1))
    carry = None
    for u in range(t // to):
        r0 = u * to
        k_u = kaug_ref[pl.ds(pl.multiple_of(i * t + r0, to), to), :]
        v_u = vt_ref[own + r0 // tk][:, r0 % tk:r0 % tk + to]
        s_t = jnp.where(causal[:, :t - r0], _dot_nt(k_u, q_aug[r0:]), NEG_INF)
        if carry is None:
            m0 = jnp.max(s_t, axis=0, keepdims=True)
            carry = (m0, _dot(v_u, jnp.exp2(s_t - m0).astype(BF16)))
        else:
            m_u, acc_u = _online_softmax_step_t((carry[0][:, r0:], carry[1][:, r0:]), s_t, v_u)
            carry = (jnp.concatenate([carry[0][:, :r0], m_u], axis=1),
                     jnp.concatenate([carry[1][:, :r0], acc_u], axis=1))

    _, acc = _flash_loop_t(own, q_aug, keys, values_t, carry)
    o_t = acc[:HEAD_DIM] * (1.0 / acc[HEAD_DIM:HEAD_DIM + 1])
    o_ref[0] = o_t.T.astype(o_ref.dtype)


def _moba_attention(qkv):
    b, s, _ = qkv.shape
    h = N_HEADS
    assert s % MOBA_TQ == 0 and MOBA_TQ % MOBA_BLOCK == 0
    n_blk = s // MOBA_BLOCK
    assert n_blk % 8 == 0 and n_blk <= LANES
    return pl.pallas_call(
        functools.partial(_moba_kernel, n_blk=n_blk),
        grid=(b, h, s // MOBA_TQ),
        in_specs=[
            pl.BlockSpec((1, MOBA_TQ, LANES), lambda bi, hi, i: (bi, i, hi)),
            pl.BlockSpec((1, s, LANES), lambda bi, hi, i: (bi, 0, h + hi)),
            pl.BlockSpec((1, s, LANES), lambda bi, hi, i: (bi, 0, 2 * h + hi)),
        ],
        out_specs=pl.BlockSpec((1, MOBA_TQ, LANES), lambda bi, hi, i: (bi, i, hi)),
        out_shape=jax.ShapeDtypeStruct((b, s, h * HEAD_DIM), BF16),
        scratch_shapes=[pltpu.VMEM((s, 2 * LANES), BF16), pltpu.VMEM((n_blk, LANES), F32),
                        pltpu.VMEM((s // MOBA_TK, HEAD_DIM + ONES_ROWS, MOBA_TK), BF16)],
        compiler_params=_cparams("parallel", "parallel", "arbitrary"),
        name="moba_attn",
    )(qkv, qkv, qkv)


NSA_TQ = 512
NSA_TK = 1024
def _cmp_kernel(*refs):
    st = NSA_CMP_STRIDE
    x_refs, (pos_ref, w_ref, ok_ref, ov_ref) = refs[:2 * st], refs[2 * st:]
    n16 = ok_ref.shape[2]
    for t, o_ref in enumerate((ok_ref, ov_ref)):
        first = jnp.zeros((n16, HEAD_DIM), F32)
        second = jnp.zeros((n16, HEAD_DIM), F32)
        for l in range(st):
            x = x_refs[t * st + l][0].astype(F32)
            first = first + _dot((x + pos_ref[t, l:l + 1, :]).astype(BF16), w_ref[t, l])
            second = second + _dot((x + pos_ref[t, st + l:st + l + 1, :]).astype(BF16), w_ref[t, st + l])
        out = first + pltpu.roll(second, n16 - 1, 0)
        o_ref[0, 0] = (out if t == 0 else out.T).astype(o_ref.dtype)


def _nsa_compress(slabs, batch, cmp_pos, w_cmp):
    g, st = NSA_KV_GROUPS, NSA_CMP_STRIDE
    b = batch
    n16 = slabs.shape[1] // b

    def slab_spec(first_block, l):
        return pl.BlockSpec((1, n16, LANES), lambda bi, gi: (l, bi, first_block + gi))

    x_specs = [slab_spec(t * g, l) for t in range(2) for l in range(st)]
    pos = cmp_pos.astype(F32)
    w = w_cmp.astype(BF16)
    return pl.pallas_call(
        _cmp_kernel,
        grid=(b, g),
        in_specs=x_specs + [_const_spec(pos.shape), _const_spec(w.shape)],
        out_specs=[pl.BlockSpec((1, 1, n16, HEAD_DIM), lambda bi, gi: (bi, gi, 0, 0)),
                   pl.BlockSpec((1, 1, HEAD_DIM, n16), lambda bi, gi: (bi, gi, 0, 0))],
        out_shape=[jax.ShapeDtypeStruct((b, g, n16, HEAD_DIM), BF16),
                   jax.ShapeDtypeStruct((b, g, HEAD_DIM, n16), BF16)],
        compiler_params=_cparams("parallel", "parallel"),
        name="nsa_compress",
    )(*([slabs] * (2 * st)), pos, w)


def _top_k_mask_t(score_t, k):
    n, q = score_t.shape
    work = score_t
    taken = jnp.zeros((1, q), F32)
    level = jnp.full((1, q), jnp.inf, F32)
    above = jnp.zeros((1, q), F32)
    for _ in range(k):
        best = jnp.max(work, axis=0, keepdims=True)
        hit = work == best
        active = taken < k
        level = jnp.where(active, best, level)
        above = jnp.where(active, taken, above)
        taken = taken + jnp.sum(hit.astype(F32), axis=0, keepdims=True)
        work = jnp.where(hit, -jnp.inf, work)
    tie = score_t == level
    lower = (lax.broadcasted_iota(jnp.int32, (n, n), 1)
             < lax.broadcasted_iota(jnp.int32, (n, n), 0)).astype(BF16)
    ties_before = _dot(lower, tie.astype(BF16))
    return (score_t > level) | (tie & (ties_before < k - above))


def _nsa_kernel(q_ref, kc_ref, vc_ref, ks_ref, vs_ref, kw_ref, vw_ref, gate_ref, ov_ref, o_ref,
                ksaug_ref, vst_ref, vwt_ref, *, n_sel):
    i = pl.program_id(2)
    tq, tk, hg = NSA_TQ, NSA_TK, NSA_HEADS_PER_GROUP
    s_len = ks_ref.shape[1]
    n_cmp_pad = kc_ref.shape[2]
    c0 = i * tq

    @pl.when(i == 0)
    def _():
        ksaug_ref[:, :LANES] = ks_ref[0]
        blk = lax.broadcasted_iota(jnp.int32, (s_len, LANES), 0) // NSA_SEL_BLOCK
        lane = lax.broadcasted_iota(jnp.int32, (s_len, LANES), 1)
        ksaug_ref[:, LANES:] = (blk == lane).astype(BF16)
        for c in range(s_len // tk):
            vst_ref[c] = _transposed_values_with_ones(vs_ref[0, c * tk:(c + 1) * tk, :])
        for c in range(s_len // tq):
            vwt_ref[c] = _transposed_values_with_ones(vw_ref[0, c * tq:(c + 1) * tq, :])

    q_all = q_ref[0]
    q_heads = [q_all[:, h * LANES:(h + 1) * LANES] for h in range(hg)]

    def key_iota(n):
        return lax.broadcasted_iota(jnp.int32, (n, tq), 0)

    def q_pos(n):
        return c0 + lax.broadcasted_iota(jnp.int32, (n, tq), 1)

    q4 = jnp.concatenate(q_heads, axis=0)

    def all_heads(mask):
        return jnp.concatenate([mask] * hg, axis=1)

    cmp_mask = all_heads(key_iota(n_cmp_pad) * NSA_CMP_STRIDE + (NSA_CMP_BLOCK - 1) <= q_pos(n_cmp_pad))
    z = jnp.where(cmp_mask, _dot_nt(kc_ref[0, 0], q4), NEG_INF)
    e = jnp.where(cmp_mask, jnp.exp2(z - jnp.max(z, axis=0, keepdims=True)), 0.0)
    l = jnp.sum(e, axis=0, keepdims=True)
    inv = 1.0 / jnp.where(l > 0.0, l, 1.0)
    o_c = _dot(vc_ref[0, 0], e.astype(BF16)) * inv
    p_c = e * inv
    p_sum = p_c[:, 0:tq]
    for h in range(1, hg):
        p_sum = p_sum + p_c[:, h * tq:(h + 1) * tq]

    ps_hi, ps_lo = _split_bf16(p_sum)
    imp = _dot(ov_ref[...], ps_hi) + _dot(ov_ref[...], ps_lo)
    blk = key_iota(LANES)
    cur = q_pos(LANES) // NSA_SEL_BLOCK
    forced = (blk == 0) | (blk == cur) | (blk == cur - 1)
    score = jnp.where(blk <= cur, imp + NSA_FORCE_BONUS * forced.astype(F32), -1.0)
    score = jnp.where(blk < n_sel, score, -jnp.inf)
    chosen = _top_k_mask_t(score, min(NSA_SEL_TOPK, n_sel))
    sel_bias = jnp.where(chosen, 0.0, NEG_INF).T.astype(BF16)
    q_aug = jnp.concatenate([q4, jnp.concatenate([sel_bias] * hg, axis=0)], axis=1)

    def keys(j):
        return ksaug_ref[pl.ds(pl.multiple_of(j * tk, tk), tk), :]

    def values_t(j):
        return vst_ref[j]

    jd = c0 // tk
    own_start = pl.multiple_of(jd * tk, tk)

    def own_step(n_keys):
        def run():
            causal = own_start + key_iota(n_keys) <= q_pos(n_keys)
            s_t = jnp.where(all_heads(causal), _dot_nt(ksaug_ref[pl.ds(own_start, n_keys), :], q_aug), NEG_INF)
            m0 = jnp.max(s_t, axis=0, keepdims=True)
            return m0, _dot(vst_ref[jd][:, :n_keys], jnp.exp2(s_t - m0).astype(BF16))
        return run

    carry = lax.cond(c0 == own_start, own_step(tq), own_step(tk))
    _, acc = _flash_loop_t(jd, q_aug, keys, values_t, carry)
    o_s = acc[:HEAD_DIM] * (1.0 / acc[HEAD_DIM:HEAD_DIM + 1])

    span = tq + NSA_WINDOW
    wstart = pl.multiple_of(jnp.maximum(c0 - NSA_WINDOW, 0), tq)
    w_chunk = wstart // tq
    gap = q_pos(span) - (wstart + key_iota(span))
    win_mask = all_heads((gap >= 0) & (gap < NSA_WINDOW))
    z = jnp.where(win_mask, _dot_nt(kw_ref[0, pl.ds(wstart, span), :], q4), NEG_INF)
    e = jnp.exp2(z - jnp.max(z, axis=0, keepdims=True)).astype(BF16)
    acc_w = _dot(vwt_ref[w_chunk], e[0:tq])
    for c in range(1, span // tq):
        acc_w = acc_w + _dot(vwt_ref[w_chunk + c], e[c * tq:(c + 1) * tq])
    o_w = acc_w[:HEAD_DIM] * (1.0 / acc_w[HEAD_DIM:HEAD_DIM + 1])

    gates_t = (1.0 / (1.0 + jnp.exp(-gate_ref[0]))).T
    for h in range(hg):
        out = jnp.zeros((HEAD_DIM, tq), F32)
        for branch, o_b in enumerate((o_c, o_s, o_w)):
            r = branch * hg + h
            out = out + gates_t[r:r + 1, :] * o_b[:, h * tq:(h + 1) * tq]
        o_ref[0, :, h * LANES:(h + 1) * LANES] = out.T.astype(o_ref.dtype)


def _nsa_attention(main, gate_logits, kc_cmp, vc_cmp_t, overlap_t):
    b, s, _ = main.shape
    g, hg = NSA_KV_GROUPS, NSA_HEADS_PER_GROUP
    n_sel = s // NSA_SEL_BLOCK
    assert s % NSA_TK == 0 and n_sel <= LANES
    n16 = kc_cmp.shape[2]
    q_blocks = N_HEADS

    def kv_spec(which):
        return pl.BlockSpec((1, s, LANES), lambda bi, gi, i: (bi, 0, q_blocks + which * g + gi),
                            pipeline_mode=pl.Buffered(1))

    return pl.pallas_call(
        functools.partial(_nsa_kernel, n_sel=n_sel),
        grid=(b, g, s // NSA_TQ),
        in_specs=[
            pl.BlockSpec((1, NSA_TQ, hg * LANES), lambda bi, gi, i: (bi, i, gi)),
            pl.BlockSpec((1, 1, n16, HEAD_DIM), lambda bi, gi, i: (bi, gi, 0, 0)),
            pl.BlockSpec((1, 1, HEAD_DIM, n16), lambda bi, gi, i: (bi, gi, 0, 0)),
            kv_spec(2), kv_spec(3), kv_spec(4), kv_spec(5),
            pl.BlockSpec((1, NSA_TQ, LANES), lambda bi, gi, i: (bi, i, gi)),
            _const_spec(overlap_t.shape),
        ],
        out_specs=pl.BlockSpec((1, NSA_TQ, hg * LANES), lambda bi, gi, i: (bi, i, gi)),
        out_shape=jax.ShapeDtypeStruct((b, s, N_HEADS * HEAD_DIM), BF16),
        scratch_shapes=[pltpu.VMEM((s, 2 * LANES), BF16),
                        pltpu.VMEM((s // NSA_TK, HEAD_DIM + ONES_ROWS, NSA_TK), BF16),
                        pltpu.VMEM((s // NSA_TQ, HEAD_DIM + ONES_ROWS, NSA_TQ), BF16)],
        compiler_params=_cparams("parallel", "parallel", "arbitrary"),
        name="nsa_attn",
    )(main, kc_cmp, vc_cmp_t, main, main, main, main, gate_logits, overlap_t)


def _nsa_overlap_t(seq):
    n_cmp = (seq - NSA_CMP_BLOCK) // NSA_CMP_STRIDE + 1
    n_sel = seq // NSA_SEL_BLOCK
    cmp_start = np.arange(seq // NSA_CMP_STRIDE) * NSA_CMP_STRIDE
    sel_start = np.arange(LANES) * NSA_SEL_BLOCK
    ov = ((cmp_start[:, None] < sel_start[None, :] + NSA_SEL_BLOCK)
          & (cmp_start[:, None] + NSA_CMP_BLOCK > sel_start[None, :]))
    ov &= (np.arange(seq // NSA_CMP_STRIDE)[:, None] < n_cmp) & (np.arange(LANES)[None, :] < n_sel)
    return jnp.asarray(ov.T, dtype=BF16)


def _glu_proj_kernel(x_ref, g_ref, w_ref, o_ref):
    h = _rms(x_ref[...], g_ref[...]).astype(BF16)
    d = o_ref.shape[1]
    for c in range(d // PROJ_CHUNK):
        a = _dot(h, w_ref[:, c * PROJ_CHUNK:(c + 1) * PROJ_CHUNK])
        gate = _dot(h, w_ref[:, d + c * PROJ_CHUNK:d + (c + 1) * PROJ_CHUNK])
        o_ref[:, c * PROJ_CHUNK:(c + 1) * PROJ_CHUNK] = (a / (1.0 + jnp.exp(-gate))).astype(o_ref.dtype)


def _glu_project(x2d, gain, w):
    t_rows, d = x2d.shape
    return pl.pallas_call(
        _glu_proj_kernel,
        grid=(t_rows // ROW_TILE,),
        in_specs=[pl.BlockSpec((ROW_TILE, d), lambda i: (i, 0)), _const_spec((1, d)), _const_spec(w.shape)],
        out_specs=pl.BlockSpec((ROW_TILE, d), lambda i: (i, 0)),
        out_shape=jax.ShapeDtypeStruct((t_rows, d), BF16),
        compiler_params=_cparams("parallel"),
        name="norm_glu_proj",
    )(x2d, gain.reshape(1, d), w)


CONV_TILE = 256
SUBLANES = 8
CONV_PHASE_ROWS = CONV_TILE + CONV_HALO - SUBLANES


def _conv_kernel(u_ref, halo_ref, dw_ref, db_ref, lg_ref, lb_ref, o_ref, ext_ref, phase_ref):
    i = pl.program_id(1)
    halo = halo_ref[0].astype(F32)
    ext_ref[0:CONV_HALO, :] = jnp.where(i == 0, 0.0, halo)
    ext_ref[CONV_HALO:, :] = u_ref[0].astype(F32)
    for b in range(1, SUBLANES):
        phase_ref[b - 1] = ext_ref[pl.ds(b, CONV_PHASE_ROWS), :]
    lead = CONV_HALO - (CONV_WIDTH - 1)
    for r in range(CONV_TILE // CONV_ROWS):
        acc = jnp.zeros((CONV_ROWS, D_MODEL), F32) + db_ref[...]
        for w in range(CONV_WIDTH):
            shift = (lead + w) % SUBLANES
            start = r * CONV_ROWS + lead + w - shift
            src = ext_ref if shift == 0 else phase_ref.at[shift - 1]
            acc = acc + dw_ref[w:w + 1, :] * src[pl.ds(start, CONV_ROWS), :]
        mu = jnp.mean(acc, axis=-1, keepdims=True)
        cen = acc - mu
        var = jnp.mean(cen * cen, axis=-1, keepdims=True)
        un = cen * lax.rsqrt(var + NORM_EPS) * lg_ref[...] + lb_ref[...]
        o_ref[0, r * CONV_ROWS:(r + 1) * CONV_ROWS, :] = (un / (1.0 + jnp.exp(-un))).astype(o_ref.dtype)


def _conv_ln_swish(u, dw_w, dw_b, ln_g, ln_b):
    b, s, d = u.shape
    halo_per_tile = CONV_TILE // CONV_HALO
    dw = jnp.concatenate([dw_w, jnp.zeros((1, d), F32)], axis=0)
    return pl.pallas_call(
        _conv_kernel,
        grid=(b, s // CONV_TILE),
        in_specs=[
            pl.BlockSpec((1, CONV_TILE, d), lambda bi, i: (bi, i, 0)),
            pl.BlockSpec((1, CONV_HALO, d), lambda bi, i: (bi, jnp.maximum(i * halo_per_tile - 1, 0), 0)),
            _const_spec(dw.shape), _const_spec((1, d)), _const_spec((1, d)), _const_spec((1, d)),
        ],
        out_specs=pl.BlockSpec((1, CONV_TILE, d), lambda bi, i: (bi, i, 0)),
        out_shape=jax.ShapeDtypeStruct((b, s, d), BF16),
        scratch_shapes=[pltpu.VMEM((CONV_TILE + CONV_HALO, d), F32),
                        pltpu.VMEM((SUBLANES - 1, CONV_PHASE_ROWS, d), F32)],
        compiler_params=_cparams("parallel", "parallel"),
        name="conv_ln_swish",
    )(u, u, dw, dw_b.reshape(1, d), ln_g.reshape(1, d), ln_b.reshape(1, d))


def _rope_tables(seq):
    half = HEAD_DIM // 2
    inv_freq = ROPE_THETA ** (-jnp.arange(half, dtype=F32) / half)
    ang = jnp.arange(seq, dtype=F32)[:, None] * inv_freq[None, :]
    cos, sin = jnp.cos(ang), jnp.sin(ang)
    return jnp.concatenate([cos, cos], axis=1), jnp.concatenate([-sin, sin], axis=1)


def _nsa_mixer(x2d, norm_gain, w_in, q_gain, k_gain, cmp_pos, w_cmp, tables, batch, seq):
    g, dh = NSA_KV_GROUPS, HEAD_DIM
    n_main = (N_HEADS + 6 * g) * dh
    scale = dh ** -0.5 * LOG2_E
    w_main = w_in[:, :n_main].astype(BF16)
    hg = NSA_HEADS_PER_GROUP
    w_g = w_in[:, n_main:].reshape(-1, 3, g, hg).transpose(0, 2, 1, 3).reshape(-1, g, 3 * hg)
    w_gate = jnp.pad(w_g, ((0, 0), (0, 0), (0, LANES - 3 * hg))).reshape(-1, g * LANES).astype(BF16)
    ones = jnp.ones((dh,), F32)
    head_gains = jnp.stack([q_gain * scale] * N_HEADS + [k_gain[0]] * g + [ones] * g
                           + [k_gain[1]] * g + [ones] * g + [k_gain[2]] * g + [ones] * g)
    rope_blocks = [True] * N_HEADS + [True] * g + [False] * g + [True] * g + [False] * g + [True] * g + [False] * g
    kc_vc_chunk = N_HEADS * dh // PROJ_CHUNK
    assert 2 * g * dh == PROJ_CHUNK
    main, gate_logits, slabs = _project(x2d, norm_gain, w_main, *tables, head_gains, rope_blocks, seq,
                                        w_gate=w_gate, slab_chunk=kc_vc_chunk)
    main = main.reshape(batch, seq, n_main)
    gate_logits = gate_logits.reshape(batch, seq, g * LANES)

    kc_cmp, vc_cmp_t = _nsa_compress(slabs, batch, cmp_pos, w_cmp)
    out = _nsa_attention(main, gate_logits, kc_cmp, vc_cmp_t, _nsa_overlap_t(seq))
    return out.reshape(batch * seq, N_HEADS * dh)


def _qkv_weight_with_scaled_q(w_in):
    n_q = N_HEADS * HEAD_DIM
    return jnp.concatenate([w_in[:, :n_q] * HEAD_DIM ** -0.5, w_in[:, n_q:]], axis=1).astype(BF16)


def _sb_mixer(x2d, norm_gain, w_in, tables, batch, seq):
    n = 3 * N_HEADS * HEAD_DIM
    head_gains = jnp.ones((n // LANES, HEAD_DIM), F32)
    qkv = _project(x2d, norm_gain, _qkv_weight_with_scaled_q(w_in), *tables, head_gains,
                   [False] * (n // LANES), seq)
    return _sb_attention(qkv.reshape(batch, seq, n)).reshape(batch * seq, N_HEADS * HEAD_DIM)


def _conv_mixer(x2d, norm_gain, w_in, dw_w, dw_b, ln_g, ln_b, batch, seq):
    u = _glu_project(x2d, norm_gain, w_in.astype(BF16))
    a = _conv_ln_swish(u.reshape(batch, seq, D_MODEL), dw_w, dw_b, ln_g, ln_b)
    return a.reshape(batch * seq, D_MODEL)


def _moba_mixer(x2d, norm_gain, w_in, q_gain, k_gain, tables, batch, seq):
    n = 3 * N_HEADS * HEAD_DIM
    ones = jnp.ones((HEAD_DIM,), F32)
    head_gains = jnp.stack([q_gain * (HEAD_DIM ** -0.5 * LOG2_E)] * N_HEADS + [k_gain] * N_HEADS + [ones] * N_HEADS)
    rope_blocks = [True] * (2 * N_HEADS) + [False] * N_HEADS
    qkv = _project(x2d, norm_gain, w_in.astype(BF16), *tables, head_gains, rope_blocks, seq)
    return _moba_attention(qkv.reshape(batch, seq, n)).reshape(batch * seq, N_HEADS * HEAD_DIM)


def kernel(x, attn_norm, mlp_norm, mlp_w_up, mlp_w_down, nsa_w_in, nsa_q_norm, nsa_k_norm, nsa_cmp_pos, nsa_w_cmp, nsa_w_out, sb_w_in, sb_w_out, conv_w_in, conv_dw_w, conv_dw_b, conv_ln_g, conv_ln_b, conv_w_out, moba_w_in, moba_q_norm, moba_k_norm, moba_w_out):
    batch, seq, d = x.shape
    depth = attn_norm.shape[0]
    tables = _rope_tables(seq)
    w_up_all, w_down_all = mlp_w_up.astype(BF16), mlp_w_down.astype(BF16)
    x2d = x.reshape(batch * seq, d)
    for i in range(depth):
        m, j = i % 4, i // 4
        if m == 0:
            a = _nsa_mixer(x2d, attn_norm[i], nsa_w_in[j], nsa_q_norm[j], nsa_k_norm[j], nsa_cmp_pos[j],
                           nsa_w_cmp[j], tables, batch, seq)
            w_out = nsa_w_out[j]
        elif m == 1:
            a = _sb_mixer(x2d, attn_norm[i], sb_w_in[j], tables, batch, seq)
            w_out = sb_w_out[j]
        elif m == 2:
            a = _conv_mixer(x2d, attn_norm[i], conv_w_in[j], conv_dw_w[j], conv_dw_b[j], conv_ln_g[j],
                            conv_ln_b[j], batch, seq)
            w_out = conv_w_out[j]
        else:
            a = _moba_mixer(x2d, attn_norm[i], moba_w_in[j], moba_q_norm[j], moba_k_norm[j], tables, batch, seq)
            w_out = moba_w_out[j]
        x2d = _mixer_out_and_mlp(x2d, a, w_out.astype(BF16), mlp_norm[i], w_up_all, w_down_all, i)
    return x2d.reshape(batch, seq, d)
```

```python
import functools

import numpy as np
import jax
import jax.numpy as jnp
from jax import lax
from jax.experimental import pallas as pl
from jax.experimental.pallas import tpu as pltpu

F32 = jnp.float32
BF16 = jnp.bfloat16

D_MODEL = 1024
N_HEADS = 8
HEAD_DIM = 128
D_FF = 4 * D_MODEL
ROPE_THETA = 10000.0
NORM_EPS = 1e-6
NEG_INF = -1e30
LOG2_E = 1.4426950408889634

NSA_KV_GROUPS = 2
NSA_HEADS_PER_GROUP = N_HEADS // NSA_KV_GROUPS
NSA_CMP_BLOCK = 32
NSA_CMP_STRIDE = 16
NSA_SEL_BLOCK = 64
NSA_SEL_TOPK = 16
NSA_WINDOW = 512
NSA_FORCE_BONUS = 1000.0

CONV_WIDTH = 31
MOBA_BLOCK = 256
MOBA_TOPK = 3

LANES = 128
VMEM_LIMIT_BYTES = 56 * 1024 * 1024
ROW_TILE = 512
PROJ_CHUNK = 512
FF_CHUNK = 1024
CONV_HALO = 32
CONV_ROWS = 32


def _cparams(*sem):
    return pltpu.CompilerParams(dimension_semantics=sem, vmem_limit_bytes=VMEM_LIMIT_BYTES)


def _const_spec(shape):
    zeros = (0,) * len(shape)
    return pl.BlockSpec(shape, lambda *_: zeros, pipeline_mode=pl.Buffered(1))


def _rms(x, gain):
    return x * lax.rsqrt(jnp.mean(x * x, axis=-1, keepdims=True) + NORM_EPS) * gain


def _dot(a, b):
    return jnp.dot(a, b, preferred_element_type=F32)


def _dot_nt(a, b):
    return lax.dot_general(a, b, (((1,), (1,)), ((), ())), preferred_element_type=F32)


def _split_bf16(x):
    hi = x.astype(BF16)
    lo = (x - hi.astype(F32)).astype(BF16)
    return hi, lo


def _proj_kernel(*refs, rope_blocks, nsa_extras, slab_chunk):
    if nsa_extras:
        x_ref, g_ref, w_ref, cos_ref, sin_ref, hg_ref, wg_ref, o_ref, og_ref, oslab_ref, slab_ref = refs
    else:
        x_ref, g_ref, w_ref, cos_ref, sin_ref, hg_ref, o_ref = refs
    h = _rms(x_ref[...], g_ref[...]).astype(BF16)
    blocks_per_chunk = PROJ_CHUNK // LANES
    for c in range(len(rope_blocks) // blocks_per_chunk):
        y = _dot(h, w_ref[:, c * PROJ_CHUNK:(c + 1) * PROJ_CHUNK])
        chunk_flags = rope_blocks[c * blocks_per_chunk:(c + 1) * blocks_per_chunk]
        keep_slabs = nsa_extras and c == slab_chunk
        if not any(chunk_flags) and not keep_slabs:
            o_ref[:, c * PROJ_CHUNK:(c + 1) * PROJ_CHUNK] = y.astype(o_ref.dtype)
            continue
        for k, flag in enumerate(chunk_flags):
            b = c * blocks_per_chunk + k
            yb = y[:, k * LANES:(k + 1) * LANES]
            if flag:
                yb = _rms(yb, hg_ref[b:b + 1, :])
                yb = yb * cos_ref[...] + pltpu.roll(yb, HEAD_DIM // 2, 1) * sin_ref[...]
            o_ref[:, b * LANES:(b + 1) * LANES] = yb.astype(o_ref.dtype)
            if keep_slabs:
                slab_ref[k] = yb
    if nsa_extras:
        og_ref[...] = _dot(h, wg_ref[...])
        st = NSA_CMP_STRIDE
        for l in range(st):
            for k in range(blocks_per_chunk):
                rows_l = slab_ref[k, pl.ds(l, ROW_TILE // st, stride=st), :]
                oslab_ref[l, :, k * LANES:(k + 1) * LANES] = rows_l.astype(oslab_ref.dtype)


def _project(x2d, gain, w, cos_t, sin_t, head_gains, rope_blocks, seq, w_gate=None, slab_chunk=None):
    t_rows, d = x2d.shape
    n = w.shape[1]
    assert n % PROJ_CHUNK == 0 and len(rope_blocks) == n // LANES and seq % ROW_TILE == 0
    seq_tiles = seq // ROW_TILE
    has_gate = w_gate is not None
    in_specs = [
        pl.BlockSpec((ROW_TILE, d), lambda i: (i, 0)),
        _const_spec((1, d)),
        _const_spec((d, n)),
        pl.BlockSpec((ROW_TILE, LANES), lambda i: (i % seq_tiles, 0)),
        pl.BlockSpec((ROW_TILE, LANES), lambda i: (i % seq_tiles, 0)),
        _const_spec(head_gains.shape),
    ]
    args = [x2d, gain.reshape(1, d), w, cos_t, sin_t, head_gains]
    out_shape = [jax.ShapeDtypeStruct((t_rows, n), BF16)]
    out_specs = [pl.BlockSpec((ROW_TILE, n), lambda i: (i, 0))]
    if has_gate:
        in_specs.append(_const_spec(w_gate.shape))
        args.append(w_gate)
        out_shape.append(jax.ShapeDtypeStruct((t_rows, w_gate.shape[1]), F32))
        out_specs.append(pl.BlockSpec((ROW_TILE, w_gate.shape[1]), lambda i: (i, 0)))
        st = NSA_CMP_STRIDE
        out_shape.append(jax.ShapeDtypeStruct((st, t_rows // st, PROJ_CHUNK), BF16))
        out_specs.append(pl.BlockSpec((st, ROW_TILE // st, PROJ_CHUNK), lambda i: (0, i, 0)))
    outs = pl.pallas_call(
        functools.partial(_proj_kernel, rope_blocks=tuple(rope_blocks), nsa_extras=has_gate,
                          slab_chunk=slab_chunk),
        grid=(t_rows // ROW_TILE,),
        in_specs=in_specs,
        out_specs=out_specs,
        out_shape=out_shape,
        scratch_shapes=[pltpu.VMEM((PROJ_CHUNK // LANES, ROW_TILE, LANES), F32)] if has_gate else [],
        compiler_params=_cparams("parallel"),
        name="norm_proj",
    )(*args)
    return outs if has_gate else outs[0]


def _mlp_kernel(x_ref, a_ref, wo_ref, g_ref, wup_ref, wdn_ref, o_ref):
    x1 = x_ref[...] + _dot(a_ref[...], wo_ref[...])
    h = _rms(x1, g_ref[...]).astype(BF16)
    acc = x1
    for c in range(D_FF // FF_CHUNK):
        u = _dot(h, wup_ref[0, :, c * FF_CHUNK:(c + 1) * FF_CHUNK])
        act = jnp.square(jnp.maximum(u, 0.0)).astype(BF16)
        acc = acc + _dot(act, wdn_ref[0, c * FF_CHUNK:(c + 1) * FF_CHUNK, :])
    o_ref[...] = acc


def _mixer_out_and_mlp(x2d, a2d, w_out, gain, w_up_all, w_down_all, layer):
    t_rows, d = x2d.shape

    def layer_spec(shape):
        return pl.BlockSpec((1,) + shape, lambda i: (layer, 0, 0), pipeline_mode=pl.Buffered(1))

    return pl.pallas_call(
        _mlp_kernel,
        grid=(t_rows // ROW_TILE,),
        in_specs=[
            pl.BlockSpec((ROW_TILE, d), lambda i: (i, 0)),
            pl.BlockSpec((ROW_TILE, d), lambda i: (i, 0)),
            _const_spec((d, d)),
            _const_spec((1, d)),
            layer_spec((d, D_FF)),
            layer_spec((D_FF, d)),
        ],
        out_specs=pl.BlockSpec((ROW_TILE, d), lambda i: (i, 0)),
        out_shape=jax.ShapeDtypeStruct((t_rows, d), F32),
        compiler_params=_cparams("parallel"),
        name="outproj_mlp",
    )(x2d, a2d, w_out, gain.reshape(1, d), w_up_all, w_down_all)


SB_TQ = 1024
SB_TK = 256
SB_NEAR_ROWS = 256
SB_UNDERFLOW_LOG = -104.0


def _sb_kernel(q_ref, k_ref, v_ref, o_ref):
    i = pl.program_id(2)
    tq, tk = SB_TQ, SB_TK
    own_blocks = tq // tk
    suffix_ones = (lax.broadcasted_iota(jnp.int32, (tk, tk), 0)
                   > lax.broadcasted_iota(jnp.int32, (tk, tk), 1)).astype(BF16)

    def block(j, q, carry, diagonal):
        o, later = carry
        start = pl.multiple_of(j * tk, tk)
        kj = k_ref[0, pl.ds(start, tk), :]
        vj = v_ref[0, pl.ds(start, tk), :]
        z = _dot_nt(q, kj)
        softplus = jnp.maximum(z, 0.0) + jnp.log(1.0 + jnp.exp(-jnp.abs(z)))
        log_om = -softplus
        if diagonal:
            below = (lax.broadcasted_iota(jnp.int32, z.shape, 0)
                     > lax.broadcasted_iota(jnp.int32, z.shape, 1))
            log_om = jnp.where(below, log_om, 0.0)
        hi, lo = _split_bf16(log_om)
        between = _dot(hi, suffix_ones) + _dot(lo, suffix_ones) + later
        a = jnp.exp(z - softplus + between)
        if diagonal:
            a = jnp.where(below, a, 0.0)
        o = o + _dot(a.astype(BF16), vj)
        later = later + jnp.sum(log_om, axis=1, keepdims=True)
        return o, later

    def block_if_live(j, r0, o, later):
        def run(state):
            return block(j, q_ref[0, r0:, :], state, False)

        return lax.cond(jnp.max(later[r0:]) > SB_UNDERFLOW_LOG, run, lambda state: state, (o[r0:], later[r0:]))

    def splice(full, part, r0, r1):
        pieces = [full[:r0]] * (r0 > 0) + [part] + [full[r1:]] * (r1 < tq)
        return jnp.concatenate(pieces, axis=0) if len(pieces) > 1 else part

    o = jnp.zeros((tq, HEAD_DIM), F32)
    later = jnp.zeros((tq, 1), F32)
    for b in reversed(range(own_blocks)):
        j = i * own_blocks + b
        r0, r1 = b * tk, min(b * tk + tk + SB_NEAR_ROWS, tq)
        if r1 < tq:
            o_far, later_far = block_if_live(j, r1, o, later)
            o, later = splice(o, o_far, r1, tq), splice(later, later_far, r1, tq)
        o_b, later_b = block(j, q_ref[0, r0:r1, :], (o[r0:r1], later[r0:r1]), True)
        o, later = splice(o, o_b, r0, r1), splice(later, later_b, r0, r1)

    def more(state):
        j, _, later = state
        return (j >= 0) & (jnp.max(later) > SB_UNDERFLOW_LOG)

    def step(state):
        j, o, later = state
        o_far, later_far = block_if_live(j, SB_NEAR_ROWS, o, later)
        o_top, later_top = block(j, q_ref[0, :SB_NEAR_ROWS, :], (o[:SB_NEAR_ROWS], later[:SB_NEAR_ROWS]), False)
        o = jnp.concatenate([o_top, o_far], axis=0)
        later = jnp.concatenate([later_top, later_far], axis=0)
        return j - 1, o, later

    _, o, _ = lax.while_loop(more, step, (i * own_blocks - 1, o, later))
    o_ref[0] = o.astype(o_ref.dtype)


def _sb_attention(qkv):
    b, s, _ = qkv.shape
    h = N_HEADS
    return pl.pallas_call(
        _sb_kernel,
        grid=(b, h, s // SB_TQ),
        in_specs=[
            pl.BlockSpec((1, SB_TQ, LANES), lambda bi, hi, i: (bi, i, hi)),
            pl.BlockSpec((1, s, LANES), lambda bi, hi, i: (bi, 0, h + hi)),
            pl.BlockSpec((1, s, LANES), lambda bi, hi, i: (bi, 0, 2 * h + hi)),
        ],
        out_specs=pl.BlockSpec((1, SB_TQ, LANES), lambda bi, hi, i: (bi, i, hi)),
        out_shape=jax.ShapeDtypeStruct((b, s, h * HEAD_DIM), BF16),
        compiler_params=_cparams("parallel", "parallel", "arbitrary"),
        name="stick_breaking_attn",
    )(qkv, qkv, qkv)


MOBA_TQ = 2048
MOBA_TK = 2048
MOBA_OWN_TK = 512
ONES_ROWS = 16


def _online_softmax_step_t(carry, s_t, v_t):
    m, acc = carry
    m_new = jnp.maximum(m, jnp.max(s_t, axis=0, keepdims=True))
    p_t = jnp.exp2(s_t - m_new)
    acc = jnp.exp2(m - m_new) * acc + _dot(v_t, p_t.astype(BF16))
    return m_new, acc


def _flash_loop_t(n_steps, q_aug, keys_fn, values_t_fn, carry):
    def body(n, carry):
        return _online_softmax_step_t(carry, _dot_nt(keys_fn(n), q_aug), values_t_fn(n))

    return lax.fori_loop(0, n_steps, body, carry)


def _transposed_values_with_ones(v):
    v_t = v.astype(F32).T.astype(BF16)
    return jnp.concatenate([v_t, jnp.ones((ONES_ROWS, v.shape[0]), BF16)], axis=0)


def _moba_kernel(q_ref, k_ref, v_ref, o_ref, kaug_ref, kmean_ref, vt_ref, *, n_blk):
    i = pl.program_id(2)
    t, bs = MOBA_TQ, MOBA_BLOCK
    s_len = k_ref.shape[1]

    @pl.when(i == 0)
    def _():
        k = k_ref[0]
        kaug_ref[:, :LANES] = k
        blk = lax.broadcasted_iota(jnp.int32, (s_len, LANES), 0) // bs
        lane = lax.broadcasted_iota(jnp.int32, (s_len, LANES), 1)
        kaug_ref[:, LANES:] = (blk == lane).astype(BF16)
        kmean_ref[...] = jnp.mean(k.astype(F32).reshape(n_blk, bs, LANES), axis=1)
        for c in range(s_len // MOBA_TK):
            vt_ref[c] = _transposed_values_with_ones(v_ref[0, c * MOBA_TK:(c + 1) * MOBA_TK, :])

    q = q_ref[0]
    km_hi, km_lo = _split_bf16(kmean_ref[...])
    gate = _dot_nt(km_hi, q) + _dot_nt(km_lo, q)
    blk_id = lax.broadcasted_iota(jnp.int32, (n_blk, t), 0)
    cur = i * (t // bs) + lax.broadcasted_iota(jnp.int32, (n_blk, t), 1) // bs
    past = blk_id < cur
    gate = jnp.where(past, gate, -jnp.inf)
    visible = (_top_k_mask_t(gate, MOBA_TOPK) & past) | (blk_id == cur)
    sel_bias = jnp.where(visible, 0.0, NEG_INF)
    sel_bias = jnp.concatenate([sel_bias, jnp.zeros((LANES - n_blk, t), F32)], axis=0)
    q_aug = jnp.concatenate([q, sel_bias.T.astype(BF16)], axis=1)

    tk = MOBA_TK

    def keys(n):
        return kaug_ref[pl.ds(pl.multiple_of(n * tk, tk), tk), :]

    def values_t(n):
        return vt_ref[n]

    own = i * (t // tk)
    to = MOBA_OWN_TK
    causal = (lax.broadcasted_iota(jnp.int32, (to, t), 0) <= lax.broadcasted_iota(jnp.int32, (to, t), 1))
    carry = None
    for u in range(t // to):
        r0 = u * to
        k_u = kaug_ref[pl.ds(pl.multiple_of(i * t + r0, to), to), :]
        v_u = vt_ref[own + r0 // tk][:, r0 % tk:r0 % tk + to]
        s_t = jnp.where(causal[:, :t - r0], _dot_nt(k_u, q_aug[r0:]), NEG_INF)
        if carry is None:
            m0 = jnp.max(s_t, axis=0, keepdims=True)
            carry = (m0, _dot(v_u, jnp.exp2(s_t - m0).astype(BF16)))
        else:
            m_u, acc_u = _online_softmax_step_t((carry[0][:, r0:], carry[1][:, r0:]), s_t, v_u)
            carry = (jnp.concatenate([carry[0][:, :r0], m_u], axis=1),
                     jnp.concatenate([carry[1][:, :r0], acc_u], axis=1))

    _, acc = _flash_loop_t(own, q_aug, keys, values_t, carry)
    o_t = acc[:HEAD_DIM] * (1.0 / acc[HEAD_DIM:HEAD_DIM + 1])
    o_ref[0] = o_t.T.astype(o_ref.dtype)


def _moba_attention(qkv):
    b, s, _ = qkv.shape
    h = N_HEADS
    assert s % MOBA_TQ == 0 and MOBA_TQ % MOBA_BLOCK == 0
    n_blk = s // MOBA_BLOCK
    assert n_blk % 8 == 0 and n_blk <= LANES
    return pl.pallas_call(
        functools.partial(_moba_kernel, n_blk=n_blk),
        grid=(b, h, s // MOBA_TQ),
        in_specs=[
            pl.BlockSpec((1, MOBA_TQ, LANES), lambda bi, hi, i: (bi, i, hi)),
            pl.BlockSpec((1, s, LANES), lambda bi, hi, i: (bi, 0, h + hi)),
            pl.BlockSpec((1, s, LANES), lambda bi, hi, i: (bi, 0, 2 * h + hi)),
        ],
        out_specs=pl.BlockSpec((1, MOBA_TQ, LANES), lambda bi, hi, i: (bi, i, hi)),
        out_shape=jax.ShapeDtypeStruct((b, s, h * HEAD_DIM), BF16),
        scratch_shapes=[pltpu.VMEM((s, 2 * LANES), BF16), pltpu.VMEM((n_blk, LANES), F32),
                        pltpu.VMEM((s // MOBA_TK, HEAD_DIM + ONES_ROWS, MOBA_TK), BF16)],
        compiler_params=_cparams("parallel", "parallel", "arbitrary"),
        name="moba_attn",
    )(qkv, qkv, qkv)


NSA_TQ = 512
NSA_TK = 1024
def _cmp_kernel(*refs):
    st = NSA_CMP_STRIDE
    x_refs, (pos_ref, w_ref, ok_ref, ov_ref) = refs[:2 * st], refs[2 * st:]
    n16 = ok_ref.shape[2]
    for t, o_ref in enumerate((ok_ref, ov_ref)):
        first = jnp.zeros((n16, HEAD_DIM), F32)
        second = jnp.zeros((n16, HEAD_DIM), F32)
        for l in range(st):
            x = x_refs[t * st + l][0].astype(F32)
            first = first + _dot((x + pos_ref[t, l:l + 1, :]).astype(BF16), w_ref[t, l])
            second = second + _dot((x + pos_ref[t, st + l:st + l + 1, :]).astype(BF16), w_ref[t, st + l])
        out = first + pltpu.roll(second, n16 - 1, 0)
        o_ref[0, 0] = (out if t == 0 else out.T).astype(o_ref.dtype)


def _nsa_compress(slabs, batch, cmp_pos, w_cmp):
    g, st = NSA_KV_GROUPS, NSA_CMP_STRIDE
    b = batch
    n16 = slabs.shape[1] // b

    def slab_spec(first_block, l):
        return pl.BlockSpec((1, n16, LANES), lambda bi, gi: (l, bi, first_block + gi))

    x_specs = [slab_spec(t * g, l) for t in range(2) for l in range(st)]
    pos = cmp_pos.astype(F32)
    w = w_cmp.astype(BF16)
    return pl.pallas_call(
        _cmp_kernel,
        grid=(b, g),
        in_specs=x_specs + [_const_spec(pos.shape), _const_spec(w.shape)],
        out_specs=[pl.BlockSpec((1, 1, n16, HEAD_DIM), lambda bi, gi: (bi, gi, 0, 0)),
                   pl.BlockSpec((1, 1, HEAD_DIM, n16), lambda bi, gi: (bi, gi, 0, 0))],
        out_shape=[jax.ShapeDtypeStruct((b, g, n16, HEAD_DIM), BF16),
                   jax.ShapeDtypeStruct((b, g, HEAD_DIM, n16), BF16)],
        compiler_params=_cparams("parallel", "parallel"),
        name="nsa_compress",
    )(*([slabs] * (2 * st)), pos, w)


def _top_k_mask_t(score_t, k):
    n, q = score_t.shape
    work = score_t
    taken = jnp.zeros((1, q), F32)
    level = jnp.full((1, q), jnp.inf, F32)
    above = jnp.zeros((1, q), F32)
    for _ in range(k):
        best = jnp.max(work, axis=0, keepdims=True)
        hit = work == best
        active = taken < k
        level = jnp.where(active, best, level)
        above = jnp.where(active, taken, above)
        taken = taken + jnp.sum(hit.astype(F32), axis=0, keepdims=True)
        work = jnp.where(hit, -jnp.inf, work)
    tie = score_t == level
    lower = (lax.broadcasted_iota(jnp.int32, (n, n), 1)
             < lax.broadcasted_iota(jnp.int32, (n, n), 0)).astype(BF16)
    ties_before = _dot(lower, tie.astype(BF16))
    return (score_t > level) | (tie & (ties_before < k - above))


def _nsa_kernel(q_ref, kc_ref, vc_ref, ks_ref, vs_ref, kw_ref, vw_ref, gate_ref, ov_ref, o_ref,
                ksaug_ref, vst_ref, vwt_ref, *, n_sel):
    i = pl.program_id(2)
    tq, tk, hg = NSA_TQ, NSA_TK, NSA_HEADS_PER_GROUP
    s_len = ks_ref.shape[1]
    n_cmp_pad = kc_ref.shape[2]
    c0 = i * tq

    @pl.when(i == 0)
    def _():
        ksaug_ref[:, :LANES] = ks_ref[0]
        blk = lax.broadcasted_iota(jnp.int32, (s_len, LANES), 0) // NSA_SEL_BLOCK
        lane = lax.broadcasted_iota(jnp.int32, (s_len, LANES), 1)
        ksaug_ref[:, LANES:] = (blk == lane).astype(BF16)
        for c in range(s_len // tk):
            vst_ref[c] = _transposed_values_with_ones(vs_ref[0, c * tk:(c + 1) * tk, :])
        for c in range(s_len // tq):
            vwt_ref[c] = _transposed_values_with_ones(vw_ref[0, c * tq:(c + 1) * tq, :])

    q_all = q_ref[0]
    q_heads = [q_all[:, h * LANES:(h + 1) * LANES] for h in range(hg)]

    def key_iota(n):
        return lax.broadcasted_iota(jnp.int32, (n, tq), 0)

    def q_pos(n):
        return c0 + lax.broadcasted_iota(jnp.int32, (n, tq), 1)

    q4 = jnp.concatenate(q_heads, axis=0)

    def all_heads(mask):
        return jnp.concatenate([mask] * hg, axis=1)

    cmp_mask = all_heads(key_iota(n_cmp_pad) * NSA_CMP_STRIDE + (NSA_CMP_BLOCK - 1) <= q_pos(n_cmp_pad))
    z = jnp.where(cmp_mask, _dot_nt(kc_ref[0, 0], q4), NEG_INF)
    e = jnp.where(cmp_mask, jnp.exp2(z - jnp.max(z, axis=0, keepdims=True)), 0.0)
    l = jnp.sum(e, axis=0, keepdims=True)
    inv = 1.0 / jnp.where(l > 0.0, l, 1.0)
    o_c = _dot(vc_ref[0, 0], e.astype(BF16)) * inv
    p_c = e * inv
    p_sum = p_c[:, 0:tq]
    for h in range(1, hg):
        p_sum = p_sum + p_c[:, h * tq:(h + 1) * tq]

    ps_hi, ps_lo = _split_bf16(p_sum)
    imp = _dot(ov_ref[...], ps_hi) + _dot(ov_ref[...], ps_lo)
    blk = key_iota(LANES)
    cur = q_pos(LANES) // NSA_SEL_BLOCK
    forced = (blk == 0) | (blk == cur) | (blk == cur - 1)
    score = jnp.where(blk <= cur, imp + NSA_FORCE_BONUS * forced.astype(F32), -1.0)
    score = jnp.where(blk < n_sel, score, -jnp.inf)
    chosen = _top_k_mask_t(score, min(NSA_SEL_TOPK, n_sel))
    sel_bias = jnp.where(chosen, 0.0, NEG_INF).T.astype(BF16)
    q_aug = jnp.concatenate([q4, jnp.concatenate([sel_bias] * hg, axis=0)], axis=1)

    def keys(j):
        return ksaug_ref[pl.ds(pl.multiple_of(j * tk, tk), tk), :]

    def values_t(j):
        return vst_ref[j]

    jd = c0 // tk
    causal = jd * tk + key_iota(tk) <= q_pos(tk)
    s_t = jnp.where(all_heads(causal), _dot_nt(keys(jd), q_aug), NEG_INF)
    m0 = jnp.max(s_t, axis=0, keepdims=True)
    carry = (m0, _dot(values_t(jd), jnp.exp2(s_t - m0).astype(BF16)))
    _, acc = _flash_loop_t(jd, q_aug, keys, values_t, carry)
    o_s = acc[:HEAD_DIM] * (1.0 / acc[HEAD_DIM:HEAD_DIM + 1])

    span = tq + NSA_WINDOW
    wstart = pl.multiple_of(jnp.maximum(c0 - NSA_WINDOW, 0), tq)
    w_chunk = wstart // tq
    gap = q_pos(span) - (wstart + key_iota(span))
    win_mask = all_heads((gap >= 0) & (gap < NSA_WINDOW))
    z = jnp.where(win_mask, _dot_nt(kw_ref[0, pl.ds(wstart, span), :], q4), NEG_INF)
    e = jnp.exp2(z - jnp.max(z, axis=0, keepdims=True)).astype(BF16)
    acc_w = _dot(vwt_ref[w_chunk], e[0:tq])
    for c in range(1, span // tq):
        acc_w = acc_w + _dot(vwt_ref[w_chunk + c], e[c * tq:(c + 1) * tq])
    o_w = acc_w[:HEAD_DIM] * (1.0 / acc_w[HEAD_DIM:HEAD_DIM + 1])

    gates_t = (1.0 / (1.0 + jnp.exp(-gate_ref[0]))).T
    for h in range(hg):
        out = jnp.zeros((HEAD_DIM, tq), F32)
        for branch, o_b in enumerate((o_c, o_s, o_w)):
            r = branch * hg + h
            out = out + gates_t[r:r + 1, :] * o_b[:, h * tq:(h + 1) * tq]
        o_ref[0, :, h * LANES:(h + 1) * LANES] = out.T.astype(o_ref.dtype)


def _nsa_attention(main, gate_logits, kc_cmp, vc_cmp_t, overlap_t):
    b, s, _ = main.shape
    g, hg = NSA_KV_GROUPS, NSA_HEADS_PER_GROUP
    n_sel = s // NSA_SEL_BLOCK
    assert s % NSA_TK == 0 and n_sel <= LANES
    n16 = kc_cmp.shape[2]
    q_blocks = N_HEADS

    def kv_spec(which):
        return pl.BlockSpec((1, s, LANES), lambda bi, gi, i: (bi, 0, q_blocks + which * g + gi),
                            pipeline_mode=pl.Buffered(1))

    return pl.pallas_call(
        functools.partial(_nsa_kernel, n_sel=n_sel),
        grid=(b, g, s // NSA_TQ),
        in_specs=[
            pl.BlockSpec((1, NSA_TQ, hg * LANES), lambda bi, gi, i: (bi, i, gi)),
            pl.BlockSpec((1, 1, n16, HEAD_DIM), lambda bi, gi, i: (bi, gi, 0, 0)),
            pl.BlockSpec((1, 1, HEAD_DIM, n16), lambda bi, gi, i: (bi, gi, 0, 0)),
            kv_spec(2), kv_spec(3), kv_spec(4), kv_spec(5),
            pl.BlockSpec((1, NSA_TQ, LANES), lambda bi, gi, i: (bi, i, gi)),
            _const_spec(overlap_t.shape),
        ],
        out_specs=pl.BlockSpec((1, NSA_TQ, hg * LANES), lambda bi, gi, i: (bi, i, gi)),
        out_shape=jax.ShapeDtypeStruct((b, s, N_HEADS * HEAD_DIM), BF16),
        scratch_shapes=[pltpu.VMEM((s, 2 * LANES), BF16),
                        pltpu.VMEM((s // NSA_TK, HEAD_DIM + ONES_ROWS, NSA_TK), BF16),
                        pltpu.VMEM((s // NSA_TQ, HEAD_DIM + ONES_ROWS, NSA_TQ), BF16)],
        compiler_params=_cparams("parallel", "parallel", "arbitrary"),
        name="nsa_attn",
    )(main, kc_cmp, vc_cmp_t, main, main, main, main, gate_logits, overlap_t)


def _nsa_overlap_t(seq):
    n_cmp = (seq - NSA_CMP_BLOCK) // NSA_CMP_STRIDE + 1
    n_sel = seq // NSA_SEL_BLOCK
    cmp_start = np.arange(seq // NSA_CMP_STRIDE) * NSA_CMP_STRIDE
    sel_start = np.arange(LANES) * NSA_SEL_BLOCK
    ov = ((cmp_start[:, None] < sel_start[None, :] + NSA_SEL_BLOCK)
          & (cmp_start[:, None] + NSA_CMP_BLOCK > sel_start[None, :]))
    ov &= (np.arange(seq // NSA_CMP_STRIDE)[:, None] < n_cmp) & (np.arange(LANES)[None, :] < n_sel)
    return jnp.asarray(ov.T, dtype=BF16)


def _glu_proj_kernel(x_ref, g_ref, w_ref, o_ref):
    h = _rms(x_ref[...], g_ref[...]).astype(BF16)
    d = o_ref.shape[1]
    for c in range(d // PROJ_CHUNK):
        a = _dot(h, w_ref[:, c * PROJ_CHUNK:(c + 1) * PROJ_CHUNK])
        gate = _dot(h, w_ref[:, d + c * PROJ_CHUNK:d + (c + 1) * PROJ_CHUNK])
        o_ref[:, c * PROJ_CHUNK:(c + 1) * PROJ_CHUNK] = (a / (1.0 + jnp.exp(-gate))).astype(o_ref.dtype)


def _glu_project(x2d, gain, w):
    t_rows, d = x2d.shape
    return pl.pallas_call(
        _glu_proj_kernel,
        grid=(t_rows // ROW_TILE,),
        in_specs=[pl.BlockSpec((ROW_TILE, d), lambda i: (i, 0)), _const_spec((1, d)), _const_spec(w.shape)],
        out_specs=pl.BlockSpec((ROW_TILE, d), lambda i: (i, 0)),
        out_shape=jax.ShapeDtypeStruct((t_rows, d), BF16),
        compiler_params=_cparams("parallel"),
        name="norm_glu_proj",
    )(x2d, gain.reshape(1, d), w)


CONV_TILE = 256
SUBLANES = 8
CONV_PHASE_ROWS = CONV_TILE + CONV_HALO - SUBLANES


def _conv_kernel(u_ref, halo_ref, dw_ref, db_ref, lg_ref, lb_ref, o_ref, ext_ref, phase_ref):
    i = pl.program_id(1)
    halo = halo_ref[0].astype(F32)
    ext_ref[0:CONV_HALO, :] = jnp.where(i == 0, 0.0, halo)
    ext_ref[CONV_HALO:, :] = u_ref[0].astype(F32)
    for b in range(1, SUBLANES):
        phase_ref[b - 1] = ext_ref[pl.ds(b, CONV_PHASE_ROWS), :]
    lead = CONV_HALO - (CONV_WIDTH - 1)
    for r in range(CONV_TILE // CONV_ROWS):
        acc = jnp.zeros((CONV_ROWS, D_MODEL), F32) + db_ref[...]
        for w in range(CONV_WIDTH):
            shift = (lead + w) % SUBLANES
            start = r * CONV_ROWS + lead + w - shift
            src = ext_ref if shift == 0 else phase_ref.at[shift - 1]
            acc = acc + dw_ref[w:w + 1, :] * src[pl.ds(start, CONV_ROWS), :]
        mu = jnp.mean(acc, axis=-1, keepdims=True)
        cen = acc - mu
        var = jnp.mean(cen * cen, axis=-1, keepdims=True)
        un = cen * lax.rsqrt(var + NORM_EPS) * lg_ref[...] + lb_ref[...]
        o_ref[0, r * CONV_ROWS:(r + 1) * CONV_ROWS, :] = (un / (1.0 + jnp.exp(-un))).astype(o_ref.dtype)


def _conv_ln_swish(u, dw_w, dw_b, ln_g, ln_b):
    b, s, d = u.shape
    halo_per_tile = CONV_TILE // CONV_HALO
    dw = jnp.concatenate([dw_w, jnp.zeros((1, d), F32)], axis=0)
    return pl.pallas_call(
        _conv_kernel,
        grid=(b, s // CONV_TILE),
        in_specs=[
            pl.BlockSpec((1, CONV_TILE, d), lambda bi, i: (bi, i, 0)),
            pl.BlockSpec((1, CONV_HALO, d), lambda bi, i: (bi, jnp.maximum(i * halo_per_tile - 1, 0), 0)),
            _const_spec(dw.shape), _const_spec((1, d)), _const_spec((1, d)), _const_spec((1, d)),
        ],
        out_specs=pl.BlockSpec((1, CONV_TILE, d), lambda bi, i: (bi, i, 0)),
        out_shape=jax.ShapeDtypeStruct((b, s, d), BF16),
        scratch_shapes=[pltpu.VMEM((CONV_TILE + CONV_HALO, d), F32),
                        pltpu.VMEM((SUBLANES - 1, CONV_PHASE_ROWS, d), F32)],
        compiler_params=_cparams("parallel", "parallel"),
        name="conv_ln_swish",
    )(u, u, dw, dw_b.reshape(1, d), ln_g.reshape(1, d), ln_b.reshape(1, d))


def _rope_tables(seq):
    half = HEAD_DIM // 2
    inv_freq = ROPE_THETA ** (-jnp.arange(half, dtype=F32) / half)
    ang = jnp.arange(seq, dtype=F32)[:, None] * inv_freq[None, :]
    cos, sin = jnp.cos(ang), jnp.sin(ang)
    return jnp.concatenate([cos, cos], axis=1), jnp.concatenate([-sin, sin], axis=1)


def _nsa_mixer(x2d, norm_gain, w_in, q_gain, k_gain, cmp_pos, w_cmp, tables, batch, seq):
    g, dh = NSA_KV_GROUPS, HEAD_DIM
    n_main = (N_HEADS + 6 * g) * dh
    scale = dh ** -0.5 * LOG2_E
    w_main = w_in[:, :n_main].astype(BF16)
    hg = NSA_HEADS_PER_GROUP
    w_g = w_in[:, n_main:].reshape(-1, 3, g, hg).transpose(0, 2, 1, 3).reshape(-1, g, 3 * hg)
    w_gate = jnp.pad(w_g, ((0, 0), (0, 0), (0, LANES - 3 * hg))).reshape(-1, g * LANES).astype(BF16)
    ones = jnp.ones((dh,), F32)
    head_gains = jnp.stack([q_gain * scale] * N_HEADS + [k_gain[0]] * g + [ones] * g
                           + [k_gain[1]] * g + [ones] * g + [k_gain[2]] * g + [ones] * g)
    rope_blocks = [True] * N_HEADS + [True] * g + [False] * g + [True] * g + [False] * g + [True] * g + [False] * g
    kc_vc_chunk = N_HEADS * dh // PROJ_CHUNK
    assert 2 * g * dh == PROJ_CHUNK
    main, gate_logits, slabs = _project(x2d, norm_gain, w_main, *tables, head_gains, rope_blocks, seq,
                                        w_gate=w_gate, slab_chunk=kc_vc_chunk)
    main = main.reshape(batch, seq, n_main)
    gate_logits = gate_logits.reshape(batch, seq, g * LANES)

    kc_cmp, vc_cmp_t = _nsa_compress(slabs, batch, cmp_pos, w_cmp)
    out = _nsa_attention(main, gate_logits, kc_cmp, vc_cmp_t, _nsa_overlap_t(seq))
    return out.reshape(batch * seq, N_HEADS * dh)


def _qkv_weight_with_scaled_q(w_in):
    n_q = N_HEADS * HEAD_DIM
    return jnp.concatenate([w_in[:, :n_q] * HEAD_DIM ** -0.5, w_in[:, n_q:]], axis=1).astype(BF16)


def _sb_mixer(x2d, norm_gain, w_in, tables, batch, seq):
    n = 3 * N_HEADS * HEAD_DIM
    head_gains = jnp.ones((n // LANES, HEAD_DIM), F32)
    qkv = _project(x2d, norm_gain, _qkv_weight_with_scaled_q(w_in), *tables, head_gains,
                   [False] * (n // LANES), seq)
    return _sb_attention(qkv.reshape(batch, seq, n)).reshape(batch * seq, N_HEADS * HEAD_DIM)


def _conv_mixer(x2d, norm_gain, w_in, dw_w, dw_b, ln_g, ln_b, batch, seq):
    u = _glu_project(x2d, norm_gain, w_in.astype(BF16))
    a = _conv_ln_swish(u.reshape(batch, seq, D_MODEL), dw_w, dw_b, ln_g, ln_b)
    return a.reshape(batch * seq, D_MODEL)


def _moba_mixer(x2d, norm_gain, w_in, q_gain, k_gain, tables, batch, seq):
    n = 3 * N_HEADS * HEAD_DIM
    ones = jnp.ones((HEAD_DIM,), F32)
    head_gains = jnp.stack([q_gain * (HEAD_DIM ** -0.5 * LOG2_E)] * N_HEADS + [k_gain] * N_HEADS + [ones] * N_HEADS)
    rope_blocks = [True] * (2 * N_HEADS) + [False] * N_HEADS
    qkv = _project(x2d, norm_gain, w_in.astype(BF16), *tables, head_gains, rope_blocks, seq)
    return _moba_attention(qkv.reshape(batch, seq, n)).reshape(batch * seq, N_HEADS * HEAD_DIM)


def kernel(x, attn_norm, mlp_norm, mlp_w_up, mlp_w_down, nsa_w_in, nsa_q_norm, nsa_k_norm, nsa_cmp_pos, nsa_w_cmp, nsa_w_out, sb_w_in, sb_w_out, conv_w_in, conv_dw_w, conv_dw_b, conv_ln_g, conv_ln_b, conv_w_out, moba_w_in, moba_q_norm, moba_k_norm, moba_w_out):
    batch, seq, d = x.shape
    depth = attn_norm.shape[0]
    tables = _rope_tables(seq)
    w_up_all, w_down_all = mlp_w_up.astype(BF16), mlp_w_down.astype(BF16)
    x2d = x.reshape(batch * seq, d)
    for i in range(depth):
        m, j = i % 4, i // 4
        if m == 0:
            a = _nsa_mixer(x2d, attn_norm[i], nsa_w_in[j], nsa_q_norm[j], nsa_k_norm[j], nsa_cmp_pos[j],
                           nsa_w_cmp[j], tables, batch, seq)
            w_out = nsa_w_out[j]
        elif m == 1:
            a = _sb_mixer(x2d, attn_norm[i], sb_w_in[j], tables, batch, seq)
            w_out = sb_w_out[j]
        elif m == 2:
            a = _conv_mixer(x2d, attn_norm[i], conv_w_in[j], conv_dw_w[j], conv_dw_b[j], conv_ln_g[j],
                            conv_ln_b[j], batch, seq)
            w_out = conv_w_out[j]
        else:
            a = _moba_mixer(x2d, attn_norm[i], moba_w_in[j], moba_q_norm[j], moba_k_norm[j], tables, batch, seq)
            w_out = moba_w_out[j]
        x2d = _mixer_out_and_mlp(x2d, a, w_out.astype(BF16), mlp_norm[i], w_up_all, w_down_all, i)
    return x2d.reshape(batch, seq, d)
```
